```python
import math
import jax, jax.numpy as jnp
from jax import lax
import numpy as np

D_MODEL = 4096
BATCH = 2
SEQ = 4096
DEPTH = 1

HEAD_SIZE = 64
D_A = D_MODEL // 2
D_B = D_MODEL - D_A
H_A = D_A // HEAD_SIZE
H_Q = D_B // HEAD_SIZE
GQA_RATIO = 8
H_KV = H_Q // GQA_RATIO
GQA_GROUP = H_Q // H_KV
WINDOW = 128
BLOCK = 128
RPB_BUCKETS = 32
RPB_MAX_EXACT = RPB_BUCKETS // 2
RPB_MAX_DIST = 128
DECAY_LORA = max(32, int(round(D_A ** 0.5 * 1.8 / 32)) * 32)
ICLR_LORA = max(32, int(round(D_A ** 0.5 * 1.8 / 32)) * 32)
GATE_LORA = max(32, int(round(D_A ** 0.6 * 0.8 / 32)) * 32)
D_FF = 4 * D_MODEL
ALPHA = (2.0 * DEPTH) ** 0.25
BETA = (8.0 * DEPTH) ** -0.25
LN_EPS = 1e-5
LNX_EPS = 64e-5
OFF_R = 0
OFF_K = OFF_R + D_A
OFF_V = OFF_K + D_A
OFF_W = OFF_V + D_A
OFF_A = OFF_W + DECAY_LORA
OFF_G = OFF_A + ICLR_LORA
RWKV_COLS = OFF_G + GATE_LORA
OFF_Q = RWKV_COLS
OFF_KB = OFF_Q + D_B
OFF_VB = OFF_KB + H_KV * HEAD_SIZE
N_IN = OFF_VB + H_KV * HEAD_SIZE

kernel_name = "hybrid_rwkv7_swa_sink_block"


def layer_norm(x, g, b, eps=LN_EPS):
    xf = x.astype(jnp.float32)
    mu = xf.mean(-1, keepdims=True)
    var = jnp.square(xf - mu).mean(-1, keepdims=True)
    return ((xf - mu) * lax.rsqrt(var + eps) * g + b).astype(x.dtype)


def token_shift(p, mu):
    prev = jnp.pad(p, ((0, 0), (1, 0), (0, 0)))[:, :-1]
    return p + (prev - p) * mu


def rwkv7_recurrence(r, decay, k, v, kk, a):
    b, _, h, n = r.shape

    def step(state, inp):
        r_t, w_t, k_t, v_t, kk_t, a_t = inp
        sa = jnp.einsum('bhvk,bhk->bhv', state, -kk_t)
        state = (state * w_t[:, :, None, :]
                 + sa[..., None] * (kk_t * a_t)[:, :, None, :]
                 + v_t[..., None] * k_t[:, :, None, :])
        return state, jnp.einsum('bhvk,bhk->bhv', state, r_t)

    xs = tuple(jnp.moveaxis(t, 1, 0) for t in (r, decay, k, v, kk, a))
    s0 = jnp.zeros((b, h, n, n), jnp.float32)
    _, ys = lax.scan(step, s0, xs)
    return jnp.moveaxis(ys, 0, 1)


def rwkv7_mixer(p, mu, w0, w_decay_up, a0, w_iclr_up, w_gate_up, k_k, k_a, r_k, lnx_g, lnx_b):
    bsz, seq, _ = p.shape
    f32 = jnp.float32
    p = token_shift(p, mu)
    r = p[..., OFF_R:OFF_K]
    k = p[..., OFF_K:OFF_V]
    v = p[..., OFF_V:OFF_W]
    xw = p[..., OFF_W:OFF_A]
    xa = p[..., OFF_A:OFF_G]
    xg = p[..., OFF_G:RWKV_COLS]
    w = -jax.nn.softplus(-(w0 + jnp.tanh(xw) @ w_decay_up)) - 0.5
    decay = jnp.exp(-jnp.exp(w.astype(f32)))
    a = jax.nn.sigmoid(a0 + xa @ w_iclr_up)
    g = jax.nn.sigmoid(xg) @ w_gate_up
    heads = lambda t: t.astype(f32).reshape(bsz, seq, H_A, HEAD_SIZE)
    kk = heads(k * k_k)
    kk = kk / jnp.maximum(jnp.linalg.norm(kk, axis=-1, keepdims=True), 1e-12)
    k = k * (1 + (a - 1) * k_a)
    r_h, k_h, v_h, a_h = heads(r), heads(k), heads(v), heads(a)
    y = rwkv7_recurrence(r_h, heads(decay), k_h, v_h, kk, a_h)
    mu_y = y.mean(-1, keepdims=True)
    var_y = jnp.square(y - mu_y).mean(-1, keepdims=True)
    y = ((y - mu_y) * lax.rsqrt(var_y + LNX_EPS)).reshape(bsz, seq, D_A) * lnx_g + lnx_b
    bonus = jnp.sum(r_h * k_h * r_k, axis=-1, keepdims=True) * v_h
    y = (y + bonus.reshape(bsz, seq, D_A)) * g
    return y.astype(p.dtype)


def t5_causal_bucket(dist):
    n = jnp.maximum(dist, 0)
    nf = jnp.maximum(n, 1).astype(jnp.float32)
    large = RPB_MAX_EXACT + (jnp.log(nf / RPB_MAX_EXACT)
                             / math.log(RPB_MAX_DIST / RPB_MAX_EXACT)
                             * (RPB_BUCKETS - RPB_MAX_EXACT)).astype(jnp.int32)
    large = jnp.minimum(large, RPB_BUCKETS - 1)
    return jnp.where(n < RPB_MAX_EXACT, n, large)


def swa_sink_attention(q, k, v, rpb_table, sinks):
    bsz, seq = q.shape[:2]
    nb = seq // BLOCK
    f32 = jnp.float32
    qb = q.reshape(bsz, nb, BLOCK, H_KV, GQA_GROUP, HEAD_SIZE)

    def band(t):
        tp = jnp.pad(t, ((0, 0), (BLOCK, 0), (0, 0), (0, 0)))
        prev = tp[:, :seq].reshape(bsz, nb, BLOCK, H_KV, HEAD_SIZE)
        cur = t.reshape(bsz, nb, BLOCK, H_KV, HEAD_SIZE)
        return jnp.concatenate([prev, cur], axis=2)

    kb, vb = band(k), band(v)
    qi = jnp.arange(BLOCK)[:, None]
    kj = jnp.arange(2 * BLOCK)[None, :]
    dist = qi + BLOCK - kj
    bias = rpb_table[t5_causal_bucket(dist)]
    bias = jnp.transpose(bias, (2, 0, 1)).reshape(H_KV, GQA_GROUP, BLOCK, 2 * BLOCK).astype(f32)
    blk = jnp.arange(nb)[:, None, None]
    valid = (dist >= 0) & (dist < WINDOW) & ((blk > 0) | (kj >= BLOCK))
    s = jnp.einsum('bnqhgd,bnkhd->bnhgqk', qb, kb).astype(f32) * (HEAD_SIZE ** -0.5) + bias
    s = jnp.where(valid[None, :, None, None], s, -1e30)
    sink = sinks.astype(f32).reshape(H_KV, GQA_GROUP)[None, None, :, :, None, None]
    m = jnp.maximum(s.max(-1, keepdims=True), sink)
    e = jnp.exp(s - m)
    probs = e / (e.sum(-1, keepdims=True) + jnp.exp(sink - m))
    o = jnp.einsum('bnhgqk,bnkhd->bnqhgd', probs.astype(v.dtype), vb)
    return o.reshape(bsz, seq, H_Q * HEAD_SIZE)


def setup_inputs(seed: int = 0) -> dict:
    key = jax.random.key(seed)
    ks = jax.random.split(key, 28)
    nrm = lambda k, shape, s: s * jax.random.normal(k, shape, jnp.float32)
    L = DEPTH
    col_scale = np.ones((N_IN,), np.float32)
    col_scale[OFF_V:OFF_W] = BETA
    col_scale[OFF_VB:N_IN] = BETA
    return {
        "x": nrm(ks[0], (BATCH, SEQ, D_MODEL), 1.0),
        "c": nrm(ks[1], (BATCH, D_MODEL), 1.0),
        "ln_emb_g": 1.0 + nrm(ks[2], (D_MODEL,), 0.02),
        "ln_emb_b": nrm(ks[3], (D_MODEL,), 0.02),
        "rpb_table": nrm(ks[4], (RPB_BUCKETS, H_Q), 0.5),
        "w_mod": nrm(ks[5], (L, D_MODEL, 6 * D_MODEL), 0.2 * D_MODEL ** -0.5),
        "b_mod": nrm(ks[6], (L, 6 * D_MODEL), 0.01),
        "w_in": nrm(ks[7], (L, D_MODEL, N_IN), D_MODEL ** -0.5) * jnp.asarray(col_scale),
        "mu_shift": jax.random.uniform(ks[8], (L, RWKV_COLS), jnp.float32),
        "w0": jax.random.uniform(ks[9], (L, D_A), jnp.float32, -6.0, -1.0),
        "w_decay_up": nrm(ks[10], (L, DECAY_LORA, D_A), 0.5 * DECAY_LORA ** -0.5),
        "a0": nrm(ks[11], (L, D_A), 0.1),
        "w_iclr_up": nrm(ks[12], (L, ICLR_LORA, D_A), 0.5 * ICLR_LORA ** -0.5),
        "w_gate_up": nrm(ks[13], (L, GATE_LORA, D_A), GATE_LORA ** -0.5),
        "k_k": 0.85 + nrm(ks[14], (L, D_A), 0.05),
        "k_a": 1.0 + nrm(ks[15], (L, D_A), 0.05),
        "r_k": nrm(ks[16], (L, H_A, HEAD_SIZE), 0.1),
        "lnx_g": 1.0 + nrm(ks[17], (L, D_A), 0.02),
        "lnx_b": nrm(ks[18], (L, D_A), 0.02),
        "attn_sinks": nrm(ks[19], (L, H_Q), 1.0),
        "w_out": nrm(ks[20], (L, D_MODEL, D_MODEL), BETA * D_MODEL ** -0.5),
        "ln1_g": 1.0 + nrm(ks[21], (L, D_MODEL), 0.02),
        "ln1_b": nrm(ks[22], (L, D_MODEL), 0.02),
        "w_up": nrm(ks[23], (L, D_MODEL, D_FF), D_MODEL ** -0.5),
        "w_down": nrm(ks[24], (L, D_FF, D_MODEL), BETA * D_FF ** -0.5),
        "ln2_g": 1.0 + nrm(ks[25], (L, D_MODEL), 0.02),
        "ln2_b": nrm(ks[26], (L, D_MODEL), 0.02),
    }


def reference(x, c, ln_emb_g, ln_emb_b, rpb_table, w_mod, b_mod, w_in, mu_shift, w0,
              w_decay_up, a0, w_iclr_up, w_gate_up, k_k, k_a, r_k, lnx_g, lnx_b,
              attn_sinks, w_out, ln1_g, ln1_b, w_up, w_down, ln2_g, ln2_b):
    bsz, seq, _ = x.shape
    cond = jax.nn.silu(c)
    x = layer_norm(x, ln_emb_g, ln_emb_b)
    for l in range(DEPTH):
        mod = (cond @ w_mod[l] + b_mod[l]).reshape(bsz, 6, D_MODEL)[:, :, None, :]
        sh1, sc1, g1, sh2, sc2, g2 = (mod[:, i] for i in range(6))
        u = x * (1 + sc1) + sh1
        p = u @ w_in[l]
        y_a = rwkv7_mixer(p[..., :RWKV_COLS], mu_shift[l], w0[l], w_decay_up[l], a0[l],
                          w_iclr_up[l], w_gate_up[l], k_k[l], k_a[l], r_k[l],
                          lnx_g[l], lnx_b[l])
        q = p[..., OFF_Q:OFF_KB].reshape(bsz, seq, H_Q, HEAD_SIZE)
        kb = p[..., OFF_KB:OFF_VB].reshape(bsz, seq, H_KV, HEAD_SIZE)
        vb = p[..., OFF_VB:N_IN].reshape(bsz, seq, H_KV, HEAD_SIZE)
        y_b = swa_sink_attention(q, kb, vb, rpb_table, attn_sinks[l])
        mix = jnp.concatenate([y_a, y_b], axis=-1) @ w_out[l]
        x = layer_norm(ALPHA * x + (1 + g1) * mix, ln1_g[l], ln1_b[l])
        u = x * (1 + sc2) + sh2
        h = jnp.square(jax.nn.relu(u @ w_up[l])) @ w_down[l]
        x = layer_norm(ALPHA * x + (1 + g2) * h, ln2_g[l], ln2_b[l])
    return x
```

```python
import functools
import math

import numpy as np
import jax
import jax.numpy as jnp
from jax import lax
from jax.experimental import pallas as pl
from jax.experimental.pallas import tpu as pltpu

F32 = jnp.float32
BF16 = jnp.bfloat16

D_MODEL = 4096
HEAD = 64
D_A = D_MODEL // 2
D_B = D_MODEL - D_A
H_A = D_A // HEAD
H_Q = D_B // HEAD
GQA = 8
H_KV = H_Q // GQA
WINDOW = 128
BLOCK = 128
RPB_BUCKETS = 32
RPB_MAX_EXACT = RPB_BUCKETS // 2
RPB_MAX_DIST = 128
DECAY_LORA = max(32, int(round(D_A ** 0.5 * 1.8 / 32)) * 32)
ICLR_LORA = max(32, int(round(D_A ** 0.5 * 1.8 / 32)) * 32)
GATE_LORA = max(32, int(round(D_A ** 0.6 * 0.8 / 32)) * 32)
LORA_COLS = DECAY_LORA + ICLR_LORA + GATE_LORA
D_FF = 4 * D_MODEL
DEPTH = 1
ALPHA = (2.0 * DEPTH) ** 0.25
LN_EPS = 1e-5
LNX_EPS = 64e-5
OFF_W = 3 * D_A
RWKV_COLS = OFF_W + LORA_COLS
OFF_Q = RWKV_COLS
OFF_KB = OFF_Q + D_B
OFF_VB = OFF_KB + H_KV * HEAD
N_IN = OFF_VB + H_KV * HEAD
NEG = -1e30

CHUNK = 64
PAIR = 2 * HEAD
VMEM_CAP = 56 * 1024 * 1024


def _cparams(sem, vmem_mb):
    return pltpu.CompilerParams(dimension_semantics=sem,
                                vmem_limit_bytes=min(int(vmem_mb * 1024 * 1024), VMEM_CAP))


def _dot(a, b):
    return jnp.dot(a.astype(BF16), b.astype(BF16), preferred_element_type=F32)


def _dot_nt(a, b):
    return lax.dot_general(a.astype(BF16), b.astype(BF16), (((1,), (1,)), ((), ())),
                           preferred_element_type=F32)


def _split2(x):
    hi = x.astype(BF16)
    lo = (x - hi.astype(F32)).astype(BF16)
    return hi, lo


def _dot_exact_rhs(x, rhs_bf16):
    hi, lo = _split2(x)
    return (jnp.dot(hi, rhs_bf16, preferred_element_type=F32)
            + jnp.dot(lo, rhs_bf16, preferred_element_type=F32))


def _dot3(a, b):
    ah, al = _split2(a)
    bh, bl = _split2(b)
    return (jnp.dot(ah, bh, preferred_element_type=F32)
            + jnp.dot(ah, bl, preferred_element_type=F32)
            + jnp.dot(al, bh, preferred_element_type=F32))


def _layer_norm(x, g, b):
    mu = jnp.mean(x, axis=-1, keepdims=True)
    xc = x - mu
    var = jnp.mean(xc * xc, axis=-1, keepdims=True)
    return xc * lax.rsqrt(var + LN_EPS) * g + b


def _mod_kernel(c_ref, w_ref, b_ref, o_ref):
    c = c_ref[...]
    cond = c * jax.nn.sigmoid(c)
    o_ref[...] = _dot3(cond, w_ref[...]) + b_ref[...]


def _modulation(c8, w_mod, b_mod, tn=512):
    d, n = w_mod.shape
    return pl.pallas_call(
        _mod_kernel,
        grid=(n // tn,),
        in_specs=[pl.BlockSpec((8, d), lambda j: (0, 0)),
                  pl.BlockSpec((d, tn), lambda j: (0, j)),
                  pl.BlockSpec((1, tn), lambda j: (0, j))],
        out_specs=pl.BlockSpec((8, tn), lambda j: (0, j)),
        out_shape=jax.ShapeDtypeStruct((8, n), F32),
        compiler_params=_cparams(("parallel",), 40),
        name="modulation",
    )(c8, w_mod, b_mod)


def _ln_mod_kernel(x_ref, g_ref, b_ref, mod_ref, u_ref):
    xn = _layer_norm(x_ref[...], g_ref[...], b_ref[...])
    u_ref[...] = (xn * (1.0 + mod_ref[1:2, :]) + mod_ref[0:1, :]).astype(u_ref.dtype)


def _ln_mod(x2, g, b, mod, seq, tr=256):
    t, d = x2.shape
    per = seq // tr
    return pl.pallas_call(
        _ln_mod_kernel,
        grid=(t // tr,),
        in_specs=[pl.BlockSpec((tr, d), lambda i: (i, 0)),
                  pl.BlockSpec((1, d), lambda i: (0, 0)),
                  pl.BlockSpec((1, d), lambda i: (0, 0)),
                  pl.BlockSpec((None, 6, d), lambda i: (i // per, 0, 0))],
        out_specs=pl.BlockSpec((tr, d), lambda i: (i, 0)),
        out_shape=jax.ShapeDtypeStruct((t, d), BF16),
        compiler_params=_cparams(("parallel",), 32),
        name="ln_mod",
    )(x2, g, b, mod)


def _mm_kernel(a_ref, b_ref, o_ref, *, relu2):
    acc = jnp.dot(a_ref[...], b_ref[...], preferred_element_type=F32)
    if relu2:
        acc = jnp.square(jnp.maximum(acc, 0.0))
    o_ref[...] = acc.astype(o_ref.dtype)


def _mm_acc_kernel(a_ref, b_ref, o_ref, acc_ref):
    k = pl.program_id(2)

    @pl.when(k == 0)
    def _():
        acc_ref[...] = jnp.zeros_like(acc_ref)

    acc_ref[...] += jnp.dot(a_ref[...], b_ref[...], preferred_element_type=F32)

    @pl.when(k == pl.num_programs(2) - 1)
    def _():
        o_ref[...] = acc_ref[...].astype(o_ref.dtype)


def _matmul(a, b, *, tm, tn, tk=None, out_dtype=F32, relu2=False, name="matmul"):
    m, kd = a.shape
    _, n = b.shape
    tm = min(tm, m)
    osz = jnp.dtype(out_dtype).itemsize
    if tk is None or tk >= kd:
        vm = (2 * tm * kd * 2 + 2 * kd * tn * 2 + 2 * tm * tn * osz + tm * tn * 4) / 2 ** 20 + 8
        return pl.pallas_call(
            functools.partial(_mm_kernel, relu2=relu2),
            grid=(m // tm, n // tn),
            in_specs=[pl.BlockSpec((tm, kd), lambda i, j: (i, 0)),
                      pl.BlockSpec((kd, tn), lambda i, j: (0, j))],
            out_specs=pl.BlockSpec((tm, tn), lambda i, j: (i, j)),
            out_shape=jax.ShapeDtypeStruct((m, n), out_dtype),
            compiler_params=_cparams(("parallel", "parallel"), vm),
            name=name,
        )(a, b)
    assert not relu2
    vm = (2 * tm * tk * 2 + 2 * tk * tn * 2 + 2 * tm * tn * osz + 2 * tm * tn * 4) / 2 ** 20 + 8
    return pl.pallas_call(
        _mm_acc_kernel,
        grid=(m // tm, n // tn, kd // tk),
        in_specs=[pl.BlockSpec((tm, tk), lambda i, j, k: (i, k)),
                  pl.BlockSpec((tk, tn), lambda i, j, k: (k, j))],
        out_specs=pl.BlockSpec((tm, tn), lambda i, j, k: (i, j)),
        out_shape=jax.ShapeDtypeStruct((m, n), out_dtype),
        scratch_shapes=[pltpu.VMEM((tm, tn), F32)],
        compiler_params=_cparams(("parallel", "parallel", "arbitrary"), vm),
        name=name,
    )(a, b)


def _head_ones(n):
    r = lax.broadcasted_iota(jnp.int32, (n, n), 0)
    c = lax.broadcasted_iota(jnp.int32, (n, n), 1)
    return ((r >> 6) == (c >> 6)).astype(BF16)


def _prep_kernel(pr_ref, pk_ref, pv_ref, pc_ref, qr_ref, qk_ref, qv_ref, qc_ref,
                 mr_ref, mk_ref, mv_ref, mc_ref, w0_ref, a0_ref, kk_ref, ka_ref, wl_ref,
                 r_ref, k_ref, v_ref, al_ref, be_ref, lw_ref, g_ref, *, per):
    tb, cb = pr_ref.shape
    first = (pl.program_id(0) % per) == 0

    def shift(x_ref, q_ref, m_ref):
        x = x_ref[...]
        width = x.shape[1]
        last = jnp.where(first, 0.0, q_ref[7:8, :])
        row = lax.broadcasted_iota(jnp.int32, (tb, width), 0)
        prev = jnp.where(row == 0, jnp.broadcast_to(last, (tb, width)), pltpu.roll(x, 1, axis=0))
        return x + (prev - x) * m_ref[...]

    r = shift(pr_ref, qr_ref, mr_ref)
    k = shift(pk_ref, qk_ref, mk_ref)
    v = shift(pv_ref, qv_ref, mv_ref)
    code = shift(pc_ref, qc_ref, mc_ref)
    lane = lax.broadcasted_iota(jnp.int32, code.shape, 1)
    act = jnp.where(lane < DECAY_LORA, jnp.tanh(code),
                    jnp.where(lane < DECAY_LORA + ICLR_LORA, code, jax.nn.sigmoid(code)))
    up = _dot3(act, wl_ref[...])
    dw = up[:, 0:cb]
    da = up[:, cb:2 * cb]
    g = up[:, 2 * cb:3 * cb]
    z = -(w0_ref[...] + dw)
    w = -(jnp.maximum(z, 0.0) + jnp.log1p(jnp.exp(-jnp.abs(z)))) - 0.5
    a = jax.nn.sigmoid(a0_ref[...] + da)
    kk = k * kk_ref[...]
    ss = _dot_exact_rhs(kk * kk, _head_ones(cb))
    kk = kk / jnp.maximum(jnp.sqrt(ss), 1e-12)
    r_ref[...] = r
    k_ref[...] = k * (1.0 + (a - 1.0) * ka_ref[...])
    v_ref[...] = v
    al_ref[...] = -kk
    be_ref[...] = kk * a
    lw_ref[...] = -jnp.exp(w)
    g_ref[...] = g


def _rwkv_prep(p, mu, w0, a0, k_k, k_a, w_lora, seq, tb=256, cb=256):
    t = p.shape[0]
    nb = D_A // cb
    cblk = OFF_W // LORA_COLS
    per = seq // tb
    rows8 = tb // 8

    def cur(off):
        return pl.BlockSpec((tb, cb), lambda i, j: (i, off + j))

    def prev(off):
        return pl.BlockSpec((8, cb), lambda i, j: (jnp.maximum(i * rows8 - 1, 0), off + j))

    def vec(off):
        return pl.BlockSpec((1, cb), lambda i, j: (0, off + j))

    in_specs = [cur(0), cur(nb), cur(2 * nb),
                pl.BlockSpec((tb, LORA_COLS), lambda i, j: (i, cblk)),
                prev(0), prev(nb), prev(2 * nb),
                pl.BlockSpec((8, LORA_COLS), lambda i, j: (jnp.maximum(i * rows8 - 1, 0), cblk)),
                vec(0), vec(nb), vec(2 * nb),
                pl.BlockSpec((1, LORA_COLS), lambda i, j: (0, cblk)),
                vec(0), vec(0), vec(0), vec(0),
                pl.BlockSpec((None, LORA_COLS, 3 * cb), lambda i, j: (j, 0, 0))]
    out = jax.ShapeDtypeStruct((t, D_A), F32)
    return pl.pallas_call(
        functools.partial(_prep_kernel, per=per),
        grid=(t // tb, nb),
        in_specs=in_specs,
        out_specs=[pl.BlockSpec((tb, cb), lambda i, j: (i, j))] * 7,
        out_shape=[out] * 7,
        compiler_params=_cparams(("parallel", "parallel"), 32),
        name="rwkv_prep",
    )(p, p, p, p, p, p, p, p, mu, mu, mu, mu, w0, a0, k_k, k_a, w_lora)


def _scan_pair_chunk(r, k, v, al, be, lw, ht, consts):
    m0, strict, incl, blk16, blk32, eye, tri = consts
    hi = lw.astype(BF16)
    rem = lw - hi.astype(F32)
    mid = rem.astype(BF16)
    lo = (rem - mid.astype(F32)).astype(BF16)
    c3 = jnp.dot(tri, jnp.concatenate([hi, mid, lo], axis=1), preferred_element_type=F32)
    c = c3[:, 0:PAIR] + c3[:, PAIR:2 * PAIR] + c3[:, 2 * PAIR:3 * PAIR]
    pc = jnp.exp(c[CHUNK - 1:CHUNK, :])
    einv = jnp.exp(-c)
    a_t = al * jnp.exp(c - lw)
    r_t = r * jnp.exp(c)
    b_t = be * einv
    k_t = k * einv

    def sm(x):
        return jnp.concatenate([jnp.where(m0, x, 0.0), jnp.where(m0, 0.0, x)], axis=0)

    a_sm, r_sm, v_sm = sm(a_t), sm(r_t), sm(v)
    s = _dot_nt(jnp.concatenate([a_sm, r_sm], axis=0),
                jnp.concatenate([b_t, b_t, k_t, k_t], axis=0))
    lab = jnp.where(strict, s[0:PAIR, 0:PAIR], 0.0)
    mak = jnp.where(strict, s[0:PAIR, PAIR:], 0.0)
    mrb = jnp.where(incl, s[PAIR:, 0:PAIR], 0.0)
    mrk = jnp.where(incl, s[PAIR:, PAIR:], 0.0)

    ld = jnp.where(blk16, lab, 0.0)
    x = jnp.where(eye, 1.0, ld)
    l2 = _dot(ld, ld)
    x = x + _dot(x, l2)
    l4 = _dot(l2, l2)
    x = x + _dot(x, l4)
    l8 = _dot(l4, l4)
    x = x + _dot(x, l8)
    e1 = jnp.where(jnp.logical_and(blk32, jnp.logical_not(blk16)), lab, 0.0)
    x = x + _dot(x, _dot(e1, x))
    e2 = jnp.where(blk32, 0.0, lab)
    x = x + _dot(x, _dot(e2, x))

    makv = _dot(mak, v_sm)
    wu = _dot(x, jnp.concatenate([a_sm, makv], axis=1))
    w_sm = wu[:, 0:PAIR]
    u_sm = wu[:, PAIR:]
    bigr = jnp.concatenate([wu, jnp.concatenate([jnp.zeros_like(v_sm), v_sm], axis=1)], axis=0)
    bk = jnp.concatenate([sm(b_t * pc), sm(k_t * pc)], axis=0)
    gz = _dot(bigr.T, bk)
    qy = _dot(jnp.concatenate([mrb, mrk], axis=1), bigr)
    q_sm = r_sm + qy[:, 0:PAIR]
    hh, hl = _split2(ht)
    gt = gz[0:PAIR, :].astype(BF16)
    ht_new = (ht * pc + jnp.dot(hh, gt, preferred_element_type=F32)
              + jnp.dot(hl, gt, preferred_element_type=F32) + gz[PAIR:, :])
    qb = q_sm.astype(BF16)
    nt = (((1,), (1,)), ((), ()))
    y_sm = (lax.dot_general(qb, hh, nt, preferred_element_type=F32)
            + lax.dot_general(qb, hl, nt, preferred_element_type=F32) + qy[:, PAIR:])
    y = y_sm[0:CHUNK, :] + y_sm[CHUNK:, :]
    return y, ht_new


def _scan_kernel(r_ref, k_ref, v_ref, al_ref, be_ref, lw_ref, g_ref, rk_ref, lg_ref, lb_ref,
                 o_ref, h_ref, y_ref, *, npair, nchunk):
    @pl.when(pl.program_id(2) == 0)
    def _():
        h_ref[...] = jnp.zeros_like(h_ref)

    row = lax.broadcasted_iota(jnp.int32, (PAIR, PAIR), 0)
    col = lax.broadcasted_iota(jnp.int32, (PAIR, PAIR), 1)
    same = (row >> 6) == (col >> 6)
    tr_ = row & (CHUNK - 1)
    tc_ = col & (CHUNK - 1)
    strict = jnp.logical_and(same, tc_ < tr_)
    incl = jnp.logical_and(same, tc_ <= tr_)
    blk16 = (row >> 4) == (col >> 4)
    blk32 = (row >> 5) == (col >> 5)
    eye = row == col
    m0 = lax.broadcasted_iota(jnp.int32, (CHUNK, PAIR), 1) < HEAD
    tri = (lax.broadcasted_iota(jnp.int32, (CHUNK, CHUNK), 1)
           <= lax.broadcasted_iota(jnp.int32, (CHUNK, CHUNK), 0)).astype(BF16)
    consts = (m0, strict, incl, blk16, blk32, eye, tri)

    def chunk_body(ci, carry):
        rows = pl.ds(pl.multiple_of(ci * CHUNK, CHUNK), CHUNK)
        for p in range(npair):
            lanes = slice(p * PAIR, (p + 1) * PAIR)
            y, hn = _scan_pair_chunk(r_ref[rows, lanes], k_ref[rows, lanes], v_ref[rows, lanes],
                                     al_ref[rows, lanes], be_ref[rows, lanes], lw_ref[rows, lanes],
                                     h_ref[p], consts)
            h_ref[p] = hn
            y_ref[rows, lanes] = y
        return carry

    lax.fori_loop(0, nchunk, chunk_body, 0)

    width = npair * PAIR
    ones = _head_ones(width)
    y = y_ref[...]
    mean = _dot_exact_rhs(y, ones) * (1.0 / HEAD)
    yc = y - mean
    var = _dot_exact_rhs(yc * yc, ones) * (1.0 / HEAD)
    yn = yc * lax.rsqrt(var + LNX_EPS) * lg_ref[...] + lb_ref[...]
    rk = _dot_exact_rhs(r_ref[...] * k_ref[...] * rk_ref[...], ones)
    o_ref[...] = ((yn + rk * v_ref[...]) * g_ref[...]).astype(o_ref.dtype)


def _rwkv_scan(r, k, v, al, be, lw, g, r_k, lnx_g, lnx_b, bsz, seq, tb=256, npair=4):
    t = r.shape[0]
    width = npair * PAIR
    ngrp = D_A // width
    per = seq // tb
    blk = pl.BlockSpec((tb, width), lambda b, j, s: (b * per + s, j))
    vec = pl.BlockSpec((1, width), lambda b, j, s: (0, j))
    return pl.pallas_call(
        functools.partial(_scan_kernel, npair=npair, nchunk=tb // CHUNK),
        grid=(bsz, ngrp, per),
        in_specs=[blk] * 7 + [vec] * 3,
        out_specs=blk,
        out_shape=jax.ShapeDtypeStruct((t, D_A), BF16),
        scratch_shapes=[pltpu.VMEM((npair, PAIR, PAIR), F32), pltpu.VMEM((tb, width), F32)],
        compiler_params=_cparams(("parallel", "parallel", "arbitrary"), 32),
        name="rwkv_scan",
    )(r, k, v, al, be, lw, g, r_k, lnx_g, lnx_b)


def _bucket_table():
    qi = np.arange(BLOCK)[:, None]
    kj = np.arange(2 * BLOCK)[None, :]
    dist = qi + BLOCK - kj
    n = np.maximum(dist, 0)
    nf = np.maximum(n, 1).astype(np.float32)
    large = RPB_MAX_EXACT + (np.log(nf / np.float32(RPB_MAX_EXACT))
                             / np.float32(math.log(RPB_MAX_DIST / RPB_MAX_EXACT))
                             * np.float32(RPB_BUCKETS - RPB_MAX_EXACT)).astype(np.int32)
    large = np.minimum(large, RPB_BUCKETS - 1)
    bucket = np.where(n < RPB_MAX_EXACT, n, large)
    valid = (dist >= 0) & (dist < WINDOW)
    return np.where(valid, bucket, -1).astype(np.int32)


def _bias_kernel(tab_ref, bkt_ref, o_ref):
    h = pl.program_id(0)
    bkt = bkt_ref[...]
    acc = jnp.full(bkt.shape, NEG, F32)
    for b in range(RPB_BUCKETS):
        acc = jnp.where(bkt == b, tab_ref[b, h], acc)
    o_ref[...] = acc


def _attn_bias(rpb_table):
    bkt = jnp.asarray(_bucket_table())
    return pl.pallas_call(
        _bias_kernel,
        grid=(H_Q,),
        in_specs=[pl.BlockSpec(memory_space=pltpu.SMEM),
                  pl.BlockSpec((BLOCK, 2 * BLOCK), lambda h: (0, 0))],
        out_specs=pl.BlockSpec((None, BLOCK, 2 * BLOCK), lambda h: (h, 0, 0)),
        out_shape=jax.ShapeDtypeStruct((H_Q, BLOCK, 2 * BLOCK), F32),
        compiler_params=_cparams(("arbitrary",), 16),
        name="attn_bias",
    )(rpb_table, bkt)


def _swa_kernel(sink_ref, q_ref, kc_ref, kp_ref, vc_ref, vp_ref, bias_ref, o_ref):
    n = pl.program_id(1)
    col = lax.broadcasted_iota(jnp.int32, (BLOCK, 2 * BLOCK), 1)
    edge = jnp.where(jnp.logical_and(n == 0, col < BLOCK), NEG, 0.0)
    scale = HEAD ** -0.5
    for h in range(H_Q):
        gsl = slice((h // GQA) * HEAD, (h // GQA + 1) * HEAD)
        if h % GQA == 0:
            kg = jnp.concatenate([kp_ref[:, gsl], kc_ref[:, gsl]], axis=0).astype(BF16)
            vg = jnp.concatenate([vp_ref[:, gsl], vc_ref[:, gsl]], axis=0).astype(BF16)
        qh = (q_ref[:, h * HEAD:(h + 1) * HEAD] * scale).astype(BF16)
        s = _dot_nt(qh, kg) + bias_ref[h] + edge
        sink = sink_ref[0, h]
        m = jnp.maximum(jnp.max(s, axis=-1, keepdims=True), sink)
        e = jnp.exp(s - m)
        den = jnp.sum(e, axis=-1, keepdims=True) + jnp.exp(sink - m)
        probs = e / den
        o = jnp.dot(probs.astype(BF16), vg, preferred_element_type=F32)
        o_ref[:, h * HEAD:(h + 1) * HEAD] = o.astype(o_ref.dtype)


def _swa(q, kv, bias, sinks, bsz, seq):
    t = q.shape[0]
    nb = seq // BLOCK
    kvw = H_KV * HEAD

    def cur(c):
        return pl.BlockSpec((BLOCK, kvw), lambda b, n: (b * nb + n, c))

    def prev(c):
        return pl.BlockSpec((BLOCK, kvw), lambda b, n: (b * nb + jnp.maximum(n - 1, 0), c))

    return pl.pallas_call(
        _swa_kernel,
        grid=(bsz, nb),
        in_specs=[pl.BlockSpec(memory_space=pltpu.SMEM),
                  pl.BlockSpec((BLOCK, D_B), lambda b, n: (b * nb + n, 0)),
                  cur(0), prev(0), cur(1), prev(1),
                  pl.BlockSpec((H_Q, BLOCK, 2 * BLOCK), lambda b, n: (0, 0, 0))],
        out_specs=pl.BlockSpec((BLOCK, D_B), lambda b, n: (b * nb + n, 0)),
        out_shape=jax.ShapeDtypeStruct((t, D_B), BF16),
        compiler_params=_cparams(("parallel", "arbitrary"), 32),
        name="swa",
    )(sinks, q, kv, kv, kv, kv, bias)


def _post_mix_kernel(x_ref, mix_ref, ge_ref, be_ref, g1_ref, b1_ref, mod_ref, x1_ref, u_ref):
    xn = _layer_norm(x_ref[...], ge_ref[...], be_ref[...])
    z = ALPHA * xn + (1.0 + mod_ref[2:3, :]) * mix_ref[...]
    x1 = _layer_norm(z, g1_ref[...], b1_ref[...])
    x1_ref[...] = x1
    u_ref[...] = (x1 * (1.0 + mod_ref[4:5, :]) + mod_ref[3:4, :]).astype(u_ref.dtype)


def _post_mix(x2, mix, ge, be, g1, b1, mod, seq, tr=128):
    t, d = x2.shape
    per = seq // tr
    row = pl.BlockSpec((tr, d), lambda i: (i, 0))
    vec = pl.BlockSpec((1, d), lambda i: (0, 0))
    return pl.pallas_call(
        _post_mix_kernel,
        grid=(t // tr,),
        in_specs=[row, row, vec, vec, vec, vec,
                  pl.BlockSpec((None, 6, d), lambda i: (i // per, 0, 0))],
        out_specs=[row, row],
        out_shape=[jax.ShapeDtypeStruct((t, d), F32), jax.ShapeDtypeStruct((t, d), BF16)],
        compiler_params=_cparams(("parallel",), 32),
        name="post_mix",
    )(x2, mix, ge, be, g1, b1, mod)


def _final_kernel(x1_ref, h_ref, g2_ref, b2_ref, mod_ref, o_ref):
    z = ALPHA * x1_ref[...] + (1.0 + mod_ref[5:6, :]) * h_ref[...]
    o_ref[...] = _layer_norm(z, g2_ref[...], b2_ref[...])


def _final(x1, h, g2, b2, mod, seq, tr=128):
    t, d = x1.shape
    per = seq // tr
    row = pl.BlockSpec((tr, d), lambda i: (i, 0))
    vec = pl.BlockSpec((1, d), lambda i: (0, 0))
    return pl.pallas_call(
        _final_kernel,
        grid=(t // tr,),
        in_specs=[row, row, vec, vec, pl.BlockSpec((None, 6, d), lambda i: (i // per, 0, 0))],
        out_specs=row,
        out_shape=jax.ShapeDtypeStruct((t, d), F32),
        compiler_params=_cparams(("parallel",), 32),
        name="final_ln",
    )(x1, h, g2, b2, mod)


def _lora_weights(w_decay_up, w_iclr_up, w_gate_up, cb):
    zd = jnp.zeros((LORA_COLS, D_A), F32)
    wd = zd.at[0:DECAY_LORA].set(w_decay_up)
    wa = zd.at[DECAY_LORA:DECAY_LORA + ICLR_LORA].set(w_iclr_up)
    wg = zd.at[DECAY_LORA + ICLR_LORA:].set(w_gate_up)
    nb = D_A // cb
    parts = [w.reshape(LORA_COLS, nb, cb) for w in (wd, wa, wg)]
    return jnp.transpose(jnp.concatenate(parts, axis=2), (1, 0, 2))


def kernel(x, c, ln_emb_g, ln_emb_b, rpb_table, w_mod, b_mod, w_in, mu_shift, w0, w_decay_up, a0,
           w_iclr_up, w_gate_up, k_k, k_a, r_k, lnx_g, lnx_b, attn_sinks, w_out, ln1_g, ln1_b,
           w_up, w_down, ln2_g, ln2_b):
    bsz, seq, d = x.shape
    assert w_mod.shape[0] == DEPTH == 1 and d == D_MODEL and bsz <= 8
    t = bsz * seq
    row = lambda a: a.reshape(1, -1)
    x2 = x.reshape(t, d)
    c8 = jnp.pad(c, ((0, 8 - bsz), (0, 0)))
    bias = _attn_bias(rpb_table)
    mod = _modulation(c8, w_mod[0], row(b_mod[0]))[:bsz].reshape(bsz, 6, d)
    u1 = _ln_mod(x2, row(ln_emb_g), row(ln_emb_b), mod, seq)
    wi = w_in[0].astype(BF16)
    p_a = _matmul(u1, wi[:, :RWKV_COLS], tm=512, tn=1280, name="in_proj_rwkv")
    q = _matmul(u1, wi[:, OFF_Q:OFF_KB], tm=1024, tn=1024, name="in_proj_q")
    kv = _matmul(u1, wi[:, OFF_KB:], tm=1024, tn=512, name="in_proj_kv")
    w_lora = _lora_weights(w_decay_up[0], w_iclr_up[0], w_gate_up[0], 256)
    r, k, v, al, be, lw, g = _rwkv_prep(p_a, row(mu_shift[0]), row(w0[0]), row(a0[0]),
                                        row(k_k[0]), row(k_a[0]), w_lora, seq)
    y_a = _rwkv_scan(r, k, v, al, be, lw, g, row(r_k[0]), row(lnx_g[0]), row(lnx_b[0]),
                     bsz, seq)
    y_b = _swa(q, kv, bias, row(attn_sinks[0]), bsz, seq)
    mix = _matmul(jnp.concatenate([y_a, y_b], axis=1), w_out[0].astype(BF16),
                  tm=1024, tn=1024, name="out_proj")
    x1, u2 = _post_mix(x2, mix, row(ln_emb_g), row(ln_emb_b), row(ln1_g[0]), row(ln1_b[0]),
                       mod, seq)
    hmid = _matmul(u2, w_up[0].astype(BF16), tm=1024, tn=1024, out_dtype=BF16, relu2=True,
                   name="mlp_up")
    hout = _matmul(hmid, w_down[0].astype(BF16), tm=1024, tn=1024, tk=4096, name="mlp_down")
    out = _final(x1, hout, row(ln2_g[0]), row(ln2_b[0]), mod, seq)
    return out.reshape(bsz, seq, d)
```

```python
import functools
import math

import numpy as np
import jax
import jax.numpy as jnp
from jax import lax
from jax.experimental import pallas as pl
from jax.experimental.pallas import tpu as pltpu

F32 = jnp.float32
BF16 = jnp.bfloat16

D_MODEL = 4096
HEAD = 64
D_A = D_MODEL // 2
D_B = D_MODEL - D_A
H_A = D_A // HEAD
H_Q = D_B // HEAD
GQA = 8
H_KV = H_Q // GQA
WINDOW = 128
BLOCK = 128
RPB_BUCKETS = 32
RPB_MAX_EXACT = RPB_BUCKETS // 2
RPB_MAX_DIST = 128
DECAY_LORA = max(32, int(round(D_A ** 0.5 * 1.8 / 32)) * 32)
ICLR_LORA = max(32, int(round(D_A ** 0.5 * 1.8 / 32)) * 32)
GATE_LORA = max(32, int(round(D_A ** 0.6 * 0.8 / 32)) * 32)
LORA_COLS = DECAY_LORA + ICLR_LORA + GATE_LORA
D_FF = 4 * D_MODEL
DEPTH = 1
ALPHA = (2.0 * DEPTH) ** 0.25
LN_EPS = 1e-5
LNX_EPS = 64e-5
OFF_W = 3 * D_A
RWKV_COLS = OFF_W + LORA_COLS
OFF_Q = RWKV_COLS
OFF_KB = OFF_Q + D_B
OFF_VB = OFF_KB + H_KV * HEAD
N_IN = OFF_VB + H_KV * HEAD
NEG = -1e30

CHUNK = 64
PAIR = 2 * HEAD
VMEM_CAP = 56 * 1024 * 1024


def _cparams(sem, vmem_mb):
    return pltpu.CompilerParams(dimension_semantics=sem,
                                vmem_limit_bytes=min(int(vmem_mb * 1024 * 1024), VMEM_CAP))


def _dot(a, b):
    return jnp.dot(a.astype(BF16), b.astype(BF16), preferred_element_type=F32)


def _dot_nt(a, b):
    return lax.dot_general(a.astype(BF16), b.astype(BF16), (((1,), (1,)), ((), ())),
                           preferred_element_type=F32)


def _split2(x):
    hi = x.astype(BF16)
    lo = (x - hi.astype(F32)).astype(BF16)
    return hi, lo


def _dot_exact_rhs(x, rhs_bf16):
    hi, lo = _split2(x)
    return (jnp.dot(hi, rhs_bf16, preferred_element_type=F32)
            + jnp.dot(lo, rhs_bf16, preferred_element_type=F32))


def _dot3(a, b):
    ah, al = _split2(a)
    bh, bl = _split2(b)
    return (jnp.dot(ah, bh, preferred_element_type=F32)
            + jnp.dot(ah, bl, preferred_element_type=F32)
            + jnp.dot(al, bh, preferred_element_type=F32))


def _layer_norm(x, g, b):
    mu = jnp.mean(x, axis=-1, keepdims=True)
    xc = x - mu
    var = jnp.mean(xc * xc, axis=-1, keepdims=True)
    return xc * lax.rsqrt(var + LN_EPS) * g + b


def _mod_kernel(c_ref, w_ref, b_ref, o_ref):
    c = c_ref[...]
    cond = c * jax.nn.sigmoid(c)
    o_ref[...] = _dot3(cond, w_ref[...]) + b_ref[...]


def _modulation(c8, w_mod, b_mod, tn=512):
    d, n = w_mod.shape
    return pl.pallas_call(
        _mod_kernel,
        grid=(n // tn,),
        in_specs=[pl.BlockSpec((8, d), lambda j: (0, 0)),
                  pl.BlockSpec((d, tn), lambda j: (0, j)),
                  pl.BlockSpec((1, tn), lambda j: (0, j))],
        out_specs=pl.BlockSpec((8, tn), lambda j: (0, j)),
        out_shape=jax.ShapeDtypeStruct((8, n), F32),
        compiler_params=_cparams(("parallel",), 40),
        name="modulation",
    )(c8, w_mod, b_mod)


def _ln_mod_kernel(x_ref, g_ref, b_ref, mod_ref, u_ref):
    xn = _layer_norm(x_ref[...], g_ref[...], b_ref[...])
    u_ref[...] = (xn * (1.0 + mod_ref[1:2, :]) + mod_ref[0:1, :]).astype(u_ref.dtype)


def _ln_mod(x2, g, b, mod, seq, tr=256):
    t, d = x2.shape
    per = seq // tr
    return pl.pallas_call(
        _ln_mod_kernel,
        grid=(t // tr,),
        in_specs=[pl.BlockSpec((tr, d), lambda i: (i, 0)),
                  pl.BlockSpec((1, d), lambda i: (0, 0)),
                  pl.BlockSpec((1, d), lambda i: (0, 0)),
                  pl.BlockSpec((None, 6, d), lambda i: (i // per, 0, 0))],
        out_specs=pl.BlockSpec((tr, d), lambda i: (i, 0)),
        out_shape=jax.ShapeDtypeStruct((t, d), BF16),
        compiler_params=_cparams(("parallel",), 32),
        name="ln_mod",
    )(x2, g, b, mod)


def _mm_kernel(a_ref, b_ref, o_ref, *, relu2):
    acc = jnp.dot(a_ref[...], b_ref[...], preferred_element_type=F32)
    if relu2:
        acc = jnp.square(jnp.maximum(acc, 0.0))
    o_ref[...] = acc.astype(o_ref.dtype)


def _mm_acc_kernel(a_ref, b_ref, o_ref, acc_ref):
    k = pl.program_id(2)

    @pl.when(k == 0)
    def _():
        acc_ref[...] = jnp.zeros_like(acc_ref)

    acc_ref[...] += jnp.dot(a_ref[...], b_ref[...], preferred_element_type=F32)

    @pl.when(k == pl.num_programs(2) - 1)
    def _():
        o_ref[...] = acc_ref[...].astype(o_ref.dtype)


def _matmul(a, b, *, tm, tn, tk=None, out_dtype=F32, relu2=False, name="matmul"):
    m, kd = a.shape
    _, n = b.shape
    tm = min(tm, m)
    osz = jnp.dtype(out_dtype).itemsize
    if tk is None or tk >= kd:
        vm = (2 * tm * kd * 2 + 2 * kd * tn * 2 + 2 * tm * tn * osz + tm * tn * 4) / 2 ** 20 + 8
        return pl.pallas_call(
            functools.partial(_mm_kernel, relu2=relu2),
            grid=(m // tm, n // tn),
            in_specs=[pl.BlockSpec((tm, kd), lambda i, j: (i, 0)),
                      pl.BlockSpec((kd, tn), lambda i, j: (0, j))],
            out_specs=pl.BlockSpec((tm, tn), lambda i, j: (i, j)),
            out_shape=jax.ShapeDtypeStruct((m, n), out_dtype),
            compiler_params=_cparams(("parallel", "parallel"), vm),
            name=name,
        )(a, b)
    assert not relu2
    vm = (2 * tm * tk * 2 + 2 * tk * tn * 2 + 2 * tm * tn * osz + 2 * tm * tn * 4) / 2 ** 20 + 8
    return pl.pallas_call(
        _mm_acc_kernel,
        grid=(m // tm, n // tn, kd // tk),
        in_specs=[pl.BlockSpec((tm, tk), lambda i, j, k: (i, k)),
                  pl.BlockSpec((tk, tn), lambda i, j, k: (k, j))],
        out_specs=pl.BlockSpec((tm, tn), lambda i, j, k: (i, j)),
        out_shape=jax.ShapeDtypeStruct((m, n), out_dtype),
        scratch_shapes=[pltpu.VMEM((tm, tn), F32)],
        compiler_params=_cparams(("parallel", "parallel", "arbitrary"), vm),
        name=name,
    )(a, b)


def _head_ones(n):
    r = lax.broadcasted_iota(jnp.int32, (n, n), 0)
    c = lax.broadcasted_iota(jnp.int32, (n, n), 1)
    return ((r >> 6) == (c >> 6)).astype(BF16)


def _prep_kernel(pr_ref, pk_ref, pv_ref, pc_ref, qr_ref, qk_ref, qv_ref, qc_ref,
                 mr_ref, mk_ref, mv_ref, mc_ref, w0_ref, a0_ref, kk_ref, ka_ref, wl_ref,
                 r_ref, k_ref, v_ref, al_ref, be_ref, lw_ref, g_ref, act_ref, *, per):
    tb, cb = pr_ref.shape
    first = (pl.program_id(0) % per) == 0

    def shift(x_ref, q_ref, m_ref):
        x = x_ref[...]
        width = x.shape[1]
        last = jnp.where(first, 0.0, q_ref[7:8, :])
        prev = pltpu.roll(x, 1, axis=0)
        row = lax.broadcasted_iota(jnp.int32, (8, width), 0)
        head = jnp.where(row == 0, jnp.broadcast_to(last, (8, width)), prev[0:8, :])
        prev = jnp.concatenate([head, prev[8:, :]], axis=0)
        return x + (prev - x) * m_ref[...]

    @pl.when(pl.program_id(1) == 0)
    def _():
        code = shift(pc_ref, qc_ref, mc_ref)
        lane = lax.broadcasted_iota(jnp.int32, code.shape, 1)
        act = jnp.where(lane < DECAY_LORA, jnp.tanh(code),
                        jnp.where(lane < DECAY_LORA + ICLR_LORA, code, jax.nn.sigmoid(code)))
        act_ref[...] = act.astype(BF16)

    r = shift(pr_ref, qr_ref, mr_ref)
    k = shift(pk_ref, qk_ref, mk_ref)
    v = shift(pv_ref, qv_ref, mv_ref)
    up = jnp.dot(act_ref[...], wl_ref[...], preferred_element_type=F32)
    dw = up[:, 0:cb]
    da = up[:, cb:2 * cb]
    g = up[:, 2 * cb:3 * cb]
    lw = -math.exp(-0.5) * jax.nn.sigmoid(w0_ref[...] + dw)
    a = jax.nn.sigmoid(a0_ref[...] + da)
    kk = k * kk_ref[...]
    ss = _dot_exact_rhs(kk * kk, _head_ones(cb))
    kk = kk * lax.rsqrt(jnp.maximum(ss, 1e-24))
    r_ref[...] = r.astype(r_ref.dtype)
    k_ref[...] = (k * (1.0 + (a - 1.0) * ka_ref[...])).astype(k_ref.dtype)
    v_ref[...] = v.astype(v_ref.dtype)
    al_ref[...] = (-kk).astype(al_ref.dtype)
    be_ref[...] = (kk * a).astype(be_ref.dtype)
    lw_ref[...] = lw
    g_ref[...] = g.astype(g_ref.dtype)


def _rwkv_prep(p, mu, w0, a0, k_k, k_a, w_lora, seq, off, tb=256, cb=256):
    t = p.shape[0]
    nb = D_A // cb
    ob = off // cb
    cblk = (off + OFF_W) // LORA_COLS
    per = seq // tb
    rows8 = tb // 8

    def cur(o):
        return pl.BlockSpec((tb, cb), lambda i, j: (i, ob + o + j))

    def prev(o):
        return pl.BlockSpec((8, cb), lambda i, j: (jnp.maximum(i * rows8 - 1, 0), ob + o + j))

    def vec(o):
        return pl.BlockSpec((1, cb), lambda i, j: (0, o + j))

    in_specs = [cur(0), cur(nb), cur(2 * nb),
                pl.BlockSpec((tb, LORA_COLS), lambda i, j: (i, cblk)),
                prev(0), prev(nb), prev(2 * nb),
                pl.BlockSpec((8, LORA_COLS), lambda i, j: (jnp.maximum(i * rows8 - 1, 0), cblk)),
                vec(0), vec(nb), vec(2 * nb),
                pl.BlockSpec((1, LORA_COLS), lambda i, j: (0, OFF_W // LORA_COLS)),
                vec(0), vec(0), vec(0), vec(0),
                pl.BlockSpec((None, LORA_COLS, 3 * cb), lambda i, j: (j, 0, 0))]
    half = jax.ShapeDtypeStruct((t, D_A), BF16)
    full = jax.ShapeDtypeStruct((t, D_A), F32)
    return pl.pallas_call(
        functools.partial(_prep_kernel, per=per),
        grid=(t // tb, nb),
        in_specs=in_specs,
        out_specs=[pl.BlockSpec((tb, cb), lambda i, j: (i, j))] * 7,
        out_shape=[half, half, half, half, half, full, half],
        scratch_shapes=[pltpu.VMEM((tb, LORA_COLS), BF16)],
        compiler_params=_cparams(("parallel", "arbitrary"), 32),
        name="rwkv_prep",
    )(p, p, p, p, p, p, p, p, mu, mu, mu, mu, w0, a0, k_k, k_a, w_lora)


def _scan_chunk(ins, hts, consts):
    m0, strict, incl, blk16, blk32, eye, tri = consts
    each = lambda f, *ls: [f(*a) for a in zip(*ls)]
    r, k, v, al, be, lw = (list(z) for z in zip(*ins))

    def cumsum(x):
        hi = x.astype(BF16)
        rem = x - hi.astype(F32)
        mid = rem.astype(BF16)
        lo = (rem - mid.astype(F32)).astype(BF16)
        c3 = jnp.dot(tri, jnp.concatenate([hi, mid, lo], axis=1), preferred_element_type=F32)
        return c3[:, 0:PAIR] + c3[:, PAIR:2 * PAIR] + c3[:, 2 * PAIR:3 * PAIR]

    def sm(x):
        return jnp.concatenate([jnp.where(m0, x, 0.0), jnp.where(m0, 0.0, x)], axis=0)

    c = each(cumsum, lw)
    pc = each(lambda c_: jnp.exp(c_[CHUNK - 1:CHUNK, :]), c)
    einv = each(lambda c_: jnp.exp(-c_), c)
    a_sm = each(lambda a_, c_, l_: sm(a_ * jnp.exp(c_ - l_)), al, c, lw)
    r_sm = each(lambda r_, c_: sm(r_ * jnp.exp(c_)), r, c)
    v_sm = each(sm, v)
    b_t = each(lambda b_, e_: b_ * e_, be, einv)
    k_t = each(lambda k_, e_: k_ * e_, k, einv)
    s = each(lambda a_, r_, b_, k_: _dot_nt(jnp.concatenate([a_, r_], axis=0),
                                            jnp.concatenate([b_, b_, k_, k_], axis=0)),
             a_sm, r_sm, b_t, k_t)
    lab = each(lambda s_: jnp.where(strict, s_[0:PAIR, 0:PAIR], 0.0), s)
    mak = each(lambda s_: jnp.where(strict, s_[0:PAIR, PAIR:], 0.0), s)
    incl2 = jnp.concatenate([incl, incl], axis=1)
    mrbk = each(lambda s_: jnp.where(incl2, s_[PAIR:, :], 0.0), s)

    ldt = each(lambda l_: jnp.where(blk16, l_, 0.0).T, lab)
    xt = each(lambda l_: jnp.where(eye, 1.0, l_), ldt)
    lt = each(lambda l_: _dot(l_, l_), ldt)
    for _ in range(2):
        xl = each(lambda l_, x_: _dot(l_, jnp.concatenate([x_, l_], axis=1)), lt, xt)
        xt = each(lambda x_, p_: x_ + p_[:, 0:PAIR], xt, xl)
        lt = each(lambda p_: p_[:, PAIR:], xl)
    x = each(lambda x_, l_: (x_ + _dot(l_, x_)).T, xt, lt)
    off16 = jnp.logical_and(blk32, jnp.logical_not(blk16))
    ex = each(lambda l_, x_: _dot(jnp.where(off16, l_, 0.0), x_), lab, x)
    x = each(lambda x_, e_: x_ + _dot(x_, e_), x, ex)
    ex = each(lambda l_, x_: _dot(jnp.where(blk32, 0.0, l_), x_), lab, x)
    x = each(lambda x_, e_: x_ + _dot(x_, e_), x, ex)

    makv = each(_dot, mak, v_sm)
    wu = each(lambda x_, a_, m_: _dot(x_, jnp.concatenate([a_, m_], axis=1)), x, a_sm, makv)
    bigr = each(lambda wu_, v_: jnp.concatenate(
        [wu_, jnp.concatenate([jnp.zeros_like(v_), v_], axis=1)], axis=0), wu, v_sm)
    bk = each(lambda b_, k_, p_: jnp.concatenate([sm(b_ * p_), sm(k_ * p_)], axis=0), b_t, k_t, pc)
    gz = each(lambda b_, g_: _dot(b_.T, g_), bk, bigr)
    qy = each(_dot, mrbk, bigr)
    hb = each(lambda h_: h_.astype(BF16), hts)

    def new_state(ht, h_, g_, p_):
        return ht * p_ + _dot_nt(h_, g_[:, 0:PAIR]) + g_[:, PAIR:].T

    def output(r_, q_, h_):
        y_sm = _dot_nt(r_ + q_[:, 0:PAIR], h_) + q_[:, PAIR:]
        return y_sm[0:CHUNK, :] + y_sm[CHUNK:, :]

    return each(output, r_sm, qy, hb), each(new_state, hts, hb, gz, pc)


def _scan_kernel(r_ref, k_ref, v_ref, al_ref, be_ref, lw_ref, g_ref, rk_ref, lg_ref, lb_ref,
                 o_ref, h_ref, y_ref, *, npair, nchunk):
    @pl.when(pl.program_id(2) == 0)
    def _():
        h_ref[...] = jnp.zeros_like(h_ref)

    row = lax.broadcasted_iota(jnp.int32, (PAIR, PAIR), 0)
    col = lax.broadcasted_iota(jnp.int32, (PAIR, PAIR), 1)
    same = (row >> 6) == (col >> 6)
    tr_ = row & (CHUNK - 1)
    tc_ = col & (CHUNK - 1)
    strict = jnp.logical_and(same, tc_ < tr_)
    incl = jnp.logical_and(same, tc_ <= tr_)
    blk16 = (row >> 4) == (col >> 4)
    blk32 = (row >> 5) == (col >> 5)
    eye = row == col
    m0 = lax.broadcasted_iota(jnp.int32, (CHUNK, PAIR), 1) < HEAD
    tri = (lax.broadcasted_iota(jnp.int32, (CHUNK, CHUNK), 1)
           <= lax.broadcasted_iota(jnp.int32, (CHUNK, CHUNK), 0)).astype(BF16)
    consts = (m0, strict, incl, blk16, blk32, eye, tri)

    def chunk_body(ci, carry):
        rows = pl.ds(pl.multiple_of(ci * CHUNK, CHUNK), CHUNK)
        lanes = [slice(p * PAIR, (p + 1) * PAIR) for p in range(npair)]
        ins = [tuple(ref[rows, ln].astype(F32)
                     for ref in (r_ref, k_ref, v_ref, al_ref, be_ref, lw_ref)) for ln in lanes]
        ys, hns = _scan_chunk(ins, [h_ref[p] for p in range(npair)], consts)
        for p in range(npair):
            h_ref[p] = hns[p]
            y_ref[rows, lanes[p]] = ys[p]
        return carry

    lax.fori_loop(0, nchunk, chunk_body, 0)

    ones = _head_ones(2 * PAIR)
    for q in range(npair // 2):
        ln = slice(q * 2 * PAIR, (q + 1) * 2 * PAIR)
        y = y_ref[:, ln]
        mean = _dot(y, ones) * (1.0 / HEAD)
        yc = y - mean
        var = _dot(yc * yc, ones) * (1.0 / HEAD)
        yn = yc * lax.rsqrt(var + LNX_EPS) * lg_ref[:, ln] + lb_ref[:, ln]
        rk = _dot(r_ref[:, ln].astype(F32) * k_ref[:, ln].astype(F32) * rk_ref[:, ln], ones)
        o_ref[:, ln] = ((yn + rk * v_ref[:, ln].astype(F32))
                        * g_ref[:, ln].astype(F32)).astype(o_ref.dtype)


def _rwkv_scan(r, k, v, al, be, lw, g, r_k, lnx_g, lnx_b, bsz, seq, tb=256, npair=16):
    t = r.shape[0]
    width = npair * PAIR
    ngrp = D_A // width
    per = seq // tb
    blk = pl.BlockSpec((tb, width), lambda b, j, s: (b * per + s, j))
    vec = pl.BlockSpec((1, width), lambda b, j, s: (0, j))
    return pl.pallas_call(
        functools.partial(_scan_kernel, npair=npair, nchunk=tb // CHUNK),
        grid=(bsz, ngrp, per),
        in_specs=[blk] * 7 + [vec] * 3,
        out_specs=blk,
        out_shape=jax.ShapeDtypeStruct((t, D_A), BF16),
        scratch_shapes=[pltpu.VMEM((npair, PAIR, PAIR), F32), pltpu.VMEM((tb, width), F32)],
        compiler_params=_cparams(("parallel", "parallel", "arbitrary"),
                                 16 * tb * width * 4 / 2 ** 20 + 16),
        name="rwkv_scan",
    )(r, k, v, al, be, lw, g, r_k, lnx_g, lnx_b)


def _bucket_table():
    qi = np.arange(BLOCK)[:, None]
    kj = np.arange(2 * BLOCK)[None, :]
    dist = qi + BLOCK - kj
    n = np.maximum(dist, 0)
    nf = np.maximum(n, 1).astype(np.float32)
    large = RPB_MAX_EXACT + (np.log(nf / np.float32(RPB_MAX_EXACT))
                             / np.float32(math.log(RPB_MAX_DIST / RPB_MAX_EXACT))
                             * np.float32(RPB_BUCKETS - RPB_MAX_EXACT)).astype(np.int32)
    large = np.minimum(large, RPB_BUCKETS - 1)
    bucket = np.where(n < RPB_MAX_EXACT, n, large)
    valid = (dist >= 0) & (dist < WINDOW)
    return np.where(valid, bucket, -1).astype(np.int32)


def _bias_kernel(tab_ref, bkt_ref, o_ref):
    first = pl.program_id(0) == 1
    h = pl.program_id(1)
    bkt = bkt_ref[...]
    acc = jnp.full(bkt.shape, NEG, F32)
    for b in range(RPB_BUCKETS):
        acc = jnp.where(bkt == b, tab_ref[b, h], acc)
    col = lax.broadcasted_iota(jnp.int32, bkt.shape, 1)
    o_ref[...] = jnp.where(jnp.logical_and(first, col < BLOCK), NEG, acc)


def _attn_bias(rpb_table):
    bkt = jnp.asarray(_bucket_table())
    return pl.pallas_call(
        _bias_kernel,
        grid=(2, H_Q),
        in_specs=[pl.BlockSpec(memory_space=pltpu.SMEM),
                  pl.BlockSpec((BLOCK, 2 * BLOCK), lambda f, h: (0, 0))],
        out_specs=pl.BlockSpec((None, None, BLOCK, 2 * BLOCK), lambda f, h: (f, h, 0, 0)),
        out_shape=jax.ShapeDtypeStruct((2, H_Q, BLOCK, 2 * BLOCK), F32),
        compiler_params=_cparams(("arbitrary", "arbitrary"), 16),
        name="attn_bias",
    )(rpb_table, bkt)


def _swa_kernel(sink_ref, q_ref, kc_ref, kp_ref, vc_ref, vp_ref, bias_ref, o_ref):
    each = lambda f, *ls: [f(*a) for a in zip(*ls)]
    lo = lax.broadcasted_iota(jnp.int32, (BLOCK, PAIR), 1) < HEAD
    scale = HEAD ** -0.5
    zeros = jnp.zeros((2 * BLOCK, HEAD), BF16)
    ones = jnp.ones((2 * BLOCK, HEAD), BF16)
    npr = GQA // 2
    for g in range(H_KV):
        gsl = slice(g * HEAD, (g + 1) * HEAD)
        kg = (jnp.concatenate([kp_ref[:, gsl], kc_ref[:, gsl]], axis=0) * scale).astype(BF16)
        vg = jnp.concatenate([vp_ref[:, gsl], vc_ref[:, gsl]], axis=0).astype(BF16)
        kdup = jnp.concatenate([kg, kg], axis=1)
        rhs = jnp.concatenate([jnp.concatenate([vg, zeros, ones, zeros], axis=1),
                               jnp.concatenate([zeros, vg, zeros, ones], axis=1)], axis=0)
        heads = [g * GQA + 2 * i for i in range(npr)]
        lanes = [slice(h * HEAD, (h + 2) * HEAD) for h in heads]
        qp = [q_ref[:, ln] for ln in lanes]
        s2 = each(lambda q_: _dot_nt(jnp.concatenate([jnp.where(lo, q_, 0.0),
                                                      jnp.where(lo, 0.0, q_)], axis=0), kdup), qp)
        s = [(s_[0:BLOCK] + bias_ref[h], s_[BLOCK:] + bias_ref[h + 1]) for s_, h in zip(s2, heads)]
        m = [(jnp.maximum(jnp.max(a, axis=-1, keepdims=True), sink_ref[0, h]),
              jnp.maximum(jnp.max(b, axis=-1, keepdims=True), sink_ref[0, h + 1]))
             for (a, b), h in zip(s, heads)]
        e = each(lambda s_, m_: jnp.concatenate([jnp.exp(s_[0] - m_[0]), jnp.exp(s_[1] - m_[1])],
                                                axis=1).astype(BF16), s, m)
        od = each(lambda e_: jnp.dot(e_, rhs, preferred_element_type=F32), e)
        for o_, m_, h, ln in zip(od, m, heads, lanes):
            den = o_[:, PAIR:] + jnp.where(lo, jnp.exp(sink_ref[0, h] - m_[0]),
                                           jnp.exp(sink_ref[0, h + 1] - m_[1]))
            o_ref[:, ln] = (o_[:, 0:PAIR] / den).astype(o_ref.dtype)


def _swa(q, kv, bias, sinks, bsz, seq, qblk, kblk):
    t = q.shape[0]
    nb = seq // BLOCK
    kvw = H_KV * HEAD

    def cur(c):
        return pl.BlockSpec((BLOCK, kvw), lambda b, n: (b * nb + n, c))

    def prev(c):
        return pl.BlockSpec((BLOCK, kvw), lambda b, n: (b * nb + jnp.maximum(n - 1, 0), c))

    return pl.pallas_call(
        _swa_kernel,
        grid=(bsz, nb),
        in_specs=[pl.BlockSpec(memory_space=pltpu.SMEM),
                  pl.BlockSpec((BLOCK, D_B), lambda b, n: (b * nb + n, qblk)),
                  cur(kblk), prev(kblk), cur(kblk + 1), prev(kblk + 1),
                  pl.BlockSpec((None, H_Q, BLOCK, 2 * BLOCK),
                               lambda b, n: (jnp.where(n == 0, 1, 0), 0, 0, 0))],
        out_specs=pl.BlockSpec((BLOCK, D_B), lambda b, n: (b * nb + n, 0)),
        out_shape=jax.ShapeDtypeStruct((t, D_B), BF16),
        compiler_params=_cparams(("parallel", "arbitrary"), 32),
        name="swa",
    )(sinks, q, kv, kv, kv, kv, bias)


def _post_mix_kernel(x_ref, mix_ref, ge_ref, be_ref, g1_ref, b1_ref, mod_ref, x1_ref, u_ref):
    xn = _layer_norm(x_ref[...], ge_ref[...], be_ref[...])
    z = ALPHA * xn + (1.0 + mod_ref[2:3, :]) * mix_ref[...].astype(F32)
    x1 = _layer_norm(z, g1_ref[...], b1_ref[...])
    x1_ref[...] = x1
    u_ref[...] = (x1 * (1.0 + mod_ref[4:5, :]) + mod_ref[3:4, :]).astype(u_ref.dtype)


def _post_mix(x2, mix, ge, be, g1, b1, mod, seq, tr=128):
    t, d = x2.shape
    per = seq // tr
    row = pl.BlockSpec((tr, d), lambda i: (i, 0))
    vec = pl.BlockSpec((1, d), lambda i: (0, 0))
    return pl.pallas_call(
        _post_mix_kernel,
        grid=(t // tr,),
        in_specs=[row, row, vec, vec, vec, vec,
                  pl.BlockSpec((None, 6, d), lambda i: (i // per, 0, 0))],
        out_specs=[row, row],
        out_shape=[jax.ShapeDtypeStruct((t, d), F32), jax.ShapeDtypeStruct((t, d), BF16)],
        compiler_params=_cparams(("parallel",), 32),
        name="post_mix",
    )(x2, mix, ge, be, g1, b1, mod)


def _final_kernel(x1_ref, h_ref, g2_ref, b2_ref, mod_ref, o_ref):
    z = ALPHA * x1_ref[...] + (1.0 + mod_ref[5:6, :]) * h_ref[...].astype(F32)
    o_ref[...] = _layer_norm(z, g2_ref[...], b2_ref[...])


def _final(x1, h, g2, b2, mod, seq, tr=128):
    t, d = x1.shape
    per = seq // tr
    row = pl.BlockSpec((tr, d), lambda i: (i, 0))
    vec = pl.BlockSpec((1, d), lambda i: (0, 0))
    return pl.pallas_call(
        _final_kernel,
        grid=(t // tr,),
        in_specs=[row, row, vec, vec, pl.BlockSpec((None, 6, d), lambda i: (i // per, 0, 0))],
        out_specs=row,
        out_shape=jax.ShapeDtypeStruct((t, d), F32),
        compiler_params=_cparams(("parallel",), 32),
        name="final_ln",
    )(x1, h, g2, b2, mod)


def _lora_weights(w_decay_up, w_iclr_up, w_gate_up, cb):
    zd = jnp.zeros((LORA_COLS, D_A), F32)
    wd = zd.at[0:DECAY_LORA].set(w_decay_up)
    wa = zd.at[DECAY_LORA:DECAY_LORA + ICLR_LORA].set(w_iclr_up)
    wg = zd.at[DECAY_LORA + ICLR_LORA:].set(w_gate_up)
    nb = D_A // cb
    parts = [w.reshape(LORA_COLS, nb, cb) for w in (wd, wa, wg)]
    return jnp.transpose(jnp.concatenate(parts, axis=2), (1, 0, 2))


def kernel(x, c, ln_emb_g, ln_emb_b, rpb_table, w_mod, b_mod, w_in, mu_shift, w0, w_decay_up, a0,
           w_iclr_up, w_gate_up, k_k, k_a, r_k, lnx_g, lnx_b, attn_sinks, w_out, ln1_g, ln1_b,
           w_up, w_down, ln2_g, ln2_b):
    bsz, seq, d = x.shape
    assert w_mod.shape[0] == DEPTH == 1 and d == D_MODEL and bsz <= 8
    t = bsz * seq
    row = lambda a: a.reshape(1, -1)
    x2 = x.reshape(t, d)
    c8 = jnp.pad(c, ((0, 8 - bsz), (0, 0)))
    bias = _attn_bias(rpb_table)
    mod = _modulation(c8, w_mod[0], row(b_mod[0]))[:bsz].reshape(bsz, 6, d)
    u1 = _ln_mod(x2, row(ln_emb_g), row(ln_emb_b), mod, seq)
    wi = jnp.concatenate([w_in[0][:, OFF_Q:], w_in[0][:, :RWKV_COLS]], axis=1).astype(BF16)
    p = _matmul(u1, wi, tm=512, tn=1280, name="in_proj")
    rwkv_off = N_IN - RWKV_COLS
    w_lora = _lora_weights(w_decay_up[0], w_iclr_up[0], w_gate_up[0], 256).astype(BF16)
    r, k, v, al, be, lw, g = _rwkv_prep(p, row(mu_shift[0]), row(w0[0]), row(a0[0]),
                                        row(k_k[0]), row(k_a[0]), w_lora, seq, rwkv_off)
    y_a = _rwkv_scan(r, k, v, al, be, lw, g, row(r_k[0]), row(lnx_g[0]), row(lnx_b[0]),
                     bsz, seq)
    y_b = _swa(p, p, bias, row(attn_sinks[0]), bsz, seq, 0, D_B // (H_KV * HEAD))
    mix = _matmul(jnp.concatenate([y_a, y_b], axis=1), w_out[0].astype(BF16),
                  tm=1024, tn=1024, out_dtype=BF16, name="out_proj")
    x1, u2 = _post_mix(x2, mix, row(ln_emb_g), row(ln_emb_b), row(ln1_g[0]), row(ln1_b[0]),
                       mod, seq)
    hmid = _matmul(u2, w_up[0].astype(BF16), tm=1024, tn=1024, out_dtype=BF16, relu2=True,
                   name="mlp_up")
    hout = _matmul(hmid, w_down[0].astype(BF16), tm=1024, tn=1024, tk=4096, out_dtype=BF16,
                   name="mlp_down")
    out = _final(x1, hout, row(ln2_g[0]), row(ln2_b[0]), mod, seq)
    return out.reshape(bsz, seq, d)
```

```python
import functools
import math

import numpy as np
import jax
import jax.numpy as jnp
from jax import lax
from jax.experimental import pallas as pl
from jax.experimental.pallas import tpu as pltpu

F32 = jnp.float32
BF16 = jnp.bfloat16

D_MODEL = 4096
HEAD = 64
D_A = D_MODEL // 2
D_B = D_MODEL - D_A
H_A = D_A // HEAD
H_Q = D_B // HEAD
GQA = 8
H_KV = H_Q // GQA
WINDOW = 128
BLOCK = 128
RPB_BUCKETS = 32
RPB_MAX_EXACT = RPB_BUCKETS // 2
RPB_MAX_DIST = 128
DECAY_LORA = max(32, int(round(D_A ** 0.5 * 1.8 / 32)) * 32)
ICLR_LORA = max(32, int(round(D_A ** 0.5 * 1.8 / 32)) * 32)
GATE_LORA = max(32, int(round(D_A ** 0.6 * 0.8 / 32)) * 32)
LORA_COLS = DECAY_LORA + ICLR_LORA + GATE_LORA
D_FF = 4 * D_MODEL
DEPTH = 1
ALPHA = (2.0 * DEPTH) ** 0.25
LN_EPS = 1e-5
LNX_EPS = 64e-5
OFF_W = 3 * D_A
RWKV_COLS = OFF_W + LORA_COLS
OFF_Q = RWKV_COLS
OFF_KB = OFF_Q + D_B
OFF_VB = OFF_KB + H_KV * HEAD
N_IN = OFF_VB + H_KV * HEAD
NEG = -1e30

CHUNK = 64
PAIR = 2 * HEAD
VMEM_CAP = 56 * 1024 * 1024


def _cparams(sem, vmem_mb):
    return pltpu.CompilerParams(dimension_semantics=sem,
                                vmem_limit_bytes=min(int(vmem_mb * 1024 * 1024), VMEM_CAP))


def _dot(a, b):
    return jnp.dot(a.astype(BF16), b.astype(BF16), preferred_element_type=F32)


def _dot_nt(a, b):
    return lax.dot_general(a.astype(BF16), b.astype(BF16), (((1,), (1,)), ((), ())),
                           preferred_element_type=F32)


def _split2(x):
    hi = x.astype(BF16)
    lo = (x - hi.astype(F32)).astype(BF16)
    return hi, lo


def _dot_exact_rhs(x, rhs_bf16):
    hi, lo = _split2(x)
    return (jnp.dot(hi, rhs_bf16, preferred_element_type=F32)
            + jnp.dot(lo, rhs_bf16, preferred_element_type=F32))


def _dot3(a, b):
    ah, al = _split2(a)
    bh, bl = _split2(b)
    return (jnp.dot(ah, bh, preferred_element_type=F32)
            + jnp.dot(ah, bl, preferred_element_type=F32)
            + jnp.dot(al, bh, preferred_element_type=F32))


def _layer_norm(x, g, b):
    mu = jnp.mean(x, axis=-1, keepdims=True)
    xc = x - mu
    var = jnp.mean(xc * xc, axis=-1, keepdims=True)
    return xc * lax.rsqrt(var + LN_EPS) * g + b


def _mod_kernel(c_ref, w_ref, b_ref, o_ref):
    c = c_ref[...]
    cond = c * jax.nn.sigmoid(c)
    o_ref[...] = _dot3(cond, w_ref[...]) + b_ref[...]


def _modulation(c8, w_mod, b_mod, tn=512):
    d, n = w_mod.shape
    return pl.pallas_call(
        _mod_kernel,
        grid=(n // tn,),
        in_specs=[pl.BlockSpec((8, d), lambda j: (0, 0)),
                  pl.BlockSpec((d, tn), lambda j: (0, j)),
                  pl.BlockSpec((1, tn), lambda j: (0, j))],
        out_specs=pl.BlockSpec((8, tn), lambda j: (0, j)),
        out_shape=jax.ShapeDtypeStruct((8, n), F32),
        compiler_params=_cparams(("parallel",), 40),
        name="modulation",
    )(c8, w_mod, b_mod)


def _ln_mod_kernel(x_ref, g_ref, b_ref, mod_ref, u_ref):
    xn = _layer_norm(x_ref[...], g_ref[...], b_ref[...])
    u_ref[...] = (xn * (1.0 + mod_ref[1:2, :]) + mod_ref[0:1, :]).astype(u_ref.dtype)


def _ln_mod(x2, g, b, mod, seq, tr=256):
    t, d = x2.shape
    per = seq // tr
    return pl.pallas_call(
        _ln_mod_kernel,
        grid=(t // tr,),
        in_specs=[pl.BlockSpec((tr, d), lambda i: (i, 0)),
                  pl.BlockSpec((1, d), lambda i: (0, 0)),
                  pl.BlockSpec((1, d), lambda i: (0, 0)),
                  pl.BlockSpec((None, 6, d), lambda i: (i // per, 0, 0))],
        out_specs=pl.BlockSpec((tr, d), lambda i: (i, 0)),
        out_shape=jax.ShapeDtypeStruct((t, d), BF16),
        compiler_params=_cparams(("parallel",), 32),
        name="ln_mod",
    )(x2, g, b, mod)


def _mm_kernel(*refs, relu2, convert):
    if convert:
        *refs, ci_ref, o_ref, co_ref = refs
        co_ref[...] = ci_ref[...].astype(co_ref.dtype)
        refs = (*refs, o_ref)
    *a_refs, b_ref, o_ref = refs
    acc, off = None, 0
    for a_ref in a_refs:
        kd = a_ref.shape[1]
        part = jnp.dot(a_ref[...], b_ref[off:off + kd, :].astype(BF16), preferred_element_type=F32)
        acc = part if acc is None else acc + part
        off += kd
    if relu2:
        acc = jnp.square(jnp.maximum(acc, 0.0))
    o_ref[...] = acc.astype(o_ref.dtype)


def _mm_acc_kernel(a_ref, b_ref, o_ref, acc_ref):
    k = pl.program_id(2)

    @pl.when(k == 0)
    def _():
        acc_ref[...] = jnp.zeros_like(acc_ref)

    acc_ref[...] += jnp.dot(a_ref[...], b_ref[...].astype(BF16), preferred_element_type=F32)

    @pl.when(k == pl.num_programs(2) - 1)
    def _():
        o_ref[...] = acc_ref[...].astype(o_ref.dtype)


def _matmul(a, b, *, tm, tn, tk=None, out_dtype=F32, relu2=False, col_shift=0, convert=None,
            a_buffers=2, name="matmul"):
    a = a if isinstance(a, (list, tuple)) else [a]
    m = a[0].shape[0]
    kd, n = b.shape
    tm = min(tm, m)
    nj = n // tn
    osz = jnp.dtype(out_dtype).itemsize
    bsz = jnp.dtype(b.dtype).itemsize
    if tk is None or tk >= kd:
        vm = (a_buffers * tm * kd * 2 + kd * tn * (2 * bsz + 2) + 2 * tm * tn * osz
              + tm * tn * 4) / 2 ** 20 + 8
        in_specs = ([pl.BlockSpec((tm, x.shape[1]), lambda i, j: (i, 0),
                                  pipeline_mode=pl.Buffered(a_buffers)) for x in a]
                    + [pl.BlockSpec((kd, tn), lambda i, j: (0, (j + col_shift) % nj))])
        out_specs = pl.BlockSpec((tm, tn), lambda i, j: (i, j))
        out_shape = jax.ShapeDtypeStruct((m, n), out_dtype)
        args = (*a, b)
        if convert is not None:
            cr, cc = convert.shape
            slab = cr // ((m // tm) * nj)
            assert slab * (m // tm) * nj == cr and slab % 16 == 0
            cspec = pl.BlockSpec((slab, cc), lambda i, j: (i * nj + j, 0))
            in_specs, args = in_specs + [cspec], (*args, convert)
            out_specs = [out_specs, cspec]
            out_shape = [out_shape, jax.ShapeDtypeStruct((cr, cc), BF16)]
            vm += slab * cc * 12 / 2 ** 20
        return pl.pallas_call(
            functools.partial(_mm_kernel, relu2=relu2, convert=convert is not None),
            grid=(m // tm, nj),
            in_specs=in_specs,
            out_specs=out_specs,
            out_shape=out_shape,
            compiler_params=_cparams(("parallel", "parallel"), vm),
            name=name,
        )(*args)
    assert not relu2 and len(a) == 1 and col_shift == 0 and convert is None
    vm = (2 * tm * tk * 2 + tk * tn * (2 * bsz + 2) + 2 * tm * tn * osz + 2 * tm * tn * 4) / 2 ** 20 + 8
    return pl.pallas_call(
        _mm_acc_kernel,
        grid=(m // tm, nj, kd // tk),
        in_specs=[pl.BlockSpec((tm, tk), lambda i, j, k: (i, k)),
                  pl.BlockSpec((tk, tn), lambda i, j, k: (k, j))],
        out_specs=pl.BlockSpec((tm, tn), lambda i, j, k: (i, j)),
        out_shape=jax.ShapeDtypeStruct((m, n), out_dtype),
        scratch_shapes=[pltpu.VMEM((tm, tn), F32)],
        compiler_params=_cparams(("parallel", "parallel", "arbitrary"), vm),
        name=name,
    )(a[0], b)


def _head_ones(n):
    r = lax.broadcasted_iota(jnp.int32, (n, n), 0)
    c = lax.broadcasted_iota(jnp.int32, (n, n), 1)
    return ((r >> 6) == (c >> 6)).astype(BF16)


def _prep_kernel(pr_ref, pk_ref, pv_ref, pc_ref, qr_ref, qk_ref, qv_ref, qc_ref,
                 mr_ref, mk_ref, mv_ref, mc_ref, w0_ref, a0_ref, kk_ref, ka_ref, wl_ref,
                 r_ref, k_ref, v_ref, al_ref, be_ref, lw_ref, g_ref, act_ref, *, per):
    tb, cb = pr_ref.shape
    first = (pl.program_id(0) % per) == 0

    def shift(x_ref, q_ref, m_ref):
        x = x_ref[...]
        width = x.shape[1]
        last = jnp.where(first, 0.0, q_ref[7:8, :])
        prev = pltpu.roll(x, 1, axis=0)
        row = lax.broadcasted_iota(jnp.int32, (8, width), 0)
        head = jnp.where(row == 0, jnp.broadcast_to(last, (8, width)), prev[0:8, :])
        prev = jnp.concatenate([head, prev[8:, :]], axis=0)
        return x + (prev - x) * m_ref[...]

    @pl.when(pl.program_id(1) == 0)
    def _():
        code = shift(pc_ref, qc_ref, mc_ref)
        lane = lax.broadcasted_iota(jnp.int32, code.shape, 1)
        act = jnp.where(lane < DECAY_LORA, jnp.tanh(code),
                        jnp.where(lane < DECAY_LORA + ICLR_LORA, code, jax.nn.sigmoid(code)))
        act_ref[...] = act.astype(BF16)

    r = shift(pr_ref, qr_ref, mr_ref)
    k = shift(pk_ref, qk_ref, mk_ref)
    v = shift(pv_ref, qv_ref, mv_ref)
    up = jnp.dot(act_ref[...], wl_ref[...], preferred_element_type=F32)
    dw = up[:, 0:cb]
    da = up[:, cb:2 * cb]
    g = up[:, 2 * cb:3 * cb]
    lw = -math.exp(-0.5) * jax.nn.sigmoid(w0_ref[...] + dw)
    a = jax.nn.sigmoid(a0_ref[...] + da)
    kk = k * kk_ref[...]
    ss = _dot_exact_rhs(kk * kk, _head_ones(cb))
    kk = kk * lax.rsqrt(jnp.maximum(ss, 1e-24))
    r_ref[...] = r.astype(r_ref.dtype)
    k_ref[...] = (k * (1.0 + (a - 1.0) * ka_ref[...])).astype(k_ref.dtype)
    v_ref[...] = v.astype(v_ref.dtype)
    al_ref[...] = (-kk).astype(al_ref.dtype)
    be_ref[...] = (kk * a).astype(be_ref.dtype)
    lw_ref[...] = lw
    g_ref[...] = g.astype(g_ref.dtype)


def _rwkv_prep(p, mu, w0, a0, k_k, k_a, w_lora, seq, off, tb=256, cb=256):
    t = p.shape[0]
    nb = D_A // cb
    ob = off // cb
    cblk = (off + OFF_W) // LORA_COLS
    per = seq // tb
    rows8 = tb // 8

    def cur(o):
        return pl.BlockSpec((tb, cb), lambda i, j: (i, ob + o + j))

    def prev(o):
        return pl.BlockSpec((8, cb), lambda i, j: (jnp.maximum(i * rows8 - 1, 0), ob + o + j))

    def vec(o):
        return pl.BlockSpec((1, cb), lambda i, j: (0, o + j))

    in_specs = [cur(0), cur(nb), cur(2 * nb),
                pl.BlockSpec((tb, LORA_COLS), lambda i, j: (i, cblk)),
                prev(0), prev(nb), prev(2 * nb),
                pl.BlockSpec((8, LORA_COLS), lambda i, j: (jnp.maximum(i * rows8 - 1, 0), cblk)),
                vec(0), vec(nb), vec(2 * nb),
                pl.BlockSpec((1, LORA_COLS), lambda i, j: (0, OFF_W // LORA_COLS)),
                vec(0), vec(0), vec(0), vec(0),
                pl.BlockSpec((None, LORA_COLS, 3 * cb), lambda i, j: (j, 0, 0))]
    half = jax.ShapeDtypeStruct((t, D_A), BF16)
    full = jax.ShapeDtypeStruct((t, D_A), F32)
    return pl.pallas_call(
        functools.partial(_prep_kernel, per=per),
        grid=(t // tb, nb),
        in_specs=in_specs,
        out_specs=[pl.BlockSpec((tb, cb), lambda i, j: (i, j))] * 7,
        out_shape=[half, half, half, half, half, full, half],
        scratch_shapes=[pltpu.VMEM((tb, LORA_COLS), BF16)],
        compiler_params=_cparams(("parallel", "arbitrary"), 32),
        name="rwkv_prep",
    )(p, p, p, p, p, p, p, p, mu, mu, mu, mu, w0, a0, k_k, k_a, w_lora)


def _scan_chunk(ins, hts, consts):
    m0, strict, incl, eye, tri = consts
    each = lambda f, *ls: [f(*a) for a in zip(*ls)]
    r, k, v, al, be, lw = (list(z) for z in zip(*ins))

    def cumsum(x):
        hi = x.astype(BF16)
        rem = x - hi.astype(F32)
        mid = rem.astype(BF16)
        lo = (rem - mid.astype(F32)).astype(BF16)
        c3 = jnp.dot(tri, jnp.concatenate([hi, mid, lo], axis=1), preferred_element_type=F32)
        return c3[:, 0:PAIR] + c3[:, PAIR:2 * PAIR] + c3[:, 2 * PAIR:3 * PAIR]

    def sm(x):
        return jnp.concatenate([jnp.where(m0, x, 0.0), jnp.where(m0, 0.0, x)], axis=0)

    c = each(cumsum, lw)
    pc = each(lambda c_: jnp.exp(c_[CHUNK - 1:CHUNK, :]), c)
    einv = each(lambda c_: jnp.exp(-c_), c)
    a_sm = each(lambda a_, c_, l_: sm(a_ * jnp.exp(c_ - l_)), al, c, lw)
    r_sm = each(lambda r_, c_: sm(r_ * jnp.exp(c_)), r, c)
    v_sm = each(sm, v)
    b_t = each(lambda b_, e_: b_ * e_, be, einv)
    k_t = each(lambda k_, e_: k_ * e_, k, einv)
    s = each(lambda a_, r_, b_, k_: _dot_nt(jnp.concatenate([a_, r_], axis=0),
                                            jnp.concatenate([b_, b_, k_, k_], axis=0)),
             a_sm, r_sm, b_t, k_t)
    lab = each(lambda s_: jnp.where(strict, s_[0:PAIR, 0:PAIR], 0.0), s)
    mak = each(lambda s_: jnp.where(strict, s_[0:PAIR, PAIR:], 0.0), s)
    incl2 = jnp.concatenate([incl, incl], axis=1)
    mrbk = each(lambda s_: jnp.where(incl2, s_[PAIR:, :], 0.0), s)

    ldt = each(lambda l_: l_.T, lab)
    xt = each(lambda l_: jnp.where(eye, 1.0, l_), ldt)
    lt = each(lambda l_: _dot(l_, l_), ldt)
    for _ in range(4):
        xl = each(lambda l_, x_: _dot(l_, jnp.concatenate([x_, l_], axis=1)), lt, xt)
        xt = each(lambda x_, p_: x_ + p_[:, 0:PAIR], xt, xl)
        lt = each(lambda p_: p_[:, PAIR:], xl)
    x = each(lambda x_, l_: (x_ + _dot(l_, x_)).T, xt, lt)

    makv = each(_dot, mak, v_sm)
    wu = each(lambda x_, a_, m_: _dot(x_, jnp.concatenate([a_, m_], axis=1)), x, a_sm, makv)
    bigr = each(lambda wu_, v_: jnp.concatenate(
        [wu_, jnp.concatenate([jnp.zeros_like(v_), v_], axis=1)], axis=0), wu, v_sm)
    bk = each(lambda b_, k_, p_: jnp.concatenate([sm(b_ * p_), sm(k_ * p_)], axis=0), b_t, k_t, pc)
    gz = each(lambda b_, g_: _dot(b_.T, g_), bk, bigr)
    qy = each(_dot, mrbk, bigr)
    hb = each(lambda h_: h_.astype(BF16), hts)

    def new_state(ht, h_, g_, p_):
        return ht * p_ + _dot_nt(h_, g_[:, 0:PAIR]) + g_[:, PAIR:].T

    def output(r_, q_, h_):
        y_sm = _dot_nt(r_ + q_[:, 0:PAIR], h_) + q_[:, PAIR:]
        return y_sm[0:CHUNK, :] + y_sm[CHUNK:, :]

    return each(output, r_sm, qy, hb), each(new_state, hts, hb, gz, pc)


def _scan_kernel(r_ref, k_ref, v_ref, al_ref, be_ref, lw_ref, g_ref, rk_ref, lg_ref, lb_ref,
                 o_ref, h_ref, y_ref, *, npair, nchunk):
    @pl.when(pl.program_id(2) == 0)
    def _():
        h_ref[...] = jnp.zeros_like(h_ref)

    row = lax.broadcasted_iota(jnp.int32, (PAIR, PAIR), 0)
    col = lax.broadcasted_iota(jnp.int32, (PAIR, PAIR), 1)
    same = (row >> 6) == (col >> 6)
    tr_ = row & (CHUNK - 1)
    tc_ = col & (CHUNK - 1)
    strict = jnp.logical_and(same, tc_ < tr_)
    incl = jnp.logical_and(same, tc_ <= tr_)
    eye = row == col
    m0 = lax.broadcasted_iota(jnp.int32, (CHUNK, PAIR), 1) < HEAD
    tri = (lax.broadcasted_iota(jnp.int32, (CHUNK, CHUNK), 1)
           <= lax.broadcasted_iota(jnp.int32, (CHUNK, CHUNK), 0)).astype(BF16)
    consts = (m0, strict, incl, eye, tri)

    def chunk_body(ci, carry):
        rows = pl.ds(pl.multiple_of(ci * CHUNK, CHUNK), CHUNK)
        lanes = [slice(p * PAIR, (p + 1) * PAIR) for p in range(npair)]
        ins = [tuple(ref[rows, ln].astype(F32)
                     for ref in (r_ref, k_ref, v_ref, al_ref, be_ref, lw_ref)) for ln in lanes]
        ys, hns = _scan_chunk(ins, [h_ref[p] for p in range(npair)], consts)
        for p in range(npair):
            h_ref[p] = hns[p]
            y_ref[rows, lanes[p]] = ys[p]
        return carry

    lax.fori_loop(0, nchunk, chunk_body, 0)

    ones = _head_ones(2 * PAIR)
    for q in range(npair // 2):
        ln = slice(q * 2 * PAIR, (q + 1) * 2 * PAIR)
        y = y_ref[:, ln]
        mean = _dot(y, ones) * (1.0 / HEAD)
        yc = y - mean
        var = _dot(yc * yc, ones) * (1.0 / HEAD)
        yn = yc * lax.rsqrt(var + LNX_EPS) * lg_ref[:, ln] + lb_ref[:, ln]
        rk = _dot(r_ref[:, ln].astype(F32) * k_ref[:, ln].astype(F32) * rk_ref[:, ln], ones)
        o_ref[:, ln] = ((yn + rk * v_ref[:, ln].astype(F32))
                        * g_ref[:, ln].astype(F32)).astype(o_ref.dtype)


def _rwkv_scan(r, k, v, al, be, lw, g, r_k, lnx_g, lnx_b, bsz, seq, tb=256, npair=16):
    t = r.shape[0]
    width = npair * PAIR
    ngrp = D_A // width
    per = seq // tb
    blk = pl.BlockSpec((tb, width), lambda b, j, s: (b * per + s, j))
    vec = pl.BlockSpec((1, width), lambda b, j, s: (0, j))
    return pl.pallas_call(
        functools.partial(_scan_kernel, npair=npair, nchunk=tb // CHUNK),
        grid=(bsz, ngrp, per),
        in_specs=[blk] * 7 + [vec] * 3,
        out_specs=blk,
        out_shape=jax.ShapeDtypeStruct((t, D_A), BF16),
        scratch_shapes=[pltpu.VMEM((npair, PAIR, PAIR), F32), pltpu.VMEM((tb, width), F32)],
        compiler_params=_cparams(("parallel", "parallel", "arbitrary"),
                                 16 * tb * width * 4 / 2 ** 20 + 16),
        name="rwkv_scan",
    )(r, k, v, al, be, lw, g, r_k, lnx_g, lnx_b)


def _bucket_table():
    qi = np.arange(BLOCK)[:, None]
    kj = np.arange(2 * BLOCK)[None, :]
    dist = qi + BLOCK - kj
    n = np.maximum(dist, 0)
    nf = np.maximum(n, 1).astype(np.float32)
    large = RPB_MAX_EXACT + (np.log(nf / np.float32(RPB_MAX_EXACT))
                             / np.float32(math.log(RPB_MAX_DIST / RPB_MAX_EXACT))
                             * np.float32(RPB_BUCKETS - RPB_MAX_EXACT)).astype(np.int32)
    large = np.minimum(large, RPB_BUCKETS - 1)
    bucket = np.where(n < RPB_MAX_EXACT, n, large)
    valid = (dist >= 0) & (dist < WINDOW)
    return np.where(valid, bucket, -1).astype(np.int32)


def _bias_kernel(tab_ref, bkt_ref, o_ref):
    first = pl.program_id(0) == 1
    h = pl.program_id(1)
    bkt = bkt_ref[...]
    acc = jnp.full(bkt.shape, NEG, F32)
    for b in range(RPB_BUCKETS):
        acc = jnp.where(bkt == b, tab_ref[b, h], acc)
    col = lax.broadcasted_iota(jnp.int32, bkt.shape, 1)
    o_ref[...] = jnp.where(jnp.logical_and(first, col < BLOCK), NEG, acc)


def _attn_bias(rpb_table):
    bkt = jnp.asarray(_bucket_table())
    return pl.pallas_call(
        _bias_kernel,
        grid=(2, H_Q),
        in_specs=[pl.BlockSpec(memory_space=pltpu.SMEM),
                  pl.BlockSpec((BLOCK, 2 * BLOCK), lambda f, h: (0, 0))],
        out_specs=pl.BlockSpec((None, None, BLOCK, 2 * BLOCK), lambda f, h: (f, h, 0, 0)),
        out_shape=jax.ShapeDtypeStruct((2, H_Q, BLOCK, 2 * BLOCK), F32),
        compiler_params=_cparams(("arbitrary", "arbitrary"), 16),
        name="attn_bias",
    )(rpb_table, bkt)


def _swa_kernel(sink_ref, q_ref, kc_ref, kp_ref, vc_ref, vp_ref, bias_ref, o_ref):
    each = lambda f, *ls: [f(*a) for a in zip(*ls)]
    lo = lax.broadcasted_iota(jnp.int32, (BLOCK, PAIR), 1) < HEAD
    scale = HEAD ** -0.5
    zeros = jnp.zeros((2 * BLOCK, HEAD), BF16)
    ones = jnp.ones((2 * BLOCK, HEAD), BF16)
    npr = GQA // 2
    for g in range(H_KV):
        gsl = slice(g * HEAD, (g + 1) * HEAD)
        kg = (jnp.concatenate([kp_ref[:, gsl], kc_ref[:, gsl]], axis=0) * scale).astype(BF16)
        vg = jnp.concatenate([vp_ref[:, gsl], vc_ref[:, gsl]], axis=0).astype(BF16)
        kdup = jnp.concatenate([kg, kg], axis=1)
        rhs = jnp.concatenate([jnp.concatenate([vg, zeros, ones, zeros], axis=1),
                               jnp.concatenate([zeros, vg, zeros, ones], axis=1)], axis=0)
        heads = [g * GQA + 2 * i for i in range(npr)]
        lanes = [slice(h * HEAD, (h + 2) * HEAD) for h in heads]
        qp = [q_ref[:, ln] for ln in lanes]
        s2 = each(lambda q_: _dot_nt(jnp.concatenate([jnp.where(lo, q_, 0.0),
                                                      jnp.where(lo, 0.0, q_)], axis=0), kdup), qp)
        s = [(s_[0:BLOCK] + bias_ref[h], s_[BLOCK:] + bias_ref[h + 1]) for s_, h in zip(s2, heads)]
        m = [(jnp.maximum(jnp.max(a, axis=-1, keepdims=True), sink_ref[0, h]),
              jnp.maximum(jnp.max(b, axis=-1, keepdims=True), sink_ref[0, h + 1]))
             for (a, b), h in zip(s, heads)]
        e = each(lambda s_, m_: jnp.concatenate([jnp.exp(s_[0] - m_[0]), jnp.exp(s_[1] - m_[1])],
                                                axis=1).astype(BF16), s, m)
        od = each(lambda e_: jnp.dot(e_, rhs, preferred_element_type=F32), e)
        for o_, m_, h, ln in zip(od, m, heads, lanes):
            den = o_[:, PAIR:] + jnp.where(lo, jnp.exp(sink_ref[0, h] - m_[0]),
                                           jnp.exp(sink_ref[0, h + 1] - m_[1]))
            o_ref[:, ln] = (o_[:, 0:PAIR] / den).astype(o_ref.dtype)


def _swa(q, kv, bias, sinks, bsz, seq, qblk, kblk):
    t = q.shape[0]
    nb = seq // BLOCK
    kvw = H_KV * HEAD

    def cur(c):
        return pl.BlockSpec((BLOCK, kvw), lambda b, n: (b * nb + n, c))

    def prev(c):
        return pl.BlockSpec((BLOCK, kvw), lambda b, n: (b * nb + jnp.maximum(n - 1, 0), c))

    return pl.pallas_call(
        _swa_kernel,
        grid=(bsz, nb),
        in_specs=[pl.BlockSpec(memory_space=pltpu.SMEM),
                  pl.BlockSpec((BLOCK, D_B), lambda b, n: (b * nb + n, qblk)),
                  cur(kblk), prev(kblk), cur(kblk + 1), prev(kblk + 1),
                  pl.BlockSpec((None, H_Q, BLOCK, 2 * BLOCK),
                               lambda b, n: (jnp.where(n == 0, 1, 0), 0, 0, 0))],
        out_specs=pl.BlockSpec((BLOCK, D_B), lambda b, n: (b * nb + n, 0)),
        out_shape=jax.ShapeDtypeStruct((t, D_B), BF16),
        compiler_params=_cparams(("parallel", "arbitrary"), 32),
        name="swa",
    )(sinks, q, kv, kv, kv, kv, bias)


def _post_mix_kernel(x_ref, mix_ref, ge_ref, be_ref, g1_ref, b1_ref, mod_ref, x1_ref, u_ref):
    xn = _layer_norm(x_ref[...], ge_ref[...], be_ref[...])
    z = ALPHA * xn + (1.0 + mod_ref[2:3, :]) * mix_ref[...].astype(F32)
    x1 = _layer_norm(z, g1_ref[...], b1_ref[...])
    x1_ref[...] = x1
    u_ref[...] = (x1 * (1.0 + mod_ref[4:5, :]) + mod_ref[3:4, :]).astype(u_ref.dtype)


def _post_mix(x2, mix, ge, be, g1, b1, mod, seq, tr=128):
    t, d = x2.shape
    per = seq // tr
    row = pl.BlockSpec((tr, d), lambda i: (i, 0))
    vec = pl.BlockSpec((1, d), lambda i: (0, 0))
    return pl.pallas_call(
        _post_mix_kernel,
        grid=(t // tr,),
        in_specs=[row, row, vec, vec, vec, vec,
                  pl.BlockSpec((None, 6, d), lambda i: (i // per, 0, 0))],
        out_specs=[row, row],
        out_shape=[jax.ShapeDtypeStruct((t, d), F32), jax.ShapeDtypeStruct((t, d), BF16)],
        compiler_params=_cparams(("parallel",), 32),
        name="post_mix",
    )(x2, mix, ge, be, g1, b1, mod)


def _final_kernel(x1_ref, h_ref, g2_ref, b2_ref, mod_ref, o_ref):
    z = ALPHA * x1_ref[...] + (1.0 + mod_ref[5:6, :]) * h_ref[...].astype(F32)
    o_ref[...] = _layer_norm(z, g2_ref[...], b2_ref[...])


def _final(x1, h, g2, b2, mod, seq, tr=128):
    t, d = x1.shape
    per = seq // tr
    row = pl.BlockSpec((tr, d), lambda i: (i, 0))
    vec = pl.BlockSpec((1, d), lambda i: (0, 0))
    return pl.pallas_call(
        _final_kernel,
        grid=(t // tr,),
        in_specs=[row, row, vec, vec, pl.BlockSpec((None, 6, d), lambda i: (i // per, 0, 0))],
        out_specs=row,
        out_shape=jax.ShapeDtypeStruct((t, d), F32),
        compiler_params=_cparams(("parallel",), 32),
        name="final_ln",
    )(x1, h, g2, b2, mod)


def _lora_weights(w_decay_up, w_iclr_up, w_gate_up, cb):
    zd = jnp.zeros((LORA_COLS, D_A), F32)
    wd = zd.at[0:DECAY_LORA].set(w_decay_up)
    wa = zd.at[DECAY_LORA:DECAY_LORA + ICLR_LORA].set(w_iclr_up)
    wg = zd.at[DECAY_LORA + ICLR_LORA:].set(w_gate_up)
    nb = D_A // cb
    parts = [w.reshape(LORA_COLS, nb, cb) for w in (wd, wa, wg)]
    return jnp.transpose(jnp.concatenate(parts, axis=2), (1, 0, 2))


def kernel(x, c, ln_emb_g, ln_emb_b, rpb_table, w_mod, b_mod, w_in, mu_shift, w0, w_decay_up, a0,
           w_iclr_up, w_gate_up, k_k, k_a, r_k, lnx_g, lnx_b, attn_sinks, w_out, ln1_g, ln1_b,
           w_up, w_down, ln2_g, ln2_b):
    bsz, seq, d = x.shape
    assert w_mod.shape[0] == DEPTH == 1 and d == D_MODEL and bsz <= 8
    t = bsz * seq
    row = lambda a: a.reshape(1, -1)
    x2 = x.reshape(t, d)
    c8 = jnp.pad(c, ((0, 8 - bsz), (0, 0)))
    bias = _attn_bias(rpb_table)
    mod = _modulation(c8, w_mod[0], row(b_mod[0]))[:bsz].reshape(bsz, 6, d)
    u1 = _ln_mod(x2, row(ln_emb_g), row(ln_emb_b), mod, seq)
    tn_in = 1280
    p = _matmul(u1, w_in[0].astype(BF16), tm=1024, tn=tn_in, col_shift=RWKV_COLS // tn_in,
                a_buffers=1, name="in_proj")
    rwkv_off = N_IN - RWKV_COLS
    w_lora = _lora_weights(w_decay_up[0], w_iclr_up[0], w_gate_up[0], 256).astype(BF16)
    r, k, v, al, be, lw, g = _rwkv_prep(p, row(mu_shift[0]), row(w0[0]), row(a0[0]),
                                        row(k_k[0]), row(k_a[0]), w_lora, seq, rwkv_off)
    y_a = _rwkv_scan(r, k, v, al, be, lw, g, row(r_k[0]), row(lnx_g[0]), row(lnx_b[0]),
                     bsz, seq)
    y_b = _swa(p, p, bias, row(attn_sinks[0]), bsz, seq, 0, D_B // (H_KV * HEAD))
    mix = _matmul([y_a, y_b], w_out[0], tm=1024, tn=512, out_dtype=BF16, name="out_proj")
    x1, u2 = _post_mix(x2, mix, row(ln_emb_g), row(ln_emb_b), row(ln1_g[0]), row(ln1_b[0]),
                       mod, seq)
    hmid, wd = _matmul(u2, w_up[0], tm=1024, tn=512, out_dtype=BF16, relu2=True,
                       convert=w_down[0], name="mlp_up")
    hout = _matmul(hmid, wd, tm=1024, tn=1024, tk=4096, out_dtype=BF16, name="mlp_down")
    out = _final(x1, hout, row(ln2_g[0]), row(ln2_b[0]), mod, seq)
    return out.reshape(bsz, seq, d)
```

```python
import functools
import math

import numpy as np
import jax
import jax.numpy as jnp
from jax import lax
from jax.experimental import pallas as pl
from jax.experimental.pallas import tpu as pltpu

F32 = jnp.float32
BF16 = jnp.bfloat16

D_MODEL = 4096
HEAD = 64
D_A = D_MODEL // 2
D_B = D_MODEL - D_A
H_A = D_A // HEAD
H_Q = D_B // HEAD
GQA = 8
H_KV = H_Q // GQA
WINDOW = 128
BLOCK = 128
RPB_BUCKETS = 32
RPB_MAX_EXACT = RPB_BUCKETS // 2
RPB_MAX_DIST = 128
DECAY_LORA = max(32, int(round(D_A ** 0.5 * 1.8 / 32)) * 32)
ICLR_LORA = max(32, int(round(D_A ** 0.5 * 1.8 / 32)) * 32)
GATE_LORA = max(32, int(round(D_A ** 0.6 * 0.8 / 32)) * 32)
LORA_COLS = DECAY_LORA + ICLR_LORA + GATE_LORA
D_FF = 4 * D_MODEL
DEPTH = 1
ALPHA = (2.0 * DEPTH) ** 0.25
LN_EPS = 1e-5
LNX_EPS = 64e-5
OFF_W = 3 * D_A
RWKV_COLS = OFF_W + LORA_COLS
OFF_Q = RWKV_COLS
OFF_KB = OFF_Q + D_B
OFF_VB = OFF_KB + H_KV * HEAD
N_IN = OFF_VB + H_KV * HEAD
NEG = -1e30

CHUNK = 64
PAIR = 2 * HEAD
NPAIR = H_A // 2
BF16_ROWS = 16
VMEM_CAP = 56 * 1024 * 1024


def _cparams(sem, vmem_mb):
    return pltpu.CompilerParams(dimension_semantics=sem,
                                vmem_limit_bytes=min(int(vmem_mb * 1024 * 1024), VMEM_CAP))


def _dot(a, b):
    return jnp.dot(a.astype(BF16), b.astype(BF16), preferred_element_type=F32)


def _dot_nt(a, b):
    return lax.dot_general(a.astype(BF16), b.astype(BF16), (((1,), (1,)), ((), ())),
                           preferred_element_type=F32)


def _split2(x):
    hi = x.astype(BF16)
    lo = (x - hi.astype(F32)).astype(BF16)
    return hi, lo


def _dot3(a, b):
    ah, al = _split2(a)
    bh, bl = _split2(b)
    return (jnp.dot(ah, bh, preferred_element_type=F32)
            + jnp.dot(ah, bl, preferred_element_type=F32)
            + jnp.dot(al, bh, preferred_element_type=F32))


def _layer_norm(x, g, b):
    mu = jnp.mean(x, axis=-1, keepdims=True)
    xc = x - mu
    var = jnp.mean(xc * xc, axis=-1, keepdims=True)
    return xc * lax.rsqrt(var + LN_EPS) * g + b


def _head_ones(n):
    r = lax.broadcasted_iota(jnp.int32, (n, n), 0)
    c = lax.broadcasted_iota(jnp.int32, (n, n), 1)
    return ((r >> 6) == (c >> 6)).astype(BF16)


def _mod_kernel(c_ref, w_ref, b_ref, o_ref):
    c = c_ref[...]
    cond = c * jax.nn.sigmoid(c)
    o_ref[...] = _dot3(cond, w_ref[...]) + b_ref[...]


def _modulation(c8, w_mod, b_mod, tn=512):
    d, n = w_mod.shape
    return pl.pallas_call(
        _mod_kernel,
        grid=(n // tn,),
        in_specs=[pl.BlockSpec((8, d), lambda j: (0, 0)),
                  pl.BlockSpec((d, tn), lambda j: (0, j)),
                  pl.BlockSpec((1, tn), lambda j: (0, j))],
        out_specs=pl.BlockSpec((8, tn), lambda j: (0, j)),
        out_shape=jax.ShapeDtypeStruct((8, n), F32),
        compiler_params=_cparams(("parallel",), 40),
        name="modulation",
    )(c8, w_mod, b_mod)


def _ln_mod_kernel(x_ref, g_ref, b_ref, mod_ref, u_ref):
    xn = _layer_norm(x_ref[...], g_ref[...], b_ref[...])
    u_ref[...] = (xn * (1.0 + mod_ref[1:2, :]) + mod_ref[0:1, :]).astype(u_ref.dtype)


def _ln_mod(x2, g, b, mod, seq, tr=256):
    t, d = x2.shape
    per = seq // tr
    return pl.pallas_call(
        _ln_mod_kernel,
        grid=(t // tr,),
        in_specs=[pl.BlockSpec((tr, d), lambda i: (i, 0)),
                  pl.BlockSpec((1, d), lambda i: (0, 0)),
                  pl.BlockSpec((1, d), lambda i: (0, 0)),
                  pl.BlockSpec((None, 6, d), lambda i: (i // per, 0, 0))],
        out_specs=pl.BlockSpec((tr, d), lambda i: (i, 0)),
        out_shape=jax.ShapeDtypeStruct((t, d), BF16),
        compiler_params=_cparams(("parallel",), 32),
        name="ln_mod",
    )(x2, g, b, mod)


def _mm_kernel(*refs, relu2, convert):
    if convert:
        *refs, ci_ref, o_ref, co_ref = refs
        co_ref[...] = ci_ref[...].astype(co_ref.dtype)
        refs = (*refs, o_ref)
    *a_refs, b_ref, o_ref = refs
    acc, off = None, 0
    for a_ref in a_refs:
        kd = a_ref.shape[1]
        part = jnp.dot(a_ref[...], b_ref[off:off + kd, :].astype(BF16), preferred_element_type=F32)
        acc = part if acc is None else acc + part
        off += kd
    if relu2:
        acc = jnp.square(jnp.maximum(acc, 0.0))
    o_ref[...] = acc.astype(o_ref.dtype)


def _mm_acc_kernel(a_ref, b_ref, o_ref, acc_ref):
    k = pl.program_id(2)

    @pl.when(k == 0)
    def _():
        acc_ref[...] = jnp.zeros_like(acc_ref)

    acc_ref[...] += jnp.dot(a_ref[...], b_ref[...].astype(BF16), preferred_element_type=F32)

    @pl.when(k == pl.num_programs(2) - 1)
    def _():
        o_ref[...] = acc_ref[...].astype(o_ref.dtype)


def _matmul(a, b, *, tm, tn, tk=None, out_dtype=F32, relu2=False, convert=None, name="matmul"):
    a = a if isinstance(a, (list, tuple)) else [a]
    m = a[0].shape[0]
    kd, n = b.shape
    tm = min(tm, m)
    nj = n // tn
    osz = jnp.dtype(out_dtype).itemsize
    bsz = jnp.dtype(b.dtype).itemsize
    bsz = 2 * bsz + (2 if bsz == 4 else 0)
    if tk is None or tk >= kd:
        vm = (2 * tm * kd * 2 + kd * tn * bsz + 2 * tm * tn * osz + tm * tn * 4) / 2 ** 20 + 8
        in_specs = ([pl.BlockSpec((tm, x.shape[1]), lambda i, j: (i, 0)) for x in a]
                    + [pl.BlockSpec((kd, tn), lambda i, j: (0, j))])
        out_specs = pl.BlockSpec((tm, tn), lambda i, j: (i, j))
        out_shape = jax.ShapeDtypeStruct((m, n), out_dtype)
        args = (*a, b)
        if convert is not None:
            cr, cc = convert.shape
            slab = cr // ((m // tm) * nj)
            assert slab * (m // tm) * nj == cr and slab % BF16_ROWS == 0
            cspec = pl.BlockSpec((slab, cc), lambda i, j: (i * nj + j, 0))
            in_specs, args = in_specs + [cspec], (*args, convert)
            out_specs = [out_specs, cspec]
            out_shape = [out_shape, jax.ShapeDtypeStruct((cr, cc), BF16)]
            vm += slab * cc * 12 / 2 ** 20
        return pl.pallas_call(
            functools.partial(_mm_kernel, relu2=relu2, convert=convert is not None),
            grid=(m // tm, nj),
            in_specs=in_specs,
            out_specs=out_specs,
            out_shape=out_shape,
            compiler_params=_cparams(("parallel", "parallel"), vm),
            name=name,
        )(*args)
    assert not relu2 and len(a) == 1 and convert is None
    vm = (2 * tm * tk * 2 + tk * tn * bsz + 2 * tm * tn * osz + 2 * tm * tn * 4) / 2 ** 20 + 8
    return pl.pallas_call(
        _mm_acc_kernel,
        grid=(m // tm, nj, kd // tk),
        in_specs=[pl.BlockSpec((tm, tk), lambda i, j, k: (i, k)),
                  pl.BlockSpec((tk, tn), lambda i, j, k: (k, j))],
        out_specs=pl.BlockSpec((tm, tn), lambda i, j, k: (i, j)),
        out_shape=jax.ShapeDtypeStruct((m, n), out_dtype),
        scratch_shapes=[pltpu.VMEM((tm, tn), F32)],
        compiler_params=_cparams(("parallel", "parallel", "arbitrary"), vm),
        name=name,
    )(a[0], b)


def _scan_chunk(ins, hts, consts, tick):
    m0, strict, incl, eye, tri = consts

    ticking = [False]

    def each(f, *ls):
        out = [f(*a) for a in zip(*ls)]
        if ticking[0]:
            tick()
        return out

    r, k, v, al, be, lw = (list(z) for z in zip(*ins))

    def cumsum(x):
        hi = x.astype(BF16)
        rem = x - hi.astype(F32)
        mid = rem.astype(BF16)
        lo = (rem - mid.astype(F32)).astype(BF16)
        c3 = jnp.dot(tri, jnp.concatenate([hi, mid, lo], axis=1), preferred_element_type=F32)
        return c3[:, 0:PAIR] + c3[:, PAIR:2 * PAIR] + c3[:, 2 * PAIR:3 * PAIR]

    def sm(x):
        return jnp.concatenate([jnp.where(m0, x, 0.0), jnp.where(m0, 0.0, x)], axis=0)

    b16 = lambda t_: t_.astype(BF16)
    nn = lambda a_, b_: jnp.dot(a_, b_, preferred_element_type=F32)
    c = each(cumsum, lw)
    pc = each(lambda c_: jnp.exp(c_[CHUNK - 1:CHUNK, :]), c)
    einv = each(lambda c_: jnp.exp(-c_), c)
    a_sm = each(lambda a_, c_, l_: b16(sm(a_ * jnp.exp(c_ - l_))), al, c, lw)
    r_sm = each(lambda r_, c_: sm(r_ * jnp.exp(c_)), r, c)
    v_sm = each(lambda v_: b16(sm(v_)), v)
    b_t = each(lambda b_, e_: b_ * e_, be, einv)
    k_t = each(lambda k_, e_: k_ * e_, k, einv)

    def scores(a_, r_, b_, k_):
        bb, kb = b16(b_), b16(k_)
        return _dot_nt(jnp.concatenate([a_, b16(r_)], axis=0),
                       jnp.concatenate([bb, bb, kb, kb], axis=0))

    s = each(scores, a_sm, r_sm, b_t, k_t)
    lab = each(lambda s_: jnp.where(strict, s_[0:PAIR, 0:PAIR], 0.0), s)
    mak = each(lambda s_: b16(jnp.where(strict, s_[0:PAIR, PAIR:], 0.0)), s)
    incl2 = jnp.concatenate([incl, incl], axis=1)
    mrbk = each(lambda s_: b16(jnp.where(incl2, s_[PAIR:, :], 0.0)), s)

    ticking[0] = True
    ldt = each(lambda l_: l_.T, lab)
    xt = each(lambda l_: jnp.where(eye, 1.0, l_), ldt)
    lt = each(lambda l_: nn(b16(l_), b16(l_)), ldt)

    def series_step(l_, x_):
        lb = b16(l_)
        return nn(lb, jnp.concatenate([b16(x_), lb], axis=1))

    for _ in range(4):
        xl = each(series_step, lt, xt)
        xt = each(lambda x_, p_: x_ + p_[:, 0:PAIR], xt, xl)
        lt = each(lambda p_: p_[:, PAIR:], xl)
    x = each(lambda x_, l_: b16((x_ + nn(b16(l_), b16(x_))).T), xt, lt)

    makv = each(nn, mak, v_sm)
    wu = each(lambda x_, a_, m_: nn(x_, jnp.concatenate([a_, b16(m_)], axis=1)), x, a_sm, makv)
    bigr = each(lambda wu_, v_: jnp.concatenate(
        [b16(wu_), jnp.concatenate([jnp.zeros_like(v_), v_], axis=1)], axis=0), wu, v_sm)
    bk = each(lambda b_, k_, p_: b16(jnp.concatenate([sm(b_ * p_), sm(k_ * p_)], axis=0).T),
              b_t, k_t, pc)
    gz = each(nn, bk, bigr)
    qy = each(nn, mrbk, bigr)
    hb = each(b16, hts)

    def new_state(ht, h_, g_, p_):
        return ht * p_ + _dot_nt(h_, g_[:, 0:PAIR]) + g_[:, PAIR:].T

    def output(r_, q_, h_):
        y_sm = _dot_nt(r_ + q_[:, 0:PAIR], h_) + q_[:, PAIR:]
        return y_sm[0:CHUNK, :] + y_sm[CHUNK:, :]

    return each(output, r_sm, qy, hb), each(new_state, hts, hb, gz, pc)


def _rwkv_kernel(pr_ref, pk_ref, pv_ref, pc_ref, qr_ref, qk_ref, qv_ref, qc_ref,
                 mr_ref, mk_ref, mv_ref, mc_ref, w0_ref, a0_ref, kk_ref, ka_ref, wl_ref,
                 rk_ref, lg_ref, lb_ref, o_ref,
                 h_ref, y_ref, r_s, k_s, v_s, al_s, be_s, lw_s, g_s, *, nchunk):
    first = pl.program_id(1) == 0

    @pl.when(first)
    def _():
        h_ref[...] = jnp.zeros_like(h_ref)

    row = lax.broadcasted_iota(jnp.int32, (PAIR, PAIR), 0)
    col = lax.broadcasted_iota(jnp.int32, (PAIR, PAIR), 1)
    same = (row >> 6) == (col >> 6)
    tr_ = row & (CHUNK - 1)
    tc_ = col & (CHUNK - 1)
    strict = jnp.logical_and(same, tc_ < tr_)
    incl = jnp.logical_and(same, tc_ <= tr_)
    eye = row == col
    m0 = lax.broadcasted_iota(jnp.int32, (CHUNK, PAIR), 1) < HEAD
    tri = (lax.broadcasted_iota(jnp.int32, (CHUNK, CHUNK), 1)
           <= lax.broadcasted_iota(jnp.int32, (CHUNK, CHUNK), 0)).astype(BF16)
    consts = (m0, strict, incl, eye, tri)
    ones = _head_ones(2 * PAIR)
    slab = 4 * PAIR

    def prep_stages(cn):
        at_start = isinstance(cn, int)
        rows = pl.ds(0, CHUNK) if at_start else pl.ds(pl.multiple_of(cn * CHUNK, CHUNK), CHUNK)

        def shift(x_ref, q_ref, m_ref, ln):
            x = x_ref[rows, ln].astype(F32)
            if at_start:
                last = jnp.where(first, 0.0, q_ref[BF16_ROWS - 1:BF16_ROWS, ln].astype(F32))
            else:
                tail = pl.ds(pl.multiple_of(cn * CHUNK - BF16_ROWS, BF16_ROWS), BF16_ROWS)
                last = x_ref[tail, ln][BF16_ROWS - 1:BF16_ROWS, :].astype(F32)
            prev = pltpu.roll(x, 1, axis=0)
            r8 = lax.broadcasted_iota(jnp.int32, (8, x.shape[1]), 0)
            head = jnp.where(r8 == 0, jnp.broadcast_to(last, (8, x.shape[1])), prev[0:8, :])
            prev = jnp.concatenate([head, prev[8:, :]], axis=0)
            return x + (prev - x) * m_ref[:, ln]

        code = shift(pc_ref, qc_ref, mc_ref, slice(None))
        lane = lax.broadcasted_iota(jnp.int32, code.shape, 1)
        act = jnp.where(lane < DECAY_LORA, jnp.tanh(code),
                        jnp.where(lane < DECAY_LORA + ICLR_LORA, code, jax.nn.sigmoid(code)))
        up = jnp.dot(act.astype(BF16), wl_ref[...], preferred_element_type=F32)
        yield
        for j in range(D_A // slab):
            ln = slice(j * slab, (j + 1) * slab)
            r_s[rows, ln] = shift(pr_ref, qr_ref, mr_ref, ln)
            v_s[rows, ln] = shift(pv_ref, qv_ref, mv_ref, ln)
            yield
            k = shift(pk_ref, qk_ref, mk_ref, ln)
            kk = k * kk_ref[:, ln]
            kk2 = kk * kk
            ss = jnp.concatenate([_dot(kk2[:, i * 2 * PAIR:(i + 1) * 2 * PAIR], ones)
                                  for i in range(slab // (2 * PAIR))], axis=1)
            yield
            a = jax.nn.sigmoid(a0_ref[:, ln] + up[:, D_A + j * slab:D_A + (j + 1) * slab])
            kk = kk * lax.rsqrt(jnp.maximum(ss, 1e-24))
            k_s[rows, ln] = k * (1.0 + (a - 1.0) * ka_ref[:, ln])
            al_s[rows, ln] = -kk
            be_s[rows, ln] = kk * a
            lw_s[rows, ln] = -math.exp(-0.5) * jax.nn.sigmoid(
                w0_ref[:, ln] + up[:, j * slab:(j + 1) * slab])
            g_s[rows, ln] = up[:, 2 * D_A + j * slab:2 * D_A + (j + 1) * slab]
            yield

    for _ in prep_stages(0):
        pass

    def chunk_body(ci, carry):
        rows = pl.ds(pl.multiple_of(ci * CHUNK, CHUNK), CHUNK)
        lanes = [slice(p * PAIR, (p + 1) * PAIR) for p in range(NPAIR)]
        ins = [tuple(s[rows, ln] for s in (r_s, k_s, v_s, al_s, be_s, lw_s)) for ln in lanes]
        hts = [h_ref[p] for p in range(NPAIR)]
        gen = prep_stages(jnp.minimum(ci + 1, nchunk - 1))
        ys, hns = _scan_chunk(ins, hts, consts, lambda: next(gen, None))
        for _ in gen:
            pass
        for p in range(NPAIR):
            h_ref[p] = hns[p]
            y_ref[rows, lanes[p]] = ys[p]
        return carry

    lax.fori_loop(0, nchunk, chunk_body, 0)

    for q in range(NPAIR // 2):
        ln = slice(q * 2 * PAIR, (q + 1) * 2 * PAIR)
        y = y_ref[:, ln]
        mean = _dot(y, ones) * (1.0 / HEAD)
        yc = y - mean
        var = _dot(yc * yc, ones) * (1.0 / HEAD)
        yn = yc * lax.rsqrt(var + LNX_EPS) * lg_ref[:, ln] + lb_ref[:, ln]
        rk = _dot(r_s[:, ln] * k_s[:, ln] * rk_ref[:, ln], ones)
        o_ref[:, ln] = ((yn + rk * v_s[:, ln]) * g_s[:, ln]).astype(o_ref.dtype)


def _rwkv(p, mu, w0, a0, k_k, k_a, w_lora, r_k, lnx_g, lnx_b, bsz, seq, tb=256):
    t = p.shape[0]
    per = seq // tb
    nchunk = tb // CHUNK
    assert nchunk >= 2
    cblk = OFF_W // LORA_COLS
    tail = tb // BF16_ROWS

    def cur(j):
        return pl.BlockSpec((tb, D_A), lambda b, s: (b * per + s, j))

    def prev(j):
        return pl.BlockSpec((BF16_ROWS, D_A),
                            lambda b, s: (jnp.maximum((b * per + s) * tail - 1, 0), j))

    def vec(j):
        return pl.BlockSpec((1, D_A), lambda b, s: (0, j))

    in_specs = [cur(0), cur(1), cur(2),
                pl.BlockSpec((tb, LORA_COLS), lambda b, s: (b * per + s, cblk)),
                prev(0), prev(1), prev(2),
                pl.BlockSpec((BF16_ROWS, LORA_COLS),
                             lambda b, s: (jnp.maximum((b * per + s) * tail - 1, 0), cblk)),
                vec(0), vec(1), vec(2),
                pl.BlockSpec((1, LORA_COLS), lambda b, s: (0, cblk)),
                vec(0), vec(0), vec(0), vec(0),
                pl.BlockSpec((LORA_COLS, 3 * D_A), lambda b, s: (0, 0)),
                vec(0), vec(0), vec(0)]
    blk = (tb, D_A)
    return pl.pallas_call(
        functools.partial(_rwkv_kernel, nchunk=nchunk),
        grid=(bsz, per),
        in_specs=in_specs,
        out_specs=pl.BlockSpec(blk, lambda b, s: (b * per + s, 0)),
        out_shape=jax.ShapeDtypeStruct((t, D_A), BF16),
        scratch_shapes=[pltpu.VMEM((NPAIR, PAIR, PAIR), F32)] + [pltpu.VMEM(blk, F32)] * 8,
        compiler_params=_cparams(("parallel", "arbitrary"), 52),
        name="rwkv",
    )(p, p, p, p, p, p, p, p, mu, mu, mu, mu, w0, a0, k_k, k_a, w_lora, r_k, lnx_g, lnx_b)


def _bucket_table():
    qi = np.arange(BLOCK)[:, None]
    kj = np.arange(2 * BLOCK)[None, :]
    dist = qi + BLOCK - kj
    n = np.maximum(dist, 0)
    nf = np.maximum(n, 1).astype(np.float32)
    large = RPB_MAX_EXACT + (np.log(nf / np.float32(RPB_MAX_EXACT))
                             / np.float32(math.log(RPB_MAX_DIST / RPB_MAX_EXACT))
                             * np.float32(RPB_BUCKETS - RPB_MAX_EXACT)).astype(np.int32)
    large = np.minimum(large, RPB_BUCKETS - 1)
    bucket = np.where(n < RPB_MAX_EXACT, n, large)
    valid = (dist >= 0) & (dist < WINDOW)
    return np.where(valid, bucket, -1).astype(np.int32)


def _bias_kernel(tab_ref, bkt_ref, o_ref):
    first = pl.program_id(0) == 1
    h = pl.program_id(1)
    bkt = bkt_ref[...]
    acc = jnp.full(bkt.shape, NEG, F32)
    for b in range(RPB_BUCKETS):
        acc = jnp.where(bkt == b, tab_ref[b, h], acc)
    col = lax.broadcasted_iota(jnp.int32, bkt.shape, 1)
    o_ref[...] = jnp.where(jnp.logical_and(first, col < BLOCK), NEG, acc)


def _attn_bias(rpb_table):
    bkt = jnp.asarray(_bucket_table())
    return pl.pallas_call(
        _bias_kernel,
        grid=(2, H_Q),
        in_specs=[pl.BlockSpec(memory_space=pltpu.SMEM),
                  pl.BlockSpec((BLOCK, 2 * BLOCK), lambda f, h: (0, 0))],
        out_specs=pl.BlockSpec((None, None, BLOCK, 2 * BLOCK), lambda f, h: (f, h, 0, 0)),
        out_shape=jax.ShapeDtypeStruct((2, H_Q, BLOCK, 2 * BLOCK), F32),
        compiler_params=_cparams(("arbitrary", "arbitrary"), 16),
        name="attn_bias",
    )(rpb_table, bkt)


def _swa_kernel(sink_ref, *refs):
    *q_refs, kc_ref, kp_ref, vc_ref, vp_ref, bias_ref, o_ref = refs
    each = lambda f, *ls: [f(*a) for a in zip(*ls)]
    lo = lax.broadcasted_iota(jnp.int32, (BLOCK, PAIR), 1) < HEAD
    scale = HEAD ** -0.5
    zeros = jnp.zeros((2 * BLOCK, HEAD), BF16)
    ones = jnp.ones((2 * BLOCK, HEAD), BF16)
    npr = GQA // 2
    kcat = jnp.concatenate([kp_ref[...], kc_ref[...]], axis=0).astype(F32) * scale
    vcat = jnp.concatenate([vp_ref[...], vc_ref[...]], axis=0).astype(F32)
    for g in range(H_KV):
        gsl = slice(g * HEAD, (g + 1) * HEAD)
        kg = kcat[:, gsl].astype(BF16)
        vg = vcat[:, gsl].astype(BF16)
        kdup = jnp.concatenate([kg, kg], axis=1)
        rhs = jnp.concatenate([jnp.concatenate([vg, zeros, ones, zeros], axis=1),
                               jnp.concatenate([zeros, vg, zeros, ones], axis=1)], axis=0)
        heads = [g * GQA + 2 * i for i in range(npr)]
        lanes = [slice(h * HEAD, (h + 2) * HEAD) for h in heads]
        qp = [q_refs[h // 4][:, (h % 4) * HEAD:(h % 4 + 2) * HEAD] for h in heads]
        s2 = each(lambda q_: _dot_nt(jnp.concatenate([jnp.where(lo, q_, 0.0).astype(BF16),
                                                      jnp.where(lo, 0.0, q_).astype(BF16)], axis=0),
                                     kdup), qp)
        s = [(s_[0:BLOCK] + bias_ref[h], s_[BLOCK:] + bias_ref[h + 1]) for s_, h in zip(s2, heads)]
        m = [(jnp.maximum(jnp.max(a, axis=-1, keepdims=True), sink_ref[0, h]),
              jnp.maximum(jnp.max(b, axis=-1, keepdims=True), sink_ref[0, h + 1]))
             for (a, b), h in zip(s, heads)]
        e = each(lambda s_, m_: jnp.concatenate([jnp.exp(s_[0] - m_[0]), jnp.exp(s_[1] - m_[1])],
                                                axis=1).astype(BF16), s, m)
        od = each(lambda e_: jnp.dot(e_, rhs, preferred_element_type=F32), e)
        for o_, m_, h, ln in zip(od, m, heads, lanes):
            den = o_[:, PAIR:] + jnp.where(lo, jnp.exp(sink_ref[0, h] - m_[0]),
                                           jnp.exp(sink_ref[0, h + 1] - m_[1]))
            o_ref[:, ln] = (o_[:, 0:PAIR] / den).astype(o_ref.dtype)


def _swa(p, bias, sinks, bsz, seq):
    t = p.shape[0]
    nb = seq // BLOCK
    kvw = H_KV * HEAD
    nq = D_B // kvw

    def cur(c):
        return pl.BlockSpec((BLOCK, kvw), lambda b, n: (b * nb + n, c))

    def prev(c):
        return pl.BlockSpec((BLOCK, kvw), lambda b, n: (b * nb + jnp.maximum(n - 1, 0), c))

    kb, vb = OFF_KB // kvw, OFF_VB // kvw
    return pl.pallas_call(
        _swa_kernel,
        grid=(bsz, nb),
        in_specs=[pl.BlockSpec(memory_space=pltpu.SMEM)]
        + [cur(OFF_Q // kvw + i) for i in range(nq)]
        + [cur(kb), prev(kb), cur(vb), prev(vb),
           pl.BlockSpec((None, H_Q, BLOCK, 2 * BLOCK),
                        lambda b, n: (jnp.where(n == 0, 1, 0), 0, 0, 0))],
        out_specs=pl.BlockSpec((BLOCK, D_B), lambda b, n: (b * nb + n, 0)),
        out_shape=jax.ShapeDtypeStruct((t, D_B), BF16),
        compiler_params=_cparams(("parallel", "arbitrary"), 32),
        name="swa",
    )(sinks, *([p] * (nq + 4)), bias)


def _post_mix_kernel(x_ref, mix_ref, ge_ref, be_ref, g1_ref, b1_ref, mod_ref, x1_ref, u_ref):
    xn = _layer_norm(x_ref[...], ge_ref[...], be_ref[...])
    z = ALPHA * xn + (1.0 + mod_ref[2:3, :]) * mix_ref[...].astype(F32)
    x1 = _layer_norm(z, g1_ref[...], b1_ref[...])
    x1_ref[...] = x1
    u_ref[...] = (x1 * (1.0 + mod_ref[4:5, :]) + mod_ref[3:4, :]).astype(u_ref.dtype)


def _post_mix(x2, mix, ge, be, g1, b1, mod, seq, tr=128):
    t, d = x2.shape
    per = seq // tr
    row = pl.BlockSpec((tr, d), lambda i: (i, 0))
    vec = pl.BlockSpec((1, d), lambda i: (0, 0))
    return pl.pallas_call(
        _post_mix_kernel,
        grid=(t // tr,),
        in_specs=[row, row, vec, vec, vec, vec,
                  pl.BlockSpec((None, 6, d), lambda i: (i // per, 0, 0))],
        out_specs=[row, row],
        out_shape=[jax.ShapeDtypeStruct((t, d), F32), jax.ShapeDtypeStruct((t, d), BF16)],
        compiler_params=_cparams(("parallel",), 32),
        name="post_mix",
    )(x2, mix, ge, be, g1, b1, mod)


def _final_kernel(x1_ref, h_ref, g2_ref, b2_ref, mod_ref, o_ref):
    z = ALPHA * x1_ref[...] + (1.0 + mod_ref[5:6, :]) * h_ref[...].astype(F32)
    o_ref[...] = _layer_norm(z, g2_ref[...], b2_ref[...])


def _final(x1, h, g2, b2, mod, seq, tr=128):
    t, d = x1.shape
    per = seq // tr
    row = pl.BlockSpec((tr, d), lambda i: (i, 0))
    vec = pl.BlockSpec((1, d), lambda i: (0, 0))
    return pl.pallas_call(
        _final_kernel,
        grid=(t // tr,),
        in_specs=[row, row, vec, vec, pl.BlockSpec((None, 6, d), lambda i: (i // per, 0, 0))],
        out_specs=row,
        out_shape=jax.ShapeDtypeStruct((t, d), F32),
        compiler_params=_cparams(("parallel",), 32),
        name="final_ln",
    )(x1, h, g2, b2, mod)


def _lora_weights(w_decay_up, w_iclr_up, w_gate_up):
    zd = jnp.zeros((LORA_COLS, D_A), F32)
    wd = zd.at[0:DECAY_LORA].set(w_decay_up)
    wa = zd.at[DECAY_LORA:DECAY_LORA + ICLR_LORA].set(w_iclr_up)
    wg = zd.at[DECAY_LORA + ICLR_LORA:].set(w_gate_up)
    return jnp.concatenate([wd, wa, wg], axis=1).astype(BF16)


def kernel(x, c, ln_emb_g, ln_emb_b, rpb_table, w_mod, b_mod, w_in, mu_shift, w0, w_decay_up, a0,
           w_iclr_up, w_gate_up, k_k, k_a, r_k, lnx_g, lnx_b, attn_sinks, w_out, ln1_g, ln1_b,
           w_up, w_down, ln2_g, ln2_b):
    bsz, seq, d = x.shape
    assert w_mod.shape[0] == DEPTH == 1 and d == D_MODEL and bsz <= 8
    t = bsz * seq
    row = lambda a: a.reshape(1, -1)
    x2 = x.reshape(t, d)
    c8 = jnp.pad(c, ((0, 8 - bsz), (0, 0)))
    bias = _attn_bias(rpb_table)
    mod = _modulation(c8, w_mod[0], row(b_mod[0]))[:bsz].reshape(bsz, 6, d)
    u1 = _ln_mod(x2, row(ln_emb_g), row(ln_emb_b), mod, seq)
    p = _matmul(u1, w_in[0].astype(BF16), tm=1024, tn=1280, out_dtype=BF16, name="in_proj")
    w_lora = _lora_weights(w_decay_up[0], w_iclr_up[0], w_gate_up[0])
    y_a = _rwkv(p, row(mu_shift[0]), row(w0[0]), row(a0[0]), row(k_k[0]), row(k_a[0]), w_lora,
                row(r_k[0]), row(lnx_g[0]), row(lnx_b[0]), bsz, seq)
    y_b = _swa(p, bias, row(attn_sinks[0]), bsz, seq)
    mix = _matmul([y_a, y_b], w_out[0], tm=1024, tn=512, out_dtype=BF16, name="out_proj")
    x1, u2 = _post_mix(x2, mix, row(ln_emb_g), row(ln_emb_b), row(ln1_g[0]), row(ln1_b[0]),
                       mod, seq)
    hmid, wd = _matmul(u2, w_up[0], tm=1024, tn=512, out_dtype=BF16, relu2=True,
                       convert=w_down[0], name="mlp_up")
    hout = _matmul(hmid, wd, tm=1024, tn=1024, tk=4096, out_dtype=BF16, name="mlp_down")
    out = _final(x1, hout, row(ln2_g[0]), row(ln2_b[0]), mod, seq)
    return out.reshape(bsz, seq, d)
```

```python
import functools
import math

import numpy as np
import jax
import jax.numpy as jnp
from jax import lax
from jax.experimental import pallas as pl
from jax.experimental.pallas import tpu as pltpu

F32 = jnp.float32
BF16 = jnp.bfloat16

D_MODEL = 4096
HEAD = 64
D_A = D_MODEL // 2
D_B = D_MODEL - D_A
H_A = D_A // HEAD
H_Q = D_B // HEAD
GQA = 8
H_KV = H_Q // GQA
WINDOW = 128
BLOCK = 128
RPB_BUCKETS = 32
RPB_MAX_EXACT = RPB_BUCKETS // 2
RPB_MAX_DIST = 128
DECAY_LORA = max(32, int(round(D_A ** 0.5 * 1.8 / 32)) * 32)
ICLR_LORA = max(32, int(round(D_A ** 0.5 * 1.8 / 32)) * 32)
GATE_LORA = max(32, int(round(D_A ** 0.6 * 0.8 / 32)) * 32)
LORA_COLS = DECAY_LORA + ICLR_LORA + GATE_LORA
D_FF = 4 * D_MODEL
DEPTH = 1
ALPHA = (2.0 * DEPTH) ** 0.25
LN_EPS = 1e-5
LNX_EPS = 64e-5
OFF_W = 3 * D_A
RWKV_COLS = OFF_W + LORA_COLS
OFF_Q = RWKV_COLS
OFF_KB = OFF_Q + D_B
OFF_VB = OFF_KB + H_KV * HEAD
N_IN = OFF_VB + H_KV * HEAD
NEG = -1e30

CHUNK = 64
PAIR = 2 * HEAD
NPAIR = H_A // 2
BF16_ROWS = 16
LANES = 128
VMEM_CAP = 56 * 1024 * 1024


def _cparams(sem, vmem_mb):
    return pltpu.CompilerParams(dimension_semantics=sem,
                                vmem_limit_bytes=min(int(vmem_mb * 1024 * 1024), VMEM_CAP))


def _dot(a, b):
    return jnp.dot(a.astype(BF16), b.astype(BF16), preferred_element_type=F32)


def _dot_nt(a, b):
    return lax.dot_general(a.astype(BF16), b.astype(BF16), (((1,), (1,)), ((), ())),
                           preferred_element_type=F32)


def _split2(x):
    hi = x.astype(BF16)
    lo = (x - hi.astype(F32)).astype(BF16)
    return hi, lo


def _dot3(a, b):
    ah, al = _split2(a)
    bh, bl = _split2(b)
    return (jnp.dot(ah, bh, preferred_element_type=F32)
            + jnp.dot(ah, bl, preferred_element_type=F32)
            + jnp.dot(al, bh, preferred_element_type=F32))


def _layer_norm(x, g, b):
    mu = jnp.mean(x, axis=-1, keepdims=True)
    xc = x - mu
    var = jnp.mean(xc * xc, axis=-1, keepdims=True)
    return xc * lax.rsqrt(var + LN_EPS) * g + b


def _ln_rows(src_ref, consume):
    groups = [slice(g * BF16_ROWS, (g + 1) * BF16_ROWS) for g in range(src_ref.shape[0] // BF16_ROWS)]
    mus = [jnp.mean(src_ref[rs, :], axis=-1, keepdims=True) for rs in groups]
    rstd = [lax.rsqrt(jnp.mean(jnp.square(src_ref[rs, :] - mu), axis=-1, keepdims=True) + LN_EPS)
            for rs, mu in zip(groups, mus)]
    for rs, mu, r in zip(groups, mus, rstd):
        consume(rs, (src_ref[rs, :] - mu) * r, mu, r)


def _head_ones(n):
    r = lax.broadcasted_iota(jnp.int32, (n, n), 0)
    c = lax.broadcasted_iota(jnp.int32, (n, n), 1)
    return ((r >> 6) == (c >> 6)).astype(BF16)


def _mod_kernel(c_ref, w_ref, b_ref, o_ref):
    c = c_ref[...]
    cond = c * jax.nn.sigmoid(c)
    o_ref[...] = _dot3(cond, w_ref[...]) + b_ref[...]


def _modulation(c8, w_mod, b_mod, tn=512):
    d, n = w_mod.shape
    return pl.pallas_call(
        _mod_kernel,
        grid=(n // tn,),
        in_specs=[pl.BlockSpec((8, d), lambda j: (0, 0)),
                  pl.BlockSpec((d, tn), lambda j: (0, j)),
                  pl.BlockSpec((1, tn), lambda j: (0, j))],
        out_specs=pl.BlockSpec((8, tn), lambda j: (0, j)),
        out_shape=jax.ShapeDtypeStruct((8, n), F32),
        compiler_params=_cparams(("parallel",), 40),
        name="modulation",
    )(c8, w_mod, b_mod)


def _ln_mod_kernel(x_ref, g_ref, b_ref, mod_ref, u_ref, mu_ref, rstd_ref):
    gain = 1.0 + mod_ref[1:2, :]
    scale = g_ref[...] * gain
    shift = b_ref[...] * gain + mod_ref[0:1, :]

    def consume(rs, xh, mu, rstd):
        u_ref[rs, :] = (xh * scale + shift).astype(u_ref.dtype)
        mu_ref[rs, :] = jnp.broadcast_to(mu, (BF16_ROWS, LANES))
        rstd_ref[rs, :] = jnp.broadcast_to(rstd, (BF16_ROWS, LANES))

    _ln_rows(x_ref, consume)


def _ln_mod(x2, g, b, mod, seq, tr=256):
    t, d = x2.shape
    per = seq // tr
    stat = pl.BlockSpec((tr, LANES), lambda i: (i, 0))
    return pl.pallas_call(
        _ln_mod_kernel,
        grid=(t // tr,),
        in_specs=[pl.BlockSpec((tr, d), lambda i: (i, 0)),
                  pl.BlockSpec((1, d), lambda i: (0, 0)),
                  pl.BlockSpec((1, d), lambda i: (0, 0)),
                  pl.BlockSpec((None, 6, d), lambda i: (i // per, 0, 0))],
        out_specs=[pl.BlockSpec((tr, d), lambda i: (i, 0)), stat, stat],
        out_shape=[jax.ShapeDtypeStruct((t, d), BF16), jax.ShapeDtypeStruct((t, LANES), F32),
                   jax.ShapeDtypeStruct((t, LANES), F32)],
        compiler_params=_cparams(("parallel",), 32),
        name="ln_mod",
    )(x2, g, b, mod)


def _mm_kernel(*refs, relu2, convert):
    if convert:
        *refs, ci_ref, o_ref, co_ref = refs
        co_ref[...] = ci_ref[...].astype(co_ref.dtype)
        refs = (*refs, o_ref)
    *a_refs, b_ref, o_ref = refs
    acc, off = None, 0
    for a_ref in a_refs:
        kd = a_ref.shape[1]
        part = jnp.dot(a_ref[...], b_ref[off:off + kd, :].astype(BF16), preferred_element_type=F32)
        acc = part if acc is None else acc + part
        off += kd
    if relu2:
        acc = jnp.square(jnp.maximum(acc, 0.0))
    o_ref[...] = acc.astype(o_ref.dtype)


def _mm_acc_kernel(a_ref, b_ref, o_ref, acc_ref):
    k = pl.program_id(2)

    @pl.when(k == 0)
    def _():
        acc_ref[...] = jnp.zeros_like(acc_ref)

    acc_ref[...] += jnp.dot(a_ref[...], b_ref[...].astype(BF16), preferred_element_type=F32)

    @pl.when(k == pl.num_programs(2) - 1)
    def _():
        o_ref[...] = acc_ref[...].astype(o_ref.dtype)


def _matmul(a, b, *, tm, tn, tk=None, out_dtype=F32, relu2=False, convert=None, name="matmul"):
    a = a if isinstance(a, (list, tuple)) else [a]
    m = a[0].shape[0]
    kd, n = b.shape
    tm = min(tm, m)
    nj = n // tn
    osz = jnp.dtype(out_dtype).itemsize
    bsz = jnp.dtype(b.dtype).itemsize
    bsz = 2 * bsz + (2 if bsz == 4 else 0)
    if tk is None or tk >= kd:
        vm = (2 * tm * kd * 2 + kd * tn * bsz + 2 * tm * tn * osz + tm * tn * 4) / 2 ** 20 + 8
        in_specs = ([pl.BlockSpec((tm, x.shape[1]), lambda i, j: (i, 0)) for x in a]
                    + [pl.BlockSpec((kd, tn), lambda i, j: (0, j))])
        out_specs = pl.BlockSpec((tm, tn), lambda i, j: (i, j))
        out_shape = jax.ShapeDtypeStruct((m, n), out_dtype)
        args = (*a, b)
        if convert is not None:
            cr, cc = convert.shape
            slab = cr // ((m // tm) * nj)
            assert slab * (m // tm) * nj == cr and slab % BF16_ROWS == 0
            cspec = pl.BlockSpec((slab, cc), lambda i, j: (i * nj + j, 0))
            in_specs, args = in_specs + [cspec], (*args, convert)
            out_specs = [out_specs, cspec]
            out_shape = [out_shape, jax.ShapeDtypeStruct((cr, cc), BF16)]
            vm += slab * cc * 12 / 2 ** 20
        return pl.pallas_call(
            functools.partial(_mm_kernel, relu2=relu2, convert=convert is not None),
            grid=(m // tm, nj),
            in_specs=in_specs,
            out_specs=out_specs,
            out_shape=out_shape,
            compiler_params=_cparams(("parallel", "parallel"), vm),
            name=name,
        )(*args)
    assert not relu2 and len(a) == 1 and convert is None
    vm = (2 * tm * tk * 2 + tk * tn * bsz + 2 * tm * tn * osz + 2 * tm * tn * 4) / 2 ** 20 + 8
    return pl.pallas_call(
        _mm_acc_kernel,
        grid=(m // tm, nj, kd // tk),
        in_specs=[pl.BlockSpec((tm, tk), lambda i, j, k: (i, k)),
                  pl.BlockSpec((tk, tn), lambda i, j, k: (k, j))],
        out_specs=pl.BlockSpec((tm, tn), lambda i, j, k: (i, j)),
        out_shape=jax.ShapeDtypeStruct((m, n), out_dtype),
        scratch_shapes=[pltpu.VMEM((tm, tn), F32)],
        compiler_params=_cparams(("parallel", "parallel", "arbitrary"), vm),
        name=name,
    )(a[0], b)


def _scan_chunk(ins, hts, consts, tick):
    m0, strict, incl, eye, tri = consts

    ticking = [False]

    def each(f, *ls):
        out = [f(*a) for a in zip(*ls)]
        if ticking[0]:
            tick()
        return out

    r, k, v, al, be, lw = (list(z) for z in zip(*ins))

    def cumsum(x):
        hi = x.astype(BF16)
        rem = x - hi.astype(F32)
        mid = rem.astype(BF16)
        lo = (rem - mid.astype(F32)).astype(BF16)
        c3 = jnp.dot(tri, jnp.concatenate([hi, mid, lo], axis=1), preferred_element_type=F32)
        return c3[:, 0:PAIR] + c3[:, PAIR:2 * PAIR] + c3[:, 2 * PAIR:3 * PAIR]

    def sm(x):
        return jnp.concatenate([jnp.where(m0, x, 0.0), jnp.where(m0, 0.0, x)], axis=0)

    b16 = lambda t_: t_.astype(BF16)
    nn = lambda a_, b_: jnp.dot(a_, b_, preferred_element_type=F32)
    c = each(cumsum, lw)
    pc = each(lambda c_: jnp.exp(c_[CHUNK - 1:CHUNK, :]), c)
    einv = each(lambda c_: jnp.exp(-c_), c)
    a_sm = each(lambda a_, c_, l_: b16(sm(a_ * jnp.exp(c_ - l_))), al, c, lw)
    r_sm = each(lambda r_, c_: sm(r_ * jnp.exp(c_)), r, c)
    v_sm = each(lambda v_: b16(sm(v_)), v)
    b_t = each(lambda b_, e_: b_ * e_, be, einv)
    k_t = each(lambda k_, e_: k_ * e_, k, einv)

    def scores(a_, r_, b_, k_):
        bb, kb = b16(b_), b16(k_)
        return _dot_nt(jnp.concatenate([a_, b16(r_)], axis=0),
                       jnp.concatenate([bb, bb, kb, kb], axis=0))

    s = each(scores, a_sm, r_sm, b_t, k_t)
    lab = each(lambda s_: jnp.where(strict, s_[0:PAIR, 0:PAIR], 0.0), s)
    mak = each(lambda s_: b16(jnp.where(strict, s_[0:PAIR, PAIR:], 0.0)), s)
    incl2 = jnp.concatenate([incl, incl], axis=1)
    mrbk = each(lambda s_: b16(jnp.where(incl2, s_[PAIR:, :], 0.0)), s)

    ticking[0] = True
    ldt = each(lambda l_: l_.T, lab)
    xt = each(lambda l_: jnp.where(eye, 1.0, l_), ldt)
    lt = each(lambda l_: nn(b16(l_), b16(l_)), ldt)

    def series_step(l_, x_):
        lb = b16(l_)
        return nn(lb, jnp.concatenate([b16(x_), lb], axis=1))

    for _ in range(4):
        xl = each(series_step, lt, xt)
        xt = each(lambda x_, p_: x_ + p_[:, 0:PAIR], xt, xl)
        lt = each(lambda p_: p_[:, PAIR:], xl)
    x = each(lambda x_, l_: b16((x_ + nn(b16(l_), b16(x_))).T), xt, lt)

    makv = each(nn, mak, v_sm)
    wu = each(lambda x_, a_, m_: nn(x_, jnp.concatenate([a_, b16(m_)], axis=1)), x, a_sm, makv)
    bigr = each(lambda wu_, v_: jnp.concatenate(
        [b16(wu_), jnp.concatenate([jnp.zeros_like(v_), v_], axis=1)], axis=0), wu, v_sm)
    bk = each(lambda b_, k_, p_: b16(jnp.concatenate([sm(b_ * p_), sm(k_ * p_)], axis=0).T),
              b_t, k_t, pc)
    gz = each(nn, bk, bigr)
    qy = each(nn, mrbk, bigr)
    hb = each(b16, hts)

    def new_state(ht, h_, g_, p_):
        return ht * p_ + _dot_nt(h_, g_[:, 0:PAIR]) + g_[:, PAIR:].T

    def output(r_, q_, h_):
        y_sm = _dot_nt(r_ + q_[:, 0:PAIR], h_) + q_[:, PAIR:]
        return y_sm[0:CHUNK, :] + y_sm[CHUNK:, :]

    return each(output, r_sm, qy, hb), each(new_state, hts, hb, gz, pc)


def _rwkv_kernel(pr_ref, pk_ref, pv_ref, pc_ref, qr_ref, qk_ref, qv_ref, qc_ref,
                 mr_ref, mk_ref, mv_ref, mc_ref, w0_ref, a0_ref, kk_ref, ka_ref, wl_ref,
                 rk_ref, lg_ref, lb_ref, o_ref,
                 h_ref, y_ref, r_s, k_s, v_s, al_s, be_s, lw_s, g_s, *, nchunk):
    first = pl.program_id(1) == 0

    @pl.when(first)
    def _():
        h_ref[...] = jnp.zeros_like(h_ref)

    row = lax.broadcasted_iota(jnp.int32, (PAIR, PAIR), 0)
    col = lax.broadcasted_iota(jnp.int32, (PAIR, PAIR), 1)
    same = (row >> 6) == (col >> 6)
    tr_ = row & (CHUNK - 1)
    tc_ = col & (CHUNK - 1)
    strict = jnp.logical_and(same, tc_ < tr_)
    incl = jnp.logical_and(same, tc_ <= tr_)
    eye = row == col
    m0 = lax.broadcasted_iota(jnp.int32, (CHUNK, PAIR), 1) < HEAD
    tri = (lax.broadcasted_iota(jnp.int32, (CHUNK, CHUNK), 1)
           <= lax.broadcasted_iota(jnp.int32, (CHUNK, CHUNK), 0)).astype(BF16)
    consts = (m0, strict, incl, eye, tri)
    ones = _head_ones(2 * PAIR)
    slab = 4 * PAIR

    def prep_stages(cn):
        at_start = isinstance(cn, int)
        rows = pl.ds(0, CHUNK) if at_start else pl.ds(pl.multiple_of(cn * CHUNK, CHUNK), CHUNK)

        def shift(x_ref, q_ref, m_ref, ln):
            x = x_ref[rows, ln].astype(F32)
            if at_start:
                last = jnp.where(first, 0.0, q_ref[BF16_ROWS - 1:BF16_ROWS, ln].astype(F32))
            else:
                tail = pl.ds(pl.multiple_of(cn * CHUNK - BF16_ROWS, BF16_ROWS), BF16_ROWS)
                last = x_ref[tail, ln][BF16_ROWS - 1:BF16_ROWS, :].astype(F32)
            prev = pltpu.roll(x, 1, axis=0)
            r8 = lax.broadcasted_iota(jnp.int32, (8, x.shape[1]), 0)
            head = jnp.where(r8 == 0, jnp.broadcast_to(last, (8, x.shape[1])), prev[0:8, :])
            prev = jnp.concatenate([head, prev[8:, :]], axis=0)
            return x + (prev - x) * m_ref[:, ln]

        code = shift(pc_ref, qc_ref, mc_ref, slice(None))
        lane = lax.broadcasted_iota(jnp.int32, code.shape, 1)
        act = jnp.where(lane < DECAY_LORA, jnp.tanh(code),
                        jnp.where(lane < DECAY_LORA + ICLR_LORA, code, jax.nn.sigmoid(code)))
        up = jnp.dot(act.astype(BF16), wl_ref[...], preferred_element_type=F32)
        yield
        for j in range(D_A // slab):
            ln = slice(j * slab, (j + 1) * slab)
            r_s[rows, ln] = shift(pr_ref, qr_ref, mr_ref, ln)
            v_s[rows, ln] = shift(pv_ref, qv_ref, mv_ref, ln)
            yield
            k = shift(pk_ref, qk_ref, mk_ref, ln)
            kk = k * kk_ref[:, ln]
            kk2 = kk * kk
            ss = jnp.concatenate([_dot(kk2[:, i * 2 * PAIR:(i + 1) * 2 * PAIR], ones)
                                  for i in range(slab // (2 * PAIR))], axis=1)
            yield
            a = jax.nn.sigmoid(a0_ref[:, ln] + up[:, D_A + j * slab:D_A + (j + 1) * slab])
            kk = kk * lax.rsqrt(jnp.maximum(ss, 1e-24))
            k_s[rows, ln] = k * (1.0 + (a - 1.0) * ka_ref[:, ln])
            al_s[rows, ln] = -kk
            be_s[rows, ln] = kk * a
            lw_s[rows, ln] = -math.exp(-0.5) * jax.nn.sigmoid(
                w0_ref[:, ln] + up[:, j * slab:(j + 1) * slab])
            g_s[rows, ln] = up[:, 2 * D_A + j * slab:2 * D_A + (j + 1) * slab]
            yield

    for _ in prep_stages(0):
        pass

    def chunk_body(ci, carry):
        rows = pl.ds(pl.multiple_of(ci * CHUNK, CHUNK), CHUNK)
        lanes = [slice(p * PAIR, (p + 1) * PAIR) for p in range(NPAIR)]
        ins = [tuple(s[rows, ln] for s in (r_s, k_s, v_s, al_s, be_s, lw_s)) for ln in lanes]
        hts = [h_ref[p] for p in range(NPAIR)]
        gen = prep_stages(jnp.minimum(ci + 1, nchunk - 1))
        ys, hns = _scan_chunk(ins, hts, consts, lambda: next(gen, None))
        for _ in gen:
            pass
        for p in range(NPAIR):
            h_ref[p] = hns[p]
            y_ref[rows, lanes[p]] = ys[p]
        return carry

    lax.fori_loop(0, nchunk, chunk_body, 0)

    for q in range(NPAIR // 2):
        ln = slice(q * 2 * PAIR, (q + 1) * 2 * PAIR)
        y = y_ref[:, ln]
        mean = _dot(y, ones) * (1.0 / HEAD)
        yc = y - mean
        var = _dot(yc * yc, ones) * (1.0 / HEAD)
        yn = yc * lax.rsqrt(var + LNX_EPS) * lg_ref[:, ln] + lb_ref[:, ln]
        rk = _dot(r_s[:, ln] * k_s[:, ln] * rk_ref[:, ln], ones)
        o_ref[:, ln] = ((yn + rk * v_s[:, ln]) * g_s[:, ln]).astype(o_ref.dtype)


def _rwkv(p, mu, w0, a0, k_k, k_a, w_lora, r_k, lnx_g, lnx_b, bsz, seq, tb=256):
    t = p.shape[0]
    per = seq // tb
    nchunk = tb // CHUNK
    assert nchunk >= 2
    cblk = OFF_W // LORA_COLS
    tail = tb // BF16_ROWS

    def cur(j):
        return pl.BlockSpec((tb, D_A), lambda b, s: (b * per + s, j))

    def prev(j):
        return pl.BlockSpec((BF16_ROWS, D_A),
                            lambda b, s: (jnp.maximum((b * per + s) * tail - 1, 0), j))

    def vec(j):
        return pl.BlockSpec((1, D_A), lambda b, s: (0, j))

    in_specs = [cur(0), cur(1), cur(2),
                pl.BlockSpec((tb, LORA_COLS), lambda b, s: (b * per + s, cblk)),
                prev(0), prev(1), prev(2),
                pl.BlockSpec((BF16_ROWS, LORA_COLS),
                             lambda b, s: (jnp.maximum((b * per + s) * tail - 1, 0), cblk)),
                vec(0), vec(1), vec(2),
                pl.BlockSpec((1, LORA_COLS), lambda b, s: (0, cblk)),
                vec(0), vec(0), vec(0), vec(0),
                pl.BlockSpec((LORA_COLS, 3 * D_A), lambda b, s: (0, 0)),
                vec(0), vec(0), vec(0)]
    blk = (tb, D_A)
    return pl.pallas_call(
        functools.partial(_rwkv_kernel, nchunk=nchunk),
        grid=(bsz, per),
        in_specs=in_specs,
        out_specs=pl.BlockSpec(blk, lambda b, s: (b * per + s, 0)),
        out_shape=jax.ShapeDtypeStruct((t, D_A), BF16),
        scratch_shapes=[pltpu.VMEM((NPAIR, PAIR, PAIR), F32)] + [pltpu.VMEM(blk, F32)] * 8,
        compiler_params=_cparams(("parallel", "arbitrary"), 52),
        name="rwkv",
    )(p, p, p, p, p, p, p, p, mu, mu, mu, mu, w0, a0, k_k, k_a, w_lora, r_k, lnx_g, lnx_b)


def _bucket_table():
    qi = np.arange(BLOCK)[:, None]
    kj = np.arange(2 * BLOCK)[None, :]
    dist = qi + BLOCK - kj
    n = np.maximum(dist, 0)
    nf = np.maximum(n, 1).astype(np.float32)
    large = RPB_MAX_EXACT + (np.log(nf / np.float32(RPB_MAX_EXACT))
                             / np.float32(math.log(RPB_MAX_DIST / RPB_MAX_EXACT))
                             * np.float32(RPB_BUCKETS - RPB_MAX_EXACT)).astype(np.int32)
    large = np.minimum(large, RPB_BUCKETS - 1)
    bucket = np.where(n < RPB_MAX_EXACT, n, large)
    valid = (dist >= 0) & (dist < WINDOW)
    return np.where(valid, bucket, -1).astype(np.int32)


def _bias_kernel(tab_ref, bkt_ref, o_ref):
    h0 = pl.program_id(0) * GQA
    bkt = bkt_ref[...]
    prev_block = lax.broadcasted_iota(jnp.int32, bkt.shape, 1) < BLOCK
    for i in range(GQA):
        acc = jnp.full(bkt.shape, NEG, F32)
        for b in range(RPB_BUCKETS):
            acc = jnp.where(bkt == b, tab_ref[b, h0 + i], acc)
        o_ref[0, i] = acc
        o_ref[1, i] = jnp.where(prev_block, NEG, acc)


def _attn_bias(rpb_table):
    bkt = jnp.asarray(_bucket_table())
    return pl.pallas_call(
        _bias_kernel,
        grid=(H_KV,),
        in_specs=[pl.BlockSpec(memory_space=pltpu.SMEM),
                  pl.BlockSpec((BLOCK, 2 * BLOCK), lambda g: (0, 0))],
        out_specs=pl.BlockSpec((2, GQA, BLOCK, 2 * BLOCK), lambda g: (0, g, 0, 0)),
        out_shape=jax.ShapeDtypeStruct((2, H_Q, BLOCK, 2 * BLOCK), F32),
        compiler_params=_cparams(("arbitrary",), 16),
        name="attn_bias",
    )(rpb_table, bkt)


def _swa_kernel(sink_ref, *refs):
    *q_refs, kc_ref, kp_ref, vc_ref, vp_ref, bias_ref, o_ref = refs
    each = lambda f, *ls: [f(*a) for a in zip(*ls)]
    lo = lax.broadcasted_iota(jnp.int32, (BLOCK, PAIR), 1) < HEAD
    scale = HEAD ** -0.5
    zeros = jnp.zeros((2 * BLOCK, HEAD), BF16)
    ones = jnp.ones((2 * BLOCK, HEAD), BF16)
    npr = GQA // 2
    kcat = jnp.concatenate([kp_ref[...], kc_ref[...]], axis=0).astype(F32) * scale
    vcat = jnp.concatenate([vp_ref[...], vc_ref[...]], axis=0).astype(F32)
    for g in range(H_KV):
        gsl = slice(g * HEAD, (g + 1) * HEAD)
        kg = kcat[:, gsl].astype(BF16)
        vg = vcat[:, gsl].astype(BF16)
        kdup = jnp.concatenate([kg, kg], axis=1)
        rhs = jnp.concatenate([jnp.concatenate([vg, zeros, ones, zeros], axis=1),
                               jnp.concatenate([zeros, vg, zeros, ones], axis=1)], axis=0)
        heads = [g * GQA + 2 * i for i in range(npr)]
        lanes = [slice(h * HEAD, (h + 2) * HEAD) for h in heads]
        qp = [q_refs[h // 4][:, (h % 4) * HEAD:(h % 4 + 2) * HEAD] for h in heads]
        s2 = each(lambda q_: _dot_nt(jnp.concatenate([jnp.where(lo, q_, 0.0).astype(BF16),
                                                      jnp.where(lo, 0.0, q_).astype(BF16)], axis=0),
                                     kdup), qp)
        s = [(s_[0:BLOCK] + bias_ref[h], s_[BLOCK:] + bias_ref[h + 1]) for s_, h in zip(s2, heads)]
        m = [(jnp.maximum(jnp.max(a, axis=-1, keepdims=True), sink_ref[0, h]),
              jnp.maximum(jnp.max(b, axis=-1, keepdims=True), sink_ref[0, h + 1]))
             for (a, b), h in zip(s, heads)]
        e = each(lambda s_, m_: jnp.concatenate([jnp.exp(s_[0] - m_[0]), jnp.exp(s_[1] - m_[1])],
                                                axis=1).astype(BF16), s, m)
        od = each(lambda e_: jnp.dot(e_, rhs, preferred_element_type=F32), e)
        for o_, m_, h, ln in zip(od, m, heads, lanes):
            den = o_[:, PAIR:] + jnp.where(lo, jnp.exp(sink_ref[0, h] - m_[0]),
                                           jnp.exp(sink_ref[0, h + 1] - m_[1]))
            o_ref[:, ln] = (o_[:, 0:PAIR] / den).astype(o_ref.dtype)


def _swa(p, bias, sinks, bsz, seq):
    t = p.shape[0]
    nb = seq // BLOCK
    kvw = H_KV * HEAD
    nq = D_B // kvw

    def cur(c):
        return pl.BlockSpec((BLOCK, kvw), lambda b, n: (b * nb + n, c))

    def prev(c):
        return pl.BlockSpec((BLOCK, kvw), lambda b, n: (b * nb + jnp.maximum(n - 1, 0), c))

    kb, vb = OFF_KB // kvw, OFF_VB // kvw
    return pl.pallas_call(
        _swa_kernel,
        grid=(bsz, nb),
        in_specs=[pl.BlockSpec(memory_space=pltpu.SMEM)]
        + [cur(OFF_Q // kvw + i) for i in range(nq)]
        + [cur(kb), prev(kb), cur(vb), prev(vb),
           pl.BlockSpec((None, H_Q, BLOCK, 2 * BLOCK),
                        lambda b, n: (jnp.where(n == 0, 1, 0), 0, 0, 0))],
        out_specs=pl.BlockSpec((BLOCK, D_B), lambda b, n: (b * nb + n, 0)),
        out_shape=jax.ShapeDtypeStruct((t, D_B), BF16),
        compiler_params=_cparams(("parallel", "arbitrary"), 32),
        name="swa",
    )(sinks, *([p] * (nq + 4)), bias)


def _post_mix_kernel(x_ref, mu_ref, rstd_ref, mix_ref, ge_ref, be_ref, g1_ref, b1_ref, mod_ref,
                     x1_ref, u_ref):
    wide = lambda s_: jnp.concatenate([s_] * (x_ref.shape[1] // LANES), axis=1)
    axn = ((x_ref[...] - wide(mu_ref[...])) * wide(rstd_ref[...]) * (ALPHA * ge_ref[...])
           + ALPHA * be_ref[...])
    z = axn + (1.0 + mod_ref[2:3, :]) * mix_ref[...].astype(F32)
    x1 = _layer_norm(z, g1_ref[...], b1_ref[...])
    x1_ref[...] = x1
    u_ref[...] = (x1 * (1.0 + mod_ref[4:5, :]) + mod_ref[3:4, :]).astype(u_ref.dtype)


def _post_mix(x2, mu, rstd, mix, ge, be, g1, b1, mod, seq, tr=256):
    t, d = x2.shape
    per = seq // tr
    row = pl.BlockSpec((tr, d), lambda i: (i, 0))
    stat = pl.BlockSpec((tr, LANES), lambda i: (i, 0))
    vec = pl.BlockSpec((1, d), lambda i: (0, 0))
    return pl.pallas_call(
        _post_mix_kernel,
        grid=(t // tr,),
        in_specs=[row, stat, stat, row, vec, vec, vec, vec,
                  pl.BlockSpec((None, 6, d), lambda i: (i // per, 0, 0))],
        out_specs=[row, row],
        out_shape=[jax.ShapeDtypeStruct((t, d), F32), jax.ShapeDtypeStruct((t, d), BF16)],
        compiler_params=_cparams(("parallel",), 52),
        name="post_mix",
    )(x2, mu, rstd, mix, ge, be, g1, b1, mod)


def _final_kernel(x1_ref, h_ref, g2_ref, b2_ref, mod_ref, o_ref):
    z = ALPHA * x1_ref[...] + (1.0 + mod_ref[5:6, :]) * h_ref[...].astype(F32)
    o_ref[...] = _layer_norm(z, g2_ref[...], b2_ref[...])


def _final(x1, h, g2, b2, mod, seq, tr=256):
    t, d = x1.shape
    per = seq // tr
    row = pl.BlockSpec((tr, d), lambda i: (i, 0))
    vec = pl.BlockSpec((1, d), lambda i: (0, 0))
    return pl.pallas_call(
        _final_kernel,
        grid=(t // tr,),
        in_specs=[row, row, vec, vec, pl.BlockSpec((None, 6, d), lambda i: (i // per, 0, 0))],
        out_specs=row,
        out_shape=jax.ShapeDtypeStruct((t, d), F32),
        compiler_params=_cparams(("parallel",), 48),
        name="final_ln",
    )(x1, h, g2, b2, mod)


def _lora_weights(w_decay_up, w_iclr_up, w_gate_up):
    zd = jnp.zeros((LORA_COLS, D_A), F32)
    wd = zd.at[0:DECAY_LORA].set(w_decay_up)
    wa = zd.at[DECAY_LORA:DECAY_LORA + ICLR_LORA].set(w_iclr_up)
    wg = zd.at[DECAY_LORA + ICLR_LORA:].set(w_gate_up)
    return jnp.concatenate([wd, wa, wg], axis=1).astype(BF16)


def kernel(x, c, ln_emb_g, ln_emb_b, rpb_table, w_mod, b_mod, w_in, mu_shift, w0, w_decay_up, a0,
           w_iclr_up, w_gate_up, k_k, k_a, r_k, lnx_g, lnx_b, attn_sinks, w_out, ln1_g, ln1_b,
           w_up, w_down, ln2_g, ln2_b):
    bsz, seq, d = x.shape
    assert w_mod.shape[0] == DEPTH == 1 and d == D_MODEL and bsz <= 8
    t = bsz * seq
    row = lambda a: a.reshape(1, -1)
    x2 = x.reshape(t, d)
    c8 = jnp.pad(c, ((0, 8 - bsz), (0, 0)))
    bias = _attn_bias(rpb_table)
    mod = _modulation(c8, w_mod[0], row(b_mod[0]))[:bsz].reshape(bsz, 6, d)
    u1, mu, rstd = _ln_mod(x2, row(ln_emb_g), row(ln_emb_b), mod, seq)
    p = _matmul(u1, w_in[0].astype(BF16), tm=1024, tn=1280, out_dtype=BF16, name="in_proj")
    w_lora = _lora_weights(w_decay_up[0], w_iclr_up[0], w_gate_up[0])
    y_a = _rwkv(p, row(mu_shift[0]), row(w0[0]), row(a0[0]), row(k_k[0]), row(k_a[0]), w_lora,
                row(r_k[0]), row(lnx_g[0]), row(lnx_b[0]), bsz, seq)
    y_b = _swa(p, bias, row(attn_sinks[0]), bsz, seq)
    mix = _matmul([y_a, y_b], w_out[0], tm=1024, tn=512, out_dtype=BF16, name="out_proj")
    x1, u2 = _post_mix(x2, mu, rstd, mix, row(ln_emb_g), row(ln_emb_b), row(ln1_g[0]),
                       row(ln1_b[0]), mod, seq)
    hmid, wd = _matmul(u2, w_up[0], tm=1024, tn=512, out_dtype=BF16, relu2=True,
                       convert=w_down[0], name="mlp_up")
    hout = _matmul(hmid, wd, tm=1024, tn=1024, tk=4096, out_dtype=BF16, name="mlp_down")
    out = _final(x1, hout, row(ln2_g[0]), row(ln2_b[0]), mod, seq)
    return out.reshape(bsz, seq, d)
```

```python
import functools
import math

import numpy as np
import jax
import jax.numpy as jnp
from jax import lax
from jax.experimental import pallas as pl
from jax.experimental.pallas import tpu as pltpu

F32 = jnp.float32
BF16 = jnp.bfloat16

D_MODEL = 4096
HEAD = 64
D_A = D_MODEL // 2
D_B = D_MODEL - D_A
H_A = D_A // HEAD
H_Q = D_B // HEAD
GQA = 8
H_KV = H_Q // GQA
WINDOW = 128
BLOCK = 128
RPB_BUCKETS = 32
RPB_MAX_EXACT = RPB_BUCKETS // 2
RPB_MAX_DIST = 128
DECAY_LORA = max(32, int(round(D_A ** 0.5 * 1.8 / 32)) * 32)
ICLR_LORA = max(32, int(round(D_A ** 0.5 * 1.8 / 32)) * 32)
GATE_LORA = max(32, int(round(D_A ** 0.6 * 0.8 / 32)) * 32)
LORA_COLS = DECAY_LORA + ICLR_LORA + GATE_LORA
D_FF = 4 * D_MODEL
DEPTH = 1
ALPHA = (2.0 * DEPTH) ** 0.25
LN_EPS = 1e-5
LNX_EPS = 64e-5
OFF_W = 3 * D_A
RWKV_COLS = OFF_W + LORA_COLS
OFF_Q = RWKV_COLS
OFF_KB = OFF_Q + D_B
OFF_VB = OFF_KB + H_KV * HEAD
N_IN = OFF_VB + H_KV * HEAD
NEG = -1e30

CHUNK = 64
PAIR = 2 * HEAD
NPAIR = H_A // 2
BF16_ROWS = 16
LANES = 128
VMEM_CAP = 56 * 1024 * 1024


def _cparams(sem, vmem_mb):
    return pltpu.CompilerParams(dimension_semantics=sem,
                                vmem_limit_bytes=min(int(vmem_mb * 1024 * 1024), VMEM_CAP))


def _dot(a, b):
    return jnp.dot(a.astype(BF16), b.astype(BF16), preferred_element_type=F32)


def _dot_nt(a, b):
    return lax.dot_general(a.astype(BF16), b.astype(BF16), (((1,), (1,)), ((), ())),
                           preferred_element_type=F32)


def _split2(x):
    hi = x.astype(BF16)
    lo = (x - hi.astype(F32)).astype(BF16)
    return hi, lo


def _dot3(a, b):
    ah, al = _split2(a)
    bh, bl = _split2(b)
    return (jnp.dot(ah, bh, preferred_element_type=F32)
            + jnp.dot(ah, bl, preferred_element_type=F32)
            + jnp.dot(al, bh, preferred_element_type=F32))


def _layer_norm(x, g, b):
    mu = jnp.mean(x, axis=-1, keepdims=True)
    xc = x - mu
    var = jnp.mean(xc * xc, axis=-1, keepdims=True)
    return xc * lax.rsqrt(var + LN_EPS) * g + b


def _ln_rows(src_ref, consume):
    groups = [slice(g * BF16_ROWS, (g + 1) * BF16_ROWS) for g in range(src_ref.shape[0] // BF16_ROWS)]
    mus = [jnp.mean(src_ref[rs, :], axis=-1, keepdims=True) for rs in groups]
    rstd = [lax.rsqrt(jnp.mean(jnp.square(src_ref[rs, :] - mu), axis=-1, keepdims=True) + LN_EPS)
            for rs, mu in zip(groups, mus)]
    for rs, mu, r in zip(groups, mus, rstd):
        consume(rs, (src_ref[rs, :] - mu) * r, mu, r)


def _head_ones(n):
    r = lax.broadcasted_iota(jnp.int32, (n, n), 0)
    c = lax.broadcasted_iota(jnp.int32, (n, n), 1)
    return ((r >> 6) == (c >> 6)).astype(BF16)


def _mod_kernel(c_ref, w_ref, b_ref, o_ref):
    c = c_ref[...]
    cond = c * jax.nn.sigmoid(c)
    o_ref[...] = _dot3(cond, w_ref[...]) + b_ref[...]


def _modulation(c8, w_mod, b_mod, tn=512):
    d, n = w_mod.shape
    return pl.pallas_call(
        _mod_kernel,
        grid=(n // tn,),
        in_specs=[pl.BlockSpec((8, d), lambda j: (0, 0)),
                  pl.BlockSpec((d, tn), lambda j: (0, j)),
                  pl.BlockSpec((1, tn), lambda j: (0, j))],
        out_specs=pl.BlockSpec((8, tn), lambda j: (0, j)),
        out_shape=jax.ShapeDtypeStruct((8, n), F32),
        compiler_params=_cparams(("parallel",), 40),
        name="modulation",
    )(c8, w_mod, b_mod)


def _ln_mod_kernel(x_ref, g_ref, b_ref, mod_ref, u_ref, mu_ref, rstd_ref):
    gain = 1.0 + mod_ref[1:2, :]
    scale = g_ref[...] * gain
    shift = b_ref[...] * gain + mod_ref[0:1, :]

    def consume(rs, xh, mu, rstd):
        u_ref[rs, :] = (xh * scale + shift).astype(u_ref.dtype)
        mu_ref[rs, :] = jnp.broadcast_to(mu, (BF16_ROWS, LANES))
        rstd_ref[rs, :] = jnp.broadcast_to(rstd, (BF16_ROWS, LANES))

    _ln_rows(x_ref, consume)


def _ln_mod(x2, g, b, mod, seq, tr=256):
    t, d = x2.shape
    per = seq // tr
    stat = pl.BlockSpec((tr, LANES), lambda i: (i, 0))
    return pl.pallas_call(
        _ln_mod_kernel,
        grid=(t // tr,),
        in_specs=[pl.BlockSpec((tr, d), lambda i: (i, 0)),
                  pl.BlockSpec((1, d), lambda i: (0, 0)),
                  pl.BlockSpec((1, d), lambda i: (0, 0)),
                  pl.BlockSpec((None, 6, d), lambda i: (i // per, 0, 0))],
        out_specs=[pl.BlockSpec((tr, d), lambda i: (i, 0)), stat, stat],
        out_shape=[jax.ShapeDtypeStruct((t, d), BF16), jax.ShapeDtypeStruct((t, LANES), F32),
                   jax.ShapeDtypeStruct((t, LANES), F32)],
        compiler_params=_cparams(("parallel",), 32),
        name="ln_mod",
    )(x2, g, b, mod)


def _mm_kernel(*refs, relu2, convert):
    if convert:
        *refs, ci_ref, o_ref, co_ref = refs
        co_ref[...] = ci_ref[...].astype(co_ref.dtype)
        refs = (*refs, o_ref)
    *a_refs, b_ref, o_ref = refs
    acc, off = None, 0
    for a_ref in a_refs:
        kd = a_ref.shape[1]
        part = jnp.dot(a_ref[...], b_ref[off:off + kd, :].astype(BF16), preferred_element_type=F32)
        acc = part if acc is None else acc + part
        off += kd
    if relu2:
        acc = jnp.square(jnp.maximum(acc, 0.0))
    o_ref[...] = acc.astype(o_ref.dtype)


def _mm_acc_kernel(a_ref, b_ref, o_ref, acc_ref):
    k = pl.program_id(2)

    @pl.when(k == 0)
    def _():
        acc_ref[...] = jnp.zeros_like(acc_ref)

    acc_ref[...] += jnp.dot(a_ref[...], b_ref[...].astype(BF16), preferred_element_type=F32)

    @pl.when(k == pl.num_programs(2) - 1)
    def _():
        o_ref[...] = acc_ref[...].astype(o_ref.dtype)


def _matmul(a, b, *, tm, tn, tk=None, out_dtype=F32, relu2=False, convert=None, name="matmul"):
    a = a if isinstance(a, (list, tuple)) else [a]
    m = a[0].shape[0]
    kd, n = b.shape
    tm = min(tm, m)
    nj = n // tn
    osz = jnp.dtype(out_dtype).itemsize
    bsz = jnp.dtype(b.dtype).itemsize
    bsz = 2 * bsz + (2 if bsz == 4 else 0)
    if tk is None or tk >= kd:
        vm = (2 * tm * kd * 2 + kd * tn * bsz + 2 * tm * tn * osz + tm * tn * 4) / 2 ** 20 + 8
        in_specs = ([pl.BlockSpec((tm, x.shape[1]), lambda i, j: (i, 0)) for x in a]
                    + [pl.BlockSpec((kd, tn), lambda i, j: (0, j))])
        out_specs = pl.BlockSpec((tm, tn), lambda i, j: (i, j))
        out_shape = jax.ShapeDtypeStruct((m, n), out_dtype)
        args = (*a, b)
        if convert is not None:
            cr, cc = convert.shape
            slab = cr // ((m // tm) * nj)
            assert slab * (m // tm) * nj == cr and slab % BF16_ROWS == 0
            cspec = pl.BlockSpec((slab, cc), lambda i, j: (i * nj + j, 0))
            in_specs, args = in_specs + [cspec], (*args, convert)
            out_specs = [out_specs, cspec]
            out_shape = [out_shape, jax.ShapeDtypeStruct((cr, cc), BF16)]
            vm += slab * cc * 12 / 2 ** 20
        return pl.pallas_call(
            functools.partial(_mm_kernel, relu2=relu2, convert=convert is not None),
            grid=(m // tm, nj),
            in_specs=in_specs,
            out_specs=out_specs,
            out_shape=out_shape,
            compiler_params=_cparams(("parallel", "parallel"), vm),
            name=name,
        )(*args)
    assert not relu2 and len(a) == 1 and convert is None
    vm = (2 * tm * tk * 2 + tk * tn * bsz + 2 * tm * tn * osz + 2 * tm * tn * 4) / 2 ** 20 + 8
    return pl.pallas_call(
        _mm_acc_kernel,
        grid=(m // tm, nj, kd // tk),
        in_specs=[pl.BlockSpec((tm, tk), lambda i, j, k: (i, k)),
                  pl.BlockSpec((tk, tn), lambda i, j, k: (k, j))],
        out_specs=pl.BlockSpec((tm, tn), lambda i, j, k: (i, j)),
        out_shape=jax.ShapeDtypeStruct((m, n), out_dtype),
        scratch_shapes=[pltpu.VMEM((tm, tn), F32)],
        compiler_params=_cparams(("parallel", "parallel", "arbitrary"), vm),
        name=name,
    )(a[0], b)


def _scan_chunk(ins, hts, consts, tick):
    m0, strict, incl, eye, tri = consts

    ticking = [False]

    def each(f, *ls):
        out = [f(*a) for a in zip(*ls)]
        if ticking[0]:
            tick()
        return out

    r, k, v, al, be, lw = (list(z) for z in zip(*ins))

    def cumsum(x):
        hi = x.astype(BF16)
        rem = x - hi.astype(F32)
        mid = rem.astype(BF16)
        lo = (rem - mid.astype(F32)).astype(BF16)
        c3 = jnp.dot(tri, jnp.concatenate([hi, mid, lo], axis=1), preferred_element_type=F32)
        return c3[:, 0:PAIR] + c3[:, PAIR:2 * PAIR] + c3[:, 2 * PAIR:3 * PAIR]

    def sm(x):
        return jnp.concatenate([jnp.where(m0, x, 0.0), jnp.where(m0, 0.0, x)], axis=0)

    b16 = lambda t_: t_.astype(BF16)
    nn = lambda a_, b_: jnp.dot(a_, b_, preferred_element_type=F32)
    c = each(cumsum, lw)
    pc = each(lambda c_: jnp.exp(c_[CHUNK - 1:CHUNK, :]), c)
    einv = each(lambda c_: jnp.exp(-c_), c)
    a_sm = each(lambda a_, c_, l_: b16(sm(a_ * jnp.exp(c_ - l_))), al, c, lw)
    r_sm = each(lambda r_, c_: sm(r_ * jnp.exp(c_)), r, c)
    v_sm = each(lambda v_: b16(sm(v_)), v)
    b_t = each(lambda b_, e_: b_ * e_, be, einv)
    k_t = each(lambda k_, e_: k_ * e_, k, einv)

    def scores(a_, r_, b_, k_):
        bb, kb = b16(b_), b16(k_)
        return _dot_nt(jnp.concatenate([a_, b16(r_)], axis=0),
                       jnp.concatenate([bb, bb, kb, kb], axis=0))

    s = each(scores, a_sm, r_sm, b_t, k_t)
    lab = each(lambda s_: jnp.where(strict, s_[0:PAIR, 0:PAIR], 0.0), s)
    mak = each(lambda s_: b16(jnp.where(strict, s_[0:PAIR, PAIR:], 0.0)), s)
    incl2 = jnp.concatenate([incl, incl], axis=1)
    mrbk = each(lambda s_: b16(jnp.where(incl2, s_[PAIR:, :], 0.0)), s)

    ticking[0] = True
    ldt = each(lambda l_: l_.T, lab)
    xt = each(lambda l_: jnp.where(eye, 1.0, l_), ldt)
    lt = each(lambda l_: nn(b16(l_), b16(l_)), ldt)

    def series_step(l_, x_):
        lb = b16(l_)
        return nn(lb, jnp.concatenate([b16(x_), lb], axis=1))

    for _ in range(4):
        xl = each(series_step, lt, xt)
        xt = each(lambda x_, p_: x_ + p_[:, 0:PAIR], xt, xl)
        lt = each(lambda p_: p_[:, PAIR:], xl)
    x = each(lambda x_, l_: b16((x_ + nn(b16(l_), b16(x_))).T), xt, lt)

    makv = each(nn, mak, v_sm)
    wu = each(lambda x_, a_, m_: nn(x_, jnp.concatenate([a_, b16(m_)], axis=1)), x, a_sm, makv)
    bigr = each(lambda wu_, v_: jnp.concatenate(
        [b16(wu_), jnp.concatenate([jnp.zeros_like(v_), v_], axis=1)], axis=0), wu, v_sm)
    bk = each(lambda b_, k_, p_: b16(jnp.concatenate([sm(b_ * p_), sm(k_ * p_)], axis=0).T),
              b_t, k_t, pc)
    gz = each(nn, bk, bigr)
    qy = each(nn, mrbk, bigr)
    hb = each(b16, hts)

    def new_state(ht, h_, g_, p_):
        return ht * p_ + _dot_nt(h_, g_[:, 0:PAIR]) + g_[:, PAIR:].T

    def output(r_, q_, h_):
        y_sm = _dot_nt(r_ + q_[:, 0:PAIR], h_) + q_[:, PAIR:]
        return y_sm[0:CHUNK, :] + y_sm[CHUNK:, :]

    return each(output, r_sm, qy, hb), each(new_state, hts, hb, gz, pc)


def _rwkv_kernel(pr_ref, pk_ref, pv_ref, pc_ref, qr_ref, qk_ref, qv_ref, qc_ref,
                 mr_ref, mk_ref, mv_ref, mc_ref, w0_ref, a0_ref, kk_ref, ka_ref, wl_ref,
                 rk_ref, lg_ref, lb_ref, o_ref,
                 h_ref, y_ref, r_s, k_s, v_s, al_s, be_s, lw_s, g_s, ud_s, *, nchunk):
    first = pl.program_id(1) == 0

    @pl.when(first)
    def _():
        h_ref[...] = jnp.zeros_like(h_ref)

    row = lax.broadcasted_iota(jnp.int32, (PAIR, PAIR), 0)
    col = lax.broadcasted_iota(jnp.int32, (PAIR, PAIR), 1)
    same = (row >> 6) == (col >> 6)
    tr_ = row & (CHUNK - 1)
    tc_ = col & (CHUNK - 1)
    strict = jnp.logical_and(same, tc_ < tr_)
    incl = jnp.logical_and(same, tc_ <= tr_)
    eye = row == col
    m0 = lax.broadcasted_iota(jnp.int32, (CHUNK, PAIR), 1) < HEAD
    tri = (lax.broadcasted_iota(jnp.int32, (CHUNK, CHUNK), 1)
           <= lax.broadcasted_iota(jnp.int32, (CHUNK, CHUNK), 0)).astype(BF16)
    consts = (m0, strict, incl, eye, tri)
    ones = _head_ones(2 * PAIR)
    slab = 4 * PAIR

    def prep_stages(cn):
        at_start = isinstance(cn, int)
        rows = pl.ds(0, CHUNK) if at_start else pl.ds(pl.multiple_of(cn * CHUNK, CHUNK), CHUNK)

        def shift(x_ref, q_ref, m_ref, ln):
            x = x_ref[rows, ln].astype(F32)
            if at_start:
                last = jnp.where(first, 0.0, q_ref[BF16_ROWS - 1:BF16_ROWS, ln].astype(F32))
            else:
                tail = pl.ds(pl.multiple_of(cn * CHUNK - BF16_ROWS, BF16_ROWS), BF16_ROWS)
                last = x_ref[tail, ln][BF16_ROWS - 1:BF16_ROWS, :].astype(F32)
            prev = pltpu.roll(x, 1, axis=0)
            r8 = lax.broadcasted_iota(jnp.int32, (8, x.shape[1]), 0)
            head = jnp.where(r8 == 0, jnp.broadcast_to(last, (8, x.shape[1])), prev[0:8, :])
            prev = jnp.concatenate([head, prev[8:, :]], axis=0)
            return x + (prev - x) * m_ref[:, ln]

        for j in range(D_A // slab):
            ln = slice(j * slab, (j + 1) * slab)
            r_s[rows, ln] = shift(pr_ref, qr_ref, mr_ref, ln)
            v_s[rows, ln] = shift(pv_ref, qv_ref, mv_ref, ln)
            yield
            k = shift(pk_ref, qk_ref, mk_ref, ln)
            kk = k * kk_ref[:, ln]
            kk2 = kk * kk
            ss = jnp.concatenate([_dot(kk2[:, i * 2 * PAIR:(i + 1) * 2 * PAIR], ones)
                                  for i in range(slab // (2 * PAIR))], axis=1)
            yield
            a = jax.nn.sigmoid(a0_ref[:, ln] + ud_s[rows, D_A + j * slab:D_A + (j + 1) * slab])
            kk = kk * lax.rsqrt(jnp.maximum(ss, 1e-24))
            k_s[rows, ln] = k * (1.0 + (a - 1.0) * ka_ref[:, ln])
            al_s[rows, ln] = -kk
            be_s[rows, ln] = kk * a
            lw_s[rows, ln] = -math.exp(-0.5) * jax.nn.sigmoid(w0_ref[:, ln] + ud_s[rows, ln])
            yield

    code = pc_ref[...].astype(F32)
    last = jnp.where(first, 0.0, qc_ref[BF16_ROWS - 1:BF16_ROWS, :].astype(F32))
    prev = pltpu.roll(code, 1, axis=0)
    r8 = lax.broadcasted_iota(jnp.int32, (8, LORA_COLS), 0)
    prev = jnp.concatenate([jnp.where(r8 == 0, jnp.broadcast_to(last, (8, LORA_COLS)), prev[0:8, :]),
                            prev[8:, :]], axis=0)
    code = code + (prev - code) * mc_ref[...]
    lane = lax.broadcasted_iota(jnp.int32, code.shape, 1)
    act = jnp.where(lane < DECAY_LORA, jnp.tanh(code),
                    jnp.where(lane < DECAY_LORA + ICLR_LORA, code, jax.nn.sigmoid(code))).astype(BF16)
    ud_s[:, 0:D_A] = jnp.dot(act, wl_ref[:, 0:D_A], preferred_element_type=F32)
    ud_s[:, D_A:] = jnp.dot(act, wl_ref[:, D_A:2 * D_A], preferred_element_type=F32)
    g_s[...] = jnp.dot(act, wl_ref[:, 2 * D_A:], preferred_element_type=F32)

    for _ in prep_stages(0):
        pass

    def chunk_body(ci, carry):
        rows = pl.ds(pl.multiple_of(ci * CHUNK, CHUNK), CHUNK)
        lanes = [slice(p * PAIR, (p + 1) * PAIR) for p in range(NPAIR)]
        ins = [tuple(s[rows, ln] for s in (r_s, k_s, v_s, al_s, be_s, lw_s)) for ln in lanes]
        hts = [h_ref[p] for p in range(NPAIR)]
        gen = prep_stages(jnp.minimum(ci + 1, nchunk - 1))
        ys, hns = _scan_chunk(ins, hts, consts, lambda: next(gen, None))
        for _ in gen:
            pass
        for p in range(NPAIR):
            h_ref[p] = hns[p]
            y_ref[rows, lanes[p]] = ys[p]
        return carry

    lax.fori_loop(0, nchunk, chunk_body, 0)

    tb = y_ref.shape[0]
    width = 2 * PAIR
    for half in range(2):
        lns = [slice(q * width, (q + 1) * width)
               for q in range(half * NPAIR // 4, (half + 1) * NPAIR // 4)]
        stack = lambda f: jnp.concatenate([f(ln) for ln in lns], axis=0)
        y = stack(lambda ln: y_ref[:, ln])
        sums = _dot(jnp.concatenate(
            [y, stack(lambda ln: r_s[:, ln] * k_s[:, ln] * rk_ref[:, ln])], axis=0), ones)
        yc = y - sums[0:len(lns) * tb] * (1.0 / HEAD)
        rstd = lax.rsqrt(_dot(yc * yc, ones) * (1.0 / HEAD) + LNX_EPS)
        for i, ln in enumerate(lns):
            rs = slice(i * tb, (i + 1) * tb)
            yn = yc[rs] * rstd[rs] * lg_ref[:, ln] + lb_ref[:, ln]
            rk = sums[len(lns) * tb + i * tb:len(lns) * tb + (i + 1) * tb]
            o_ref[:, ln] = ((yn + rk * v_s[:, ln]) * g_s[:, ln]).astype(o_ref.dtype)


def _rwkv(p, mu, w0, a0, k_k, k_a, w_lora, r_k, lnx_g, lnx_b, bsz, seq, tb=256):
    t = p.shape[0]
    per = seq // tb
    nchunk = tb // CHUNK
    assert nchunk >= 2
    cblk = OFF_W // LORA_COLS
    tail = tb // BF16_ROWS

    def cur(j):
        return pl.BlockSpec((tb, D_A), lambda b, s: (b * per + s, j))

    def prev(j):
        return pl.BlockSpec((BF16_ROWS, D_A),
                            lambda b, s: (jnp.maximum((b * per + s) * tail - 1, 0), j))

    def vec(j):
        return pl.BlockSpec((1, D_A), lambda b, s: (0, j))

    in_specs = [cur(0), cur(1), cur(2),
                pl.BlockSpec((tb, LORA_COLS), lambda b, s: (b * per + s, cblk)),
                prev(0), prev(1), prev(2),
                pl.BlockSpec((BF16_ROWS, LORA_COLS),
                             lambda b, s: (jnp.maximum((b * per + s) * tail - 1, 0), cblk)),
                vec(0), vec(1), vec(2),
                pl.BlockSpec((1, LORA_COLS), lambda b, s: (0, cblk)),
                vec(0), vec(0), vec(0), vec(0),
                pl.BlockSpec((LORA_COLS, 3 * D_A), lambda b, s: (0, 0)),
                vec(0), vec(0), vec(0)]
    blk = (tb, D_A)
    return pl.pallas_call(
        functools.partial(_rwkv_kernel, nchunk=nchunk),
        grid=(bsz, per),
        in_specs=in_specs,
        out_specs=pl.BlockSpec(blk, lambda b, s: (b * per + s, 0)),
        out_shape=jax.ShapeDtypeStruct((t, D_A), BF16),
        scratch_shapes=([pltpu.VMEM((NPAIR, PAIR, PAIR), F32)] + [pltpu.VMEM(blk, F32)] * 8
                        + [pltpu.VMEM((tb, 2 * D_A), F32)]),
        compiler_params=_cparams(("parallel", "arbitrary"), 52),
        name="rwkv",
    )(p, p, p, p, p, p, p, p, mu, mu, mu, mu, w0, a0, k_k, k_a, w_lora, r_k, lnx_g, lnx_b)


def _bucket_table():
    qi = np.arange(BLOCK)[:, None]
    kj = np.arange(2 * BLOCK)[None, :]
    dist = qi + BLOCK - kj
    n = np.maximum(dist, 0)
    nf = np.maximum(n, 1).astype(np.float32)
    large = RPB_MAX_EXACT + (np.log(nf / np.float32(RPB_MAX_EXACT))
                             / np.float32(math.log(RPB_MAX_DIST / RPB_MAX_EXACT))
                             * np.float32(RPB_BUCKETS - RPB_MAX_EXACT)).astype(np.int32)
    large = np.minimum(large, RPB_BUCKETS - 1)
    bucket = np.where(n < RPB_MAX_EXACT, n, large)
    valid = (dist >= 0) & (dist < WINDOW)
    return np.where(valid, bucket, -1).astype(np.int32)


def _bias_kernel(tab_ref, bkt_ref, o_ref):
    h0 = pl.program_id(0) * GQA
    bkt = bkt_ref[...]
    prev_block = lax.broadcasted_iota(jnp.int32, bkt.shape, 1) < BLOCK
    for i in range(GQA):
        acc = jnp.full(bkt.shape, NEG, F32)
        for b in range(RPB_BUCKETS):
            acc = jnp.where(bkt == b, tab_ref[b, h0 + i], acc)
        o_ref[0, i] = acc
        o_ref[1, i] = jnp.where(prev_block, NEG, acc)


def _attn_bias(rpb_table):
    bkt = jnp.asarray(_bucket_table())
    return pl.pallas_call(
        _bias_kernel,
        grid=(H_KV,),
        in_specs=[pl.BlockSpec(memory_space=pltpu.SMEM),
                  pl.BlockSpec((BLOCK, 2 * BLOCK), lambda g: (0, 0))],
        out_specs=pl.BlockSpec((2, GQA, BLOCK, 2 * BLOCK), lambda g: (0, g, 0, 0)),
        out_shape=jax.ShapeDtypeStruct((2, H_Q, BLOCK, 2 * BLOCK), F32),
        compiler_params=_cparams(("arbitrary",), 16),
        name="attn_bias",
    )(rpb_table, bkt)


def _swa_kernel(sink_ref, *refs):
    *q_refs, kc_ref, kp_ref, vc_ref, vp_ref, bias_ref, o_ref = refs
    each = lambda f, *ls: [f(*a) for a in zip(*ls)]
    lo = lax.broadcasted_iota(jnp.int32, (BLOCK, PAIR), 1) < HEAD
    scale = HEAD ** -0.5
    zeros = jnp.zeros((2 * BLOCK, HEAD), BF16)
    ones = jnp.ones((2 * BLOCK, HEAD), BF16)
    npr = GQA // 2
    kcat = jnp.concatenate([kp_ref[...], kc_ref[...]], axis=0).astype(F32) * scale
    vcat = jnp.concatenate([vp_ref[...], vc_ref[...]], axis=0).astype(F32)
    for g in range(H_KV):
        gsl = slice(g * HEAD, (g + 1) * HEAD)
        kg = kcat[:, gsl].astype(BF16)
        vg = vcat[:, gsl].astype(BF16)
        kdup = jnp.concatenate([kg, kg], axis=1)
        rhs = jnp.concatenate([jnp.concatenate([vg, zeros, ones, zeros], axis=1),
                               jnp.concatenate([zeros, vg, zeros, ones], axis=1)], axis=0)
        heads = [g * GQA + 2 * i for i in range(npr)]
        lanes = [slice(h * HEAD, (h + 2) * HEAD) for h in heads]
        qp = [q_refs[h // 4][:, (h % 4) * HEAD:(h % 4 + 2) * HEAD] for h in heads]
        s2 = each(lambda q_: _dot_nt(jnp.concatenate([jnp.where(lo, q_, 0.0).astype(BF16),
                                                      jnp.where(lo, 0.0, q_).astype(BF16)], axis=0),
                                     kdup), qp)
        s = [(s_[0:BLOCK] + bias_ref[h], s_[BLOCK:] + bias_ref[h + 1]) for s_, h in zip(s2, heads)]
        m = [(jnp.maximum(jnp.max(a, axis=-1, keepdims=True), sink_ref[0, h]),
              jnp.maximum(jnp.max(b, axis=-1, keepdims=True), sink_ref[0, h + 1]))
             for (a, b), h in zip(s, heads)]
        e = each(lambda s_, m_: jnp.concatenate([jnp.exp(s_[0] - m_[0]), jnp.exp(s_[1] - m_[1])],
                                                axis=1).astype(BF16), s, m)
        od = each(lambda e_: jnp.dot(e_, rhs, preferred_element_type=F32), e)
        for o_, m_, h, ln in zip(od, m, heads, lanes):
            den = o_[:, PAIR:] + jnp.where(lo, jnp.exp(sink_ref[0, h] - m_[0]),
                                           jnp.exp(sink_ref[0, h + 1] - m_[1]))
            o_ref[:, ln] = (o_[:, 0:PAIR] / den).astype(o_ref.dtype)


def _swa(p, bias, sinks, bsz, seq):
    t = p.shape[0]
    nb = seq // BLOCK
    kvw = H_KV * HEAD
    nq = D_B // kvw

    def cur(c):
        return pl.BlockSpec((BLOCK, kvw), lambda b, n: (b * nb + n, c))

    def prev(c):
        return pl.BlockSpec((BLOCK, kvw), lambda b, n: (b * nb + jnp.maximum(n - 1, 0), c))

    kb, vb = OFF_KB // kvw, OFF_VB // kvw
    return pl.pallas_call(
        _swa_kernel,
        grid=(bsz, nb),
        in_specs=[pl.BlockSpec(memory_space=pltpu.SMEM)]
        + [cur(OFF_Q // kvw + i) for i in range(nq)]
        + [cur(kb), prev(kb), cur(vb), prev(vb),
           pl.BlockSpec((None, H_Q, BLOCK, 2 * BLOCK),
                        lambda b, n: (jnp.where(n == 0, 1, 0), 0, 0, 0))],
        out_specs=pl.BlockSpec((BLOCK, D_B), lambda b, n: (b * nb + n, 0)),
        out_shape=jax.ShapeDtypeStruct((t, D_B), BF16),
        compiler_params=_cparams(("parallel", "arbitrary"), 32),
        name="swa",
    )(sinks, *([p] * (nq + 4)), bias)


def _post_mix_kernel(x_ref, mu_ref, rstd_ref, mix_ref, ge_ref, be_ref, g1_ref, b1_ref, mod_ref,
                     x1_ref, u_ref):
    wide = lambda s_: jnp.concatenate([s_] * (x_ref.shape[1] // LANES), axis=1)
    axn = ((x_ref[...] - wide(mu_ref[...])) * wide(rstd_ref[...]) * (ALPHA * ge_ref[...])
           + ALPHA * be_ref[...])
    z = axn + (1.0 + mod_ref[2:3, :]) * mix_ref[...].astype(F32)
    x1 = _layer_norm(z, g1_ref[...], b1_ref[...])
    x1_ref[...] = x1
    u_ref[...] = (x1 * (1.0 + mod_ref[4:5, :]) + mod_ref[3:4, :]).astype(u_ref.dtype)


def _post_mix(x2, mu, rstd, mix, ge, be, g1, b1, mod, seq, tr=256):
    t, d = x2.shape
    per = seq // tr
    row = pl.BlockSpec((tr, d), lambda i: (i, 0))
    stat = pl.BlockSpec((tr, LANES), lambda i: (i, 0))
    vec = pl.BlockSpec((1, d), lambda i: (0, 0))
    return pl.pallas_call(
        _post_mix_kernel,
        grid=(t // tr,),
        in_specs=[row, stat, stat, row, vec, vec, vec, vec,
                  pl.BlockSpec((None, 6, d), lambda i: (i // per, 0, 0))],
        out_specs=[row, row],
        out_shape=[jax.ShapeDtypeStruct((t, d), F32), jax.ShapeDtypeStruct((t, d), BF16)],
        compiler_params=_cparams(("parallel",), 52),
        name="post_mix",
    )(x2, mu, rstd, mix, ge, be, g1, b1, mod)


def _final_kernel(x1_ref, h_ref, g2_ref, b2_ref, mod_ref, o_ref):
    z = ALPHA * x1_ref[...] + (1.0 + mod_ref[5:6, :]) * h_ref[...].astype(F32)
    o_ref[...] = _layer_norm(z, g2_ref[...], b2_ref[...])


def _final(x1, h, g2, b2, mod, seq, tr=256):
    t, d = x1.shape
    per = seq // tr
    row = pl.BlockSpec((tr, d), lambda i: (i, 0))
    vec = pl.BlockSpec((1, d), lambda i: (0, 0))
    return pl.pallas_call(
        _final_kernel,
        grid=(t // tr,),
        in_specs=[row, row, vec, vec, pl.BlockSpec((None, 6, d), lambda i: (i // per, 0, 0))],
        out_specs=row,
        out_shape=jax.ShapeDtypeStruct((t, d), F32),
        compiler_params=_cparams(("parallel",), 48),
        name="final_ln",
    )(x1, h, g2, b2, mod)


def _lora_weights(w_decay_up, w_iclr_up, w_gate_up):
    zd = jnp.zeros((LORA_COLS, D_A), F32)
    wd = zd.at[0:DECAY_LORA].set(w_decay_up)
    wa = zd.at[DECAY_LORA:DECAY_LORA + ICLR_LORA].set(w_iclr_up)
    wg = zd.at[DECAY_LORA + ICLR_LORA:].set(w_gate_up)
    return jnp.concatenate([wd, wa, wg], axis=1).astype(BF16)


def kernel(x, c, ln_emb_g, ln_emb_b, rpb_table, w_mod, b_mod, w_in, mu_shift, w0, w_decay_up, a0,
           w_iclr_up, w_gate_up, k_k, k_a, r_k, lnx_g, lnx_b, attn_sinks, w_out, ln1_g, ln1_b,
           w_up, w_down, ln2_g, ln2_b):
    bsz, seq, d = x.shape
    assert w_mod.shape[0] == DEPTH == 1 and d == D_MODEL and bsz <= 8
    t = bsz * seq
    row = lambda a: a.reshape(1, -1)
    x2 = x.reshape(t, d)
    c8 = jnp.pad(c, ((0, 8 - bsz), (0, 0)))
    bias = _attn_bias(rpb_table)
    mod = _modulation(c8, w_mod[0], row(b_mod[0]))[:bsz].reshape(bsz, 6, d)
    u1, mu, rstd = _ln_mod(x2, row(ln_emb_g), row(ln_emb_b), mod, seq)
    p = _matmul(u1, w_in[0].astype(BF16), tm=1024, tn=1280, out_dtype=BF16, name="in_proj")
    w_lora = _lora_weights(w_decay_up[0], w_iclr_up[0], w_gate_up[0])
    y_a = _rwkv(p, row(mu_shift[0]), row(w0[0]), row(a0[0]), row(k_k[0]), row(k_a[0]), w_lora,
                row(r_k[0]), row(lnx_g[0]), row(lnx_b[0]), bsz, seq)
    y_b = _swa(p, bias, row(attn_sinks[0]), bsz, seq)
    mix = _matmul([y_a, y_b], w_out[0], tm=1024, tn=512, out_dtype=BF16, name="out_proj")
    x1, u2 = _post_mix(x2, mu, rstd, mix, row(ln_emb_g), row(ln_emb_b), row(ln1_g[0]),
                       row(ln1_b[0]), mod, seq)
    hmid, wd = _matmul(u2, w_up[0], tm=1024, tn=512, out_dtype=BF16, relu2=True,
                       convert=w_down[0], name="mlp_up")
    hout = _matmul(hmid, wd, tm=1024, tn=1024, tk=4096, out_dtype=BF16, name="mlp_down")
    out = _final(x1, hout, row(ln2_g[0]), row(ln2_b[0]), mod, seq)
    return out.reshape(bsz, seq, d)
```

```python
import functools
import math

import numpy as np
import jax
import jax.numpy as jnp
from jax import lax
from jax.experimental import pallas as pl
from jax.experimental.pallas import tpu as pltpu

F32 = jnp.float32
BF16 = jnp.bfloat16

D_MODEL = 4096
HEAD = 64
D_A = D_MODEL // 2
D_B = D_MODEL - D_A
H_A = D_A // HEAD
H_Q = D_B // HEAD
GQA = 8
H_KV = H_Q // GQA
WINDOW = 128
BLOCK = 128
RPB_BUCKETS = 32
RPB_MAX_EXACT = RPB_BUCKETS // 2
RPB_MAX_DIST = 128
DECAY_LORA = max(32, int(round(D_A ** 0.5 * 1.8 / 32)) * 32)
ICLR_LORA = max(32, int(round(D_A ** 0.5 * 1.8 / 32)) * 32)
GATE_LORA = max(32, int(round(D_A ** 0.6 * 0.8 / 32)) * 32)
LORA_COLS = DECAY_LORA + ICLR_LORA + GATE_LORA
D_FF = 4 * D_MODEL
DEPTH = 1
ALPHA = (2.0 * DEPTH) ** 0.25
LN_EPS = 1e-5
LNX_EPS = 64e-5
OFF_W = 3 * D_A
RWKV_COLS = OFF_W + LORA_COLS
OFF_Q = RWKV_COLS
OFF_KB = OFF_Q + D_B
OFF_VB = OFF_KB + H_KV * HEAD
N_IN = OFF_VB + H_KV * HEAD
NEG = -1e30

CHUNK = 64
PAIR = 2 * HEAD
NPAIR = H_A // 2
BF16_ROWS = 16
LANES = 128
VMEM_CAP = 56 * 1024 * 1024


def _cparams(sem, vmem_mb):
    return pltpu.CompilerParams(dimension_semantics=sem,
                                vmem_limit_bytes=min(int(vmem_mb * 1024 * 1024), VMEM_CAP))


def _dot(a, b):
    return jnp.dot(a.astype(BF16), b.astype(BF16), preferred_element_type=F32)


def _dot_nt(a, b):
    return lax.dot_general(a.astype(BF16), b.astype(BF16), (((1,), (1,)), ((), ())),
                           preferred_element_type=F32)


def _split2(x):
    hi = x.astype(BF16)
    lo = (x - hi.astype(F32)).astype(BF16)
    return hi, lo


def _layer_norm(x, g, b):
    mu = jnp.mean(x, axis=-1, keepdims=True)
    xc = x - mu
    var = jnp.mean(xc * xc, axis=-1, keepdims=True)
    return xc * lax.rsqrt(var + LN_EPS) * g + b


def _ln_rows(src_ref, consume):
    groups = [slice(g * BF16_ROWS, (g + 1) * BF16_ROWS) for g in range(src_ref.shape[0] // BF16_ROWS)]
    mus = [jnp.mean(src_ref[rs, :], axis=-1, keepdims=True) for rs in groups]
    rstd = [lax.rsqrt(jnp.mean(jnp.square(src_ref[rs, :] - mu), axis=-1, keepdims=True) + LN_EPS)
            for rs, mu in zip(groups, mus)]
    for rs, mu, r in zip(groups, mus, rstd):
        consume(rs, (src_ref[rs, :] - mu) * r, mu, r)


def _head_ones(n):
    r = lax.broadcasted_iota(jnp.int32, (n, n), 0)
    c = lax.broadcasted_iota(jnp.int32, (n, n), 1)
    return ((r >> 6) == (c >> 6)).astype(BF16)


def _mod_kernel(c_ref, w_ref, b_ref, o_ref):
    c = c_ref[...]
    cond = c * jax.nn.sigmoid(c)
    ch, cl = _split2(cond)
    wh, wl = _split2(w_ref[...])
    rows = c.shape[0]
    both = jnp.dot(jnp.concatenate([ch, cl], axis=0), wh, preferred_element_type=F32)
    o_ref[...] = (both[0:rows] + both[rows:] + jnp.dot(ch, wl, preferred_element_type=F32)
                  + b_ref[...])


def _modulation(c8, w_mod, b_mod, tn=1024):
    d, n = w_mod.shape
    return pl.pallas_call(
        _mod_kernel,
        grid=(n // tn,),
        in_specs=[pl.BlockSpec((8, d), lambda j: (0, 0)),
                  pl.BlockSpec((d, tn), lambda j: (0, j)),
                  pl.BlockSpec((1, tn), lambda j: (0, j))],
        out_specs=pl.BlockSpec((8, tn), lambda j: (0, j)),
        out_shape=jax.ShapeDtypeStruct((8, n), F32),
        compiler_params=_cparams(("parallel",), 56),
        name="modulation",
    )(c8, w_mod, b_mod)


def _ln_mod_kernel(x_ref, g_ref, b_ref, mod_ref, u_ref, mu_ref, rstd_ref):
    gain = 1.0 + mod_ref[1:2, :]
    scale = g_ref[...] * gain
    shift = b_ref[...] * gain + mod_ref[0:1, :]

    def consume(rs, xh, mu, rstd):
        u_ref[rs, :] = (xh * scale + shift).astype(u_ref.dtype)
        mu_ref[rs, :] = jnp.broadcast_to(mu, (BF16_ROWS, LANES))
        rstd_ref[rs, :] = jnp.broadcast_to(rstd, (BF16_ROWS, LANES))

    _ln_rows(x_ref, consume)


def _ln_mod(x2, g, b, mod, seq, tr=512):
    t, d = x2.shape
    per = seq // tr
    stat = pl.BlockSpec((tr, LANES), lambda i: (i, 0))
    return pl.pallas_call(
        _ln_mod_kernel,
        grid=(t // tr,),
        in_specs=[pl.BlockSpec((tr, d), lambda i: (i, 0)),
                  pl.BlockSpec((1, d), lambda i: (0, 0)),
                  pl.BlockSpec((1, d), lambda i: (0, 0)),
                  pl.BlockSpec((None, 6, d), lambda i: (i // per, 0, 0))],
        out_specs=[pl.BlockSpec((tr, d), lambda i: (i, 0)), stat, stat],
        out_shape=[jax.ShapeDtypeStruct((t, d), BF16), jax.ShapeDtypeStruct((t, LANES), F32),
                   jax.ShapeDtypeStruct((t, LANES), F32)],
        compiler_params=_cparams(("parallel",), 48),
        name="ln_mod",
    )(x2, g, b, mod)


def _mm_kernel(*refs, relu2, convert):
    if convert:
        *refs, ci_ref, o_ref, co_ref = refs
        co_ref[...] = ci_ref[...].astype(co_ref.dtype)
        refs = (*refs, o_ref)
    *a_refs, b_ref, o_ref = refs
    acc, off = None, 0
    for a_ref in a_refs:
        kd = a_ref.shape[1]
        part = jnp.dot(a_ref[...], b_ref[off:off + kd, :].astype(BF16), preferred_element_type=F32)
        acc = part if acc is None else acc + part
        off += kd
    if relu2:
        acc = jnp.square(jnp.maximum(acc, 0.0))
    o_ref[...] = acc.astype(o_ref.dtype)


def _mm_acc_kernel(a_ref, b_ref, o_ref, acc_ref):
    k = pl.program_id(2)

    @pl.when(k == 0)
    def _():
        acc_ref[...] = jnp.zeros_like(acc_ref)

    acc_ref[...] += jnp.dot(a_ref[...], b_ref[...].astype(BF16), preferred_element_type=F32)

    @pl.when(k == pl.num_programs(2) - 1)
    def _():
        o_ref[...] = acc_ref[...].astype(o_ref.dtype)


def _matmul(a, b, *, tm, tn, tk=None, out_dtype=F32, relu2=False, convert=None, name="matmul"):
    a = a if isinstance(a, (list, tuple)) else [a]
    m = a[0].shape[0]
    kd, n = b.shape
    tm = min(tm, m)
    nj = n // tn
    osz = jnp.dtype(out_dtype).itemsize
    bsz = jnp.dtype(b.dtype).itemsize
    bsz = 2 * bsz + (2 if bsz == 4 else 0)
    if tk is None or tk >= kd:
        vm = (2 * tm * kd * 2 + kd * tn * bsz + 2 * tm * tn * osz + tm * tn * 4) / 2 ** 20 + 8
        in_specs = ([pl.BlockSpec((tm, x.shape[1]), lambda i, j: (i, 0)) for x in a]
                    + [pl.BlockSpec((kd, tn), lambda i, j: (0, j))])
        out_specs = pl.BlockSpec((tm, tn), lambda i, j: (i, j))
        out_shape = jax.ShapeDtypeStruct((m, n), out_dtype)
        args = (*a, b)
        if convert is not None:
            cr, cc = convert.shape
            slab = cr // ((m // tm) * nj)
            assert slab * (m // tm) * nj == cr and slab % BF16_ROWS == 0
            cspec = pl.BlockSpec((slab, cc), lambda i, j: (i * nj + j, 0))
            in_specs, args = in_specs + [cspec], (*args, convert)
            out_specs = [out_specs, cspec]
            out_shape = [out_shape, jax.ShapeDtypeStruct((cr, cc), BF16)]
            vm += slab * cc * 12 / 2 ** 20
        return pl.pallas_call(
            functools.partial(_mm_kernel, relu2=relu2, convert=convert is not None),
            grid=(m // tm, nj),
            in_specs=in_specs,
            out_specs=out_specs,
            out_shape=out_shape,
            compiler_params=_cparams(("parallel", "parallel"), vm),
            name=name,
        )(*args)
    assert not relu2 and len(a) == 1 and convert is None
    vm = (2 * tm * tk * 2 + tk * tn * bsz + 2 * tm * tn * osz + 2 * tm * tn * 4) / 2 ** 20 + 8
    return pl.pallas_call(
        _mm_acc_kernel,
        grid=(m // tm, nj, kd // tk),
        in_specs=[pl.BlockSpec((tm, tk), lambda i, j, k: (i, k)),
                  pl.BlockSpec((tk, tn), lambda i, j, k: (k, j))],
        out_specs=pl.BlockSpec((tm, tn), lambda i, j, k: (i, j)),
        out_shape=jax.ShapeDtypeStruct((m, n), out_dtype),
        scratch_shapes=[pltpu.VMEM((tm, tn), F32)],
        compiler_params=_cparams(("parallel", "parallel", "arbitrary"), vm),
        name=name,
    )(a[0], b)


def _scan_chunk(ins, hts, consts, tick):
    m0, strict, incl, eye, tri = consts

    ticking = [False]

    def each(f, *ls):
        out = [f(*a) for a in zip(*ls)]
        if ticking[0]:
            tick()
        return out

    r, k, v, al, be, lw = (list(z) for z in zip(*ins))

    def cumsum(x):
        hi = x.astype(BF16)
        rem = x - hi.astype(F32)
        mid = rem.astype(BF16)
        lo = (rem - mid.astype(F32)).astype(BF16)
        c3 = jnp.dot(tri, jnp.concatenate([hi, mid, lo], axis=1), preferred_element_type=F32)
        return c3[:, 0:PAIR] + c3[:, PAIR:2 * PAIR] + c3[:, 2 * PAIR:3 * PAIR]

    def sm(x):
        return jnp.concatenate([jnp.where(m0, x, 0.0), jnp.where(m0, 0.0, x)], axis=0)

    b16 = lambda t_: t_.astype(BF16)
    nn = lambda a_, b_: jnp.dot(a_, b_, preferred_element_type=F32)
    c = each(cumsum, lw)
    pc = each(lambda c_: jnp.exp(c_[CHUNK - 1:CHUNK, :]), c)
    einv = each(lambda c_: jnp.exp(-c_), c)
    a_sm = each(lambda a_, c_, l_: b16(sm(a_ * jnp.exp(c_ - l_))), al, c, lw)
    r_sm = each(lambda r_, c_: sm(r_ * jnp.exp(c_)), r, c)
    v_sm = each(lambda v_: b16(sm(v_)), v)
    b_t = each(lambda b_, e_: b_ * e_, be, einv)
    k_t = each(lambda k_, e_: k_ * e_, k, einv)

    def scores(a_, r_, b_, k_):
        bb, kb = b16(b_), b16(k_)
        return _dot_nt(jnp.concatenate([a_, b16(r_)], axis=0),
                       jnp.concatenate([bb, bb, kb, kb], axis=0))

    s = each(scores, a_sm, r_sm, b_t, k_t)
    lab = each(lambda s_: jnp.where(strict, s_[0:PAIR, 0:PAIR], 0.0), s)
    mak = each(lambda s_: b16(jnp.where(strict, s_[0:PAIR, PAIR:], 0.0)), s)
    incl2 = jnp.concatenate([incl, incl], axis=1)
    mrbk = each(lambda s_: b16(jnp.where(incl2, s_[PAIR:, :], 0.0)), s)

    ticking[0] = True
    ldt = each(lambda l_: l_.T, lab)
    xt = each(lambda l_: jnp.where(eye, 1.0, l_), ldt)
    lt = each(lambda l_: nn(b16(l_), b16(l_)), ldt)

    def series_step(l_, x_):
        lb = b16(l_)
        return nn(lb, jnp.concatenate([b16(x_), lb], axis=1))

    for _ in range(4):
        xl = each(series_step, lt, xt)
        xt = each(lambda x_, p_: x_ + p_[:, 0:PAIR], xt, xl)
        lt = each(lambda p_: p_[:, PAIR:], xl)
    x = each(lambda x_, l_: b16((x_ + nn(b16(l_), b16(x_))).T), xt, lt)

    makv = each(nn, mak, v_sm)
    wu = each(lambda x_, a_, m_: nn(x_, jnp.concatenate([a_, b16(m_)], axis=1)), x, a_sm, makv)
    bigr = each(lambda wu_, v_: jnp.concatenate(
        [b16(wu_), jnp.concatenate([jnp.zeros_like(v_), v_], axis=1)], axis=0), wu, v_sm)
    bk = each(lambda b_, k_, p_: b16(jnp.concatenate([sm(b_ * p_), sm(k_ * p_)], axis=0).T),
              b_t, k_t, pc)
    gz = each(nn, bk, bigr)
    qy = each(nn, mrbk, bigr)
    hb = each(b16, hts)

    def new_state(ht, h_, g_, p_):
        return ht * p_ + _dot_nt(h_, g_[:, 0:PAIR]) + g_[:, PAIR:].T

    def output(r_, q_, h_):
        y_sm = _dot_nt(r_ + q_[:, 0:PAIR], h_) + q_[:, PAIR:]
        return y_sm[0:CHUNK, :] + y_sm[CHUNK:, :]

    return each(output, r_sm, qy, hb), each(new_state, hts, hb, gz, pc)


def _rwkv_kernel(pr_ref, pk_ref, pv_ref, pc_ref, qr_ref, qk_ref, qv_ref, qc_ref,
                 mr_ref, mk_ref, mv_ref, mc_ref, w0_ref, a0_ref, kk_ref, ka_ref, wl_ref,
                 rk_ref, lg_ref, lb_ref, o_ref,
                 h_ref, y_ref, r_s, k_s, v_s, al_s, be_s, lw_s, g_s, ud_s, *, nchunk):
    first = pl.program_id(1) == 0

    @pl.when(first)
    def _():
        h_ref[...] = jnp.zeros_like(h_ref)

    row = lax.broadcasted_iota(jnp.int32, (PAIR, PAIR), 0)
    col = lax.broadcasted_iota(jnp.int32, (PAIR, PAIR), 1)
    same = (row >> 6) == (col >> 6)
    tr_ = row & (CHUNK - 1)
    tc_ = col & (CHUNK - 1)
    strict = jnp.logical_and(same, tc_ < tr_)
    incl = jnp.logical_and(same, tc_ <= tr_)
    eye = row == col
    m0 = lax.broadcasted_iota(jnp.int32, (CHUNK, PAIR), 1) < HEAD
    tri = (lax.broadcasted_iota(jnp.int32, (CHUNK, CHUNK), 1)
           <= lax.broadcasted_iota(jnp.int32, (CHUNK, CHUNK), 0)).astype(BF16)
    consts = (m0, strict, incl, eye, tri)
    ones = _head_ones(2 * PAIR)
    slab = 4 * PAIR

    def prep_stages(cn):
        at_start = isinstance(cn, int)
        rows = pl.ds(0, CHUNK) if at_start else pl.ds(pl.multiple_of(cn * CHUNK, CHUNK), CHUNK)

        def shift(x_ref, q_ref, m_ref, ln):
            x = x_ref[rows, ln].astype(F32)
            if at_start:
                last = jnp.where(first, 0.0, q_ref[BF16_ROWS - 1:BF16_ROWS, ln].astype(F32))
            else:
                tail = pl.ds(pl.multiple_of(cn * CHUNK - BF16_ROWS, BF16_ROWS), BF16_ROWS)
                last = x_ref[tail, ln][BF16_ROWS - 1:BF16_ROWS, :].astype(F32)
            prev = pltpu.roll(x, 1, axis=0)
            r8 = lax.broadcasted_iota(jnp.int32, (8, x.shape[1]), 0)
            head = jnp.where(r8 == 0, jnp.broadcast_to(last, (8, x.shape[1])), prev[0:8, :])
            prev = jnp.concatenate([head, prev[8:, :]], axis=0)
            return x + (prev - x) * m_ref[:, ln]

        for j in range(D_A // slab):
            ln = slice(j * slab, (j + 1) * slab)
            r_s[rows, ln] = shift(pr_ref, qr_ref, mr_ref, ln)
            v_s[rows, ln] = shift(pv_ref, qv_ref, mv_ref, ln)
            yield
            k = shift(pk_ref, qk_ref, mk_ref, ln)
            kk = k * kk_ref[:, ln]
            kk2 = kk * kk
            ss = jnp.concatenate([_dot(kk2[:, i * 2 * PAIR:(i + 1) * 2 * PAIR], ones)
                                  for i in range(slab // (2 * PAIR))], axis=1)
            yield
            a = jax.nn.sigmoid(a0_ref[:, ln] + ud_s[rows, D_A + j * slab:D_A + (j + 1) * slab])
            kk = kk * lax.rsqrt(jnp.maximum(ss, 1e-24))
            k_s[rows, ln] = k * (1.0 + (a - 1.0) * ka_ref[:, ln])
            al_s[rows, ln] = -kk
            be_s[rows, ln] = kk * a
            lw_s[rows, ln] = -math.exp(-0.5) * jax.nn.sigmoid(w0_ref[:, ln] + ud_s[rows, ln])
            yield

    code = pc_ref[...].astype(F32)
    last = jnp.where(first, 0.0, qc_ref[BF16_ROWS - 1:BF16_ROWS, :].astype(F32))
    prev = pltpu.roll(code, 1, axis=0)
    r8 = lax.broadcasted_iota(jnp.int32, (8, LORA_COLS), 0)
    prev = jnp.concatenate([jnp.where(r8 == 0, jnp.broadcast_to(last, (8, LORA_COLS)), prev[0:8, :]),
                            prev[8:, :]], axis=0)
    code = code + (prev - code) * mc_ref[...]
    lane = lax.broadcasted_iota(jnp.int32, code.shape, 1)
    act = jnp.where(lane < DECAY_LORA, jnp.tanh(code),
                    jnp.where(lane < DECAY_LORA + ICLR_LORA, code, jax.nn.sigmoid(code))).astype(BF16)
    ud_s[:, 0:D_A] = jnp.dot(act, wl_ref[:, 0:D_A], preferred_element_type=F32)
    ud_s[:, D_A:] = jnp.dot(act, wl_ref[:, D_A:2 * D_A], preferred_element_type=F32)
    g_s[...] = jnp.dot(act, wl_ref[:, 2 * D_A:], preferred_element_type=F32)

    for _ in prep_stages(0):
        pass

    def chunk_body(ci, carry):
        rows = pl.ds(pl.multiple_of(ci * CHUNK, CHUNK), CHUNK)
        lanes = [slice(p * PAIR, (p + 1) * PAIR) for p in range(NPAIR)]
        ins = [tuple(s[rows, ln] for s in (r_s, k_s, v_s, al_s, be_s, lw_s)) for ln in lanes]
        hts = [h_ref[p] for p in range(NPAIR)]
        gen = prep_stages(jnp.minimum(ci + 1, nchunk - 1))
        ys, hns = _scan_chunk(ins, hts, consts, lambda: next(gen, None))
        for _ in gen:
            pass
        for p in range(NPAIR):
            h_ref[p] = hns[p]
            y_ref[rows, lanes[p]] = ys[p]
        return carry

    lax.fori_loop(0, nchunk, chunk_body, 0)

    tb = y_ref.shape[0]
    width = 2 * PAIR
    for half in range(2):
        lns = [slice(q * width, (q + 1) * width)
               for q in range(half * NPAIR // 4, (half + 1) * NPAIR // 4)]
        stack = lambda f: jnp.concatenate([f(ln) for ln in lns], axis=0)
        y = stack(lambda ln: y_ref[:, ln])
        sums = _dot(jnp.concatenate(
            [y, stack(lambda ln: r_s[:, ln] * k_s[:, ln] * rk_ref[:, ln])], axis=0), ones)
        yc = y - sums[0:len(lns) * tb] * (1.0 / HEAD)
        rstd = lax.rsqrt(_dot(yc * yc, ones) * (1.0 / HEAD) + LNX_EPS)
        for i, ln in enumerate(lns):
            rs = slice(i * tb, (i + 1) * tb)
            yn = yc[rs] * rstd[rs] * lg_ref[:, ln] + lb_ref[:, ln]
            rk = sums[len(lns) * tb + i * tb:len(lns) * tb + (i + 1) * tb]
            o_ref[:, ln] = ((yn + rk * v_s[:, ln]) * g_s[:, ln]).astype(o_ref.dtype)


def _rwkv(p, mu, w0, a0, k_k, k_a, w_lora, r_k, lnx_g, lnx_b, bsz, seq, tb=256):
    t = p.shape[0]
    per = seq // tb
    nchunk = tb // CHUNK
    assert nchunk >= 2
    cblk = OFF_W // LORA_COLS
    tail = tb // BF16_ROWS

    def cur(j):
        return pl.BlockSpec((tb, D_A), lambda b, s: (b * per + s, j))

    def prev(j):
        return pl.BlockSpec((BF16_ROWS, D_A),
                            lambda b, s: (jnp.maximum((b * per + s) * tail - 1, 0), j))

    def vec(j):
        return pl.BlockSpec((1, D_A), lambda b, s: (0, j))

    in_specs = [cur(0), cur(1), cur(2),
                pl.BlockSpec((tb, LORA_COLS), lambda b, s: (b * per + s, cblk)),
                prev(0), prev(1), prev(2),
                pl.BlockSpec((BF16_ROWS, LORA_COLS),
                             lambda b, s: (jnp.maximum((b * per + s) * tail - 1, 0), cblk)),
                vec(0), vec(1), vec(2),
                pl.BlockSpec((1, LORA_COLS), lambda b, s: (0, cblk)),
                vec(0), vec(0), vec(0), vec(0),
                pl.BlockSpec((LORA_COLS, 3 * D_A), lambda b, s: (0, 0)),
                vec(0), vec(0), vec(0)]
    blk = (tb, D_A)
    return pl.pallas_call(
        functools.partial(_rwkv_kernel, nchunk=nchunk),
        grid=(bsz, per),
        in_specs=in_specs,
        out_specs=pl.BlockSpec(blk, lambda b, s: (b * per + s, 0)),
        out_shape=jax.ShapeDtypeStruct((t, D_A), BF16),
        scratch_shapes=([pltpu.VMEM((NPAIR, PAIR, PAIR), F32)] + [pltpu.VMEM(blk, F32)] * 8
                        + [pltpu.VMEM((tb, 2 * D_A), F32)]),
        compiler_params=_cparams(("parallel", "arbitrary"), 52),
        name="rwkv",
    )(p, p, p, p, p, p, p, p, mu, mu, mu, mu, w0, a0, k_k, k_a, w_lora, r_k, lnx_g, lnx_b)


def _bucket_table():
    qi = np.arange(BLOCK)[:, None]
    kj = np.arange(2 * BLOCK)[None, :]
    dist = qi + BLOCK - kj
    n = np.maximum(dist, 0)
    nf = np.maximum(n, 1).astype(np.float32)
    large = RPB_MAX_EXACT + (np.log(nf / np.float32(RPB_MAX_EXACT))
                             / np.float32(math.log(RPB_MAX_DIST / RPB_MAX_EXACT))
                             * np.float32(RPB_BUCKETS - RPB_MAX_EXACT)).astype(np.int32)
    large = np.minimum(large, RPB_BUCKETS - 1)
    bucket = np.where(n < RPB_MAX_EXACT, n, large)
    valid = (dist >= 0) & (dist < WINDOW)
    return np.where(valid, bucket, -1).astype(np.int32)


def _bias_kernel(tab_ref, bkt_ref, o_ref):
    h0 = pl.program_id(0) * GQA
    bkt = bkt_ref[...]
    prev_block = lax.broadcasted_iota(jnp.int32, bkt.shape, 1) < BLOCK
    for i in range(GQA):
        acc = jnp.full(bkt.shape, NEG, F32)
        for b in range(RPB_BUCKETS):
            acc = jnp.where(bkt == b, tab_ref[b, h0 + i], acc)
        o_ref[0, i] = acc
        o_ref[1, i] = jnp.where(prev_block, NEG, acc)


def _attn_bias(rpb_table):
    bkt = jnp.asarray(_bucket_table())
    return pl.pallas_call(
        _bias_kernel,
        grid=(H_KV,),
        in_specs=[pl.BlockSpec(memory_space=pltpu.SMEM),
                  pl.BlockSpec((BLOCK, 2 * BLOCK), lambda g: (0, 0))],
        out_specs=pl.BlockSpec((2, GQA, BLOCK, 2 * BLOCK), lambda g: (0, g, 0, 0)),
        out_shape=jax.ShapeDtypeStruct((2, H_Q, BLOCK, 2 * BLOCK), F32),
        compiler_params=_cparams(("arbitrary",), 16),
        name="attn_bias",
    )(rpb_table, bkt)


def _swa_kernel(sink_ref, *refs):
    *q_refs, kc_ref, kp_ref, vc_ref, vp_ref, bias_ref, o_ref = refs
    each = lambda f, *ls: [f(*a) for a in zip(*ls)]
    lo = lax.broadcasted_iota(jnp.int32, (BLOCK, PAIR), 1) < HEAD
    scale = HEAD ** -0.5
    zeros = jnp.zeros((2 * BLOCK, HEAD), BF16)
    ones = jnp.ones((2 * BLOCK, HEAD), BF16)
    npr = GQA // 2
    kcat = jnp.concatenate([kp_ref[...], kc_ref[...]], axis=0).astype(F32) * scale
    vcat = jnp.concatenate([vp_ref[...], vc_ref[...]], axis=0).astype(F32)
    for g in range(H_KV):
        gsl = slice(g * HEAD, (g + 1) * HEAD)
        kg = kcat[:, gsl].astype(BF16)
        vg = vcat[:, gsl].astype(BF16)
        kdup = jnp.concatenate([kg, kg], axis=1)
        rhs = jnp.concatenate([jnp.concatenate([vg, zeros, ones, zeros], axis=1),
                               jnp.concatenate([zeros, vg, zeros, ones], axis=1)], axis=0)
        heads = [g * GQA + 2 * i for i in range(npr)]
        lanes = [slice(h * HEAD, (h + 2) * HEAD) for h in heads]
        qp = [q_refs[h // 4][:, (h % 4) * HEAD:(h % 4 + 2) * HEAD] for h in heads]
        s2 = each(lambda q_: _dot_nt(jnp.concatenate([jnp.where(lo, q_, 0.0).astype(BF16),
                                                      jnp.where(lo, 0.0, q_).astype(BF16)], axis=0),
                                     kdup), qp)
        s = [(s_[0:BLOCK] + bias_ref[h], s_[BLOCK:] + bias_ref[h + 1]) for s_, h in zip(s2, heads)]
        m = [(jnp.maximum(jnp.max(a, axis=-1, keepdims=True), sink_ref[0, h]),
              jnp.maximum(jnp.max(b, axis=-1, keepdims=True), sink_ref[0, h + 1]))
             for (a, b), h in zip(s, heads)]
        e = each(lambda s_, m_: jnp.concatenate([jnp.exp(s_[0] - m_[0]), jnp.exp(s_[1] - m_[1])],
                                                axis=1).astype(BF16), s, m)
        od = each(lambda e_: jnp.dot(e_, rhs, preferred_element_type=F32), e)
        for o_, m_, h, ln in zip(od, m, heads, lanes):
            den = o_[:, PAIR:] + jnp.where(lo, jnp.exp(sink_ref[0, h] - m_[0]),
                                           jnp.exp(sink_ref[0, h + 1] - m_[1]))
            o_ref[:, ln] = (o_[:, 0:PAIR] / den).astype(o_ref.dtype)


def _swa(p, bias, sinks, bsz, seq):
    t = p.shape[0]
    nb = seq // BLOCK
    kvw = H_KV * HEAD
    nq = D_B // kvw

    def cur(c):
        return pl.BlockSpec((BLOCK, kvw), lambda b, n: (b * nb + n, c))

    def prev(c):
        return pl.BlockSpec((BLOCK, kvw), lambda b, n: (b * nb + jnp.maximum(n - 1, 0), c))

    kb, vb = OFF_KB // kvw, OFF_VB // kvw
    return pl.pallas_call(
        _swa_kernel,
        grid=(bsz, nb),
        in_specs=[pl.BlockSpec(memory_space=pltpu.SMEM)]
        + [cur(OFF_Q // kvw + i) for i in range(nq)]
        + [cur(kb), prev(kb), cur(vb), prev(vb),
           pl.BlockSpec((None, H_Q, BLOCK, 2 * BLOCK),
                        lambda b, n: (jnp.where(n == 0, 1, 0), 0, 0, 0))],
        out_specs=pl.BlockSpec((BLOCK, D_B), lambda b, n: (b * nb + n, 0)),
        out_shape=jax.ShapeDtypeStruct((t, D_B), BF16),
        compiler_params=_cparams(("parallel", "arbitrary"), 32),
        name="swa",
    )(sinks, *([p] * (nq + 4)), bias)


def _post_mix_kernel(x_ref, mu_ref, rstd_ref, mix_ref, ge_ref, be_ref, g1_ref, b1_ref, mod_ref,
                     x1_ref, u_ref):
    wide = lambda s_: jnp.concatenate([s_] * (x_ref.shape[1] // LANES), axis=1)
    axn = ((x_ref[...] - wide(mu_ref[...])) * wide(rstd_ref[...]) * (ALPHA * ge_ref[...])
           + ALPHA * be_ref[...])
    z = axn + (1.0 + mod_ref[2:3, :]) * mix_ref[...].astype(F32)
    x1 = _layer_norm(z, g1_ref[...], b1_ref[...])
    x1_ref[...] = x1
    u_ref[...] = (x1 * (1.0 + mod_ref[4:5, :]) + mod_ref[3:4, :]).astype(u_ref.dtype)


def _post_mix(x2, mu, rstd, mix, ge, be, g1, b1, mod, seq, tr=256):
    t, d = x2.shape
    per = seq // tr
    row = pl.BlockSpec((tr, d), lambda i: (i, 0))
    stat = pl.BlockSpec((tr, LANES), lambda i: (i, 0))
    vec = pl.BlockSpec((1, d), lambda i: (0, 0))
    return pl.pallas_call(
        _post_mix_kernel,
        grid=(t // tr,),
        in_specs=[row, stat, stat, row, vec, vec, vec, vec,
                  pl.BlockSpec((None, 6, d), lambda i: (i // per, 0, 0))],
        out_specs=[row, row],
        out_shape=[jax.ShapeDtypeStruct((t, d), F32), jax.ShapeDtypeStruct((t, d), BF16)],
        compiler_params=_cparams(("parallel",), 52),
        name="post_mix",
    )(x2, mu, rstd, mix, ge, be, g1, b1, mod)


def _final_kernel(x1_ref, h_ref, g2_ref, b2_ref, mod_ref, o_ref):
    z = ALPHA * x1_ref[...] + (1.0 + mod_ref[5:6, :]) * h_ref[...].astype(F32)
    o_ref[...] = _layer_norm(z, g2_ref[...], b2_ref[...])


def _final(x1, h, g2, b2, mod, seq, tr=256):
    t, d = x1.shape
    per = seq // tr
    row = pl.BlockSpec((tr, d), lambda i: (i, 0))
    vec = pl.BlockSpec((1, d), lambda i: (0, 0))
    return pl.pallas_call(
        _final_kernel,
        grid=(t // tr,),
        in_specs=[row, row, vec, vec, pl.BlockSpec((None, 6, d), lambda i: (i // per, 0, 0))],
        out_specs=row,
        out_shape=jax.ShapeDtypeStruct((t, d), F32),
        compiler_params=_cparams(("parallel",), 48),
        name="final_ln",
    )(x1, h, g2, b2, mod)


def _lora_weights(w_decay_up, w_iclr_up, w_gate_up):
    zd = jnp.zeros((LORA_COLS, D_A), F32)
    wd = zd.at[0:DECAY_LORA].set(w_decay_up)
    wa = zd.at[DECAY_LORA:DECAY_LORA + ICLR_LORA].set(w_iclr_up)
    wg = zd.at[DECAY_LORA + ICLR_LORA:].set(w_gate_up)
    return jnp.concatenate([wd, wa, wg], axis=1).astype(BF16)


def kernel(x, c, ln_emb_g, ln_emb_b, rpb_table, w_mod, b_mod, w_in, mu_shift, w0, w_decay_up, a0,
           w_iclr_up, w_gate_up, k_k, k_a, r_k, lnx_g, lnx_b, attn_sinks, w_out, ln1_g, ln1_b,
           w_up, w_down, ln2_g, ln2_b):
    bsz, seq, d = x.shape
    assert w_mod.shape[0] == DEPTH == 1 and d == D_MODEL and bsz <= 8
    t = bsz * seq
    row = lambda a: a.reshape(1, -1)
    x2 = x.reshape(t, d)
    c8 = jnp.pad(c, ((0, 8 - bsz), (0, 0)))
    bias = _attn_bias(rpb_table)
    mod = _modulation(c8, w_mod[0], row(b_mod[0]))[:bsz].reshape(bsz, 6, d)
    u1, mu, rstd = _ln_mod(x2, row(ln_emb_g), row(ln_emb_b), mod, seq)
    p = _matmul(u1, w_in[0].astype(BF16), tm=1024, tn=1280, out_dtype=BF16, name="in_proj")
    w_lora = _lora_weights(w_decay_up[0], w_iclr_up[0], w_gate_up[0])
    y_a = _rwkv(p, row(mu_shift[0]), row(w0[0]), row(a0[0]), row(k_k[0]), row(k_a[0]), w_lora,
                row(r_k[0]), row(lnx_g[0]), row(lnx_b[0]), bsz, seq)
    y_b = _swa(p, bias, row(attn_sinks[0]), bsz, seq)
    mix = _matmul([y_a, y_b], w_out[0], tm=1024, tn=512, out_dtype=BF16, name="out_proj")
    x1, u2 = _post_mix(x2, mu, rstd, mix, row(ln_emb_g), row(ln_emb_b), row(ln1_g[0]),
                       row(ln1_b[0]), mod, seq)
    hmid, wd = _matmul(u2, w_up[0], tm=1024, tn=512, out_dtype=BF16, relu2=True,
                       convert=w_down[0], name="mlp_up")
    hout = _matmul(hmid, wd, tm=1024, tn=1024, tk=4096, out_dtype=BF16, name="mlp_down")
    out = _final(x1, hout, row(ln2_g[0]), row(ln2_b[0]), mod, seq)
    return out.reshape(bsz, seq, d)
```

```python
import functools
import math

import numpy as np
import jax
import jax.numpy as jnp
from jax import lax
from jax.experimental import pallas as pl
from jax.experimental.pallas import tpu as pltpu

F32 = jnp.float32
BF16 = jnp.bfloat16

D_MODEL = 4096
HEAD = 64
D_A = D_MODEL // 2
D_B = D_MODEL - D_A
H_A = D_A // HEAD
H_Q = D_B // HEAD
GQA = 8
H_KV = H_Q // GQA
WINDOW = 128
BLOCK = 128
RPB_BUCKETS = 32
RPB_MAX_EXACT = RPB_BUCKETS // 2
RPB_MAX_DIST = 128
DECAY_LORA = max(32, int(round(D_A ** 0.5 * 1.8 / 32)) * 32)
ICLR_LORA = max(32, int(round(D_A ** 0.5 * 1.8 / 32)) * 32)
GATE_LORA = max(32, int(round(D_A ** 0.6 * 0.8 / 32)) * 32)
LORA_COLS = DECAY_LORA + ICLR_LORA + GATE_LORA
D_FF = 4 * D_MODEL
DEPTH = 1
ALPHA = (2.0 * DEPTH) ** 0.25
LN_EPS = 1e-5
LNX_EPS = 64e-5
OFF_W = 3 * D_A
RWKV_COLS = OFF_W + LORA_COLS
OFF_Q = RWKV_COLS
OFF_KB = OFF_Q + D_B
OFF_VB = OFF_KB + H_KV * HEAD
N_IN = OFF_VB + H_KV * HEAD
NEG = -1e30

CHUNK = 64
PAIR = 2 * HEAD
NPAIR = H_A // 2
BF16_ROWS = 16
LANES = 128
VMEM_CAP = 56 * 1024 * 1024


def _cparams(sem, vmem_mb):
    return pltpu.CompilerParams(dimension_semantics=sem,
                                vmem_limit_bytes=min(int(vmem_mb * 1024 * 1024), VMEM_CAP))


def _dot(a, b):
    return jnp.dot(a.astype(BF16), b.astype(BF16), preferred_element_type=F32)


def _dot_nt(a, b):
    return lax.dot_general(a.astype(BF16), b.astype(BF16), (((1,), (1,)), ((), ())),
                           preferred_element_type=F32)


def _split2(x):
    hi = x.astype(BF16)
    lo = (x - hi.astype(F32)).astype(BF16)
    return hi, lo


def _layer_norm(x, g, b):
    mu = jnp.mean(x, axis=-1, keepdims=True)
    xc = x - mu
    var = jnp.mean(xc * xc, axis=-1, keepdims=True)
    return xc * lax.rsqrt(var + LN_EPS) * g + b


def _ln_rows(src_ref, consume):
    groups = [slice(g * BF16_ROWS, (g + 1) * BF16_ROWS) for g in range(src_ref.shape[0] // BF16_ROWS)]
    mus = [jnp.mean(src_ref[rs, :], axis=-1, keepdims=True) for rs in groups]
    rstd = [lax.rsqrt(jnp.mean(jnp.square(src_ref[rs, :] - mu), axis=-1, keepdims=True) + LN_EPS)
            for rs, mu in zip(groups, mus)]
    for rs, mu, r in zip(groups, mus, rstd):
        consume(rs, (src_ref[rs, :] - mu) * r, mu, r)


def _head_ones(n):
    r = lax.broadcasted_iota(jnp.int32, (n, n), 0)
    c = lax.broadcasted_iota(jnp.int32, (n, n), 1)
    return ((r >> 6) == (c >> 6)).astype(BF16)


def _mod_kernel(c_ref, w_ref, b_ref, o_ref):
    c = c_ref[...]
    cond = c * jax.nn.sigmoid(c)
    ch, cl = _split2(cond)
    wh, wl = _split2(w_ref[...])
    rows = c.shape[0]
    both = jnp.dot(jnp.concatenate([ch, cl], axis=0), wh, preferred_element_type=F32)
    o_ref[...] = (both[0:rows] + both[rows:] + jnp.dot(ch, wl, preferred_element_type=F32)
                  + b_ref[...])


def _modulation(c8, w_mod, b_mod, tn=1024):
    d, n = w_mod.shape
    return pl.pallas_call(
        _mod_kernel,
        grid=(n // tn,),
        in_specs=[pl.BlockSpec((8, d), lambda j: (0, 0)),
                  pl.BlockSpec((d, tn), lambda j: (0, j)),
                  pl.BlockSpec((1, tn), lambda j: (0, j))],
        out_specs=pl.BlockSpec((8, tn), lambda j: (0, j)),
        out_shape=jax.ShapeDtypeStruct((8, n), F32),
        compiler_params=_cparams(("parallel",), 56),
        name="modulation",
    )(c8, w_mod, b_mod)


def _ln_mod_kernel(x_ref, g_ref, b_ref, mod_ref, u_ref, mu_ref, rstd_ref):
    gain = 1.0 + mod_ref[1:2, :]
    scale = g_ref[...] * gain
    shift = b_ref[...] * gain + mod_ref[0:1, :]

    def consume(rs, xh, mu, rstd):
        u_ref[rs, :] = (xh * scale + shift).astype(u_ref.dtype)
        mu_ref[rs, :] = jnp.broadcast_to(mu, (BF16_ROWS, LANES))
        rstd_ref[rs, :] = jnp.broadcast_to(rstd, (BF16_ROWS, LANES))

    _ln_rows(x_ref, consume)


def _ln_mod(x2, g, b, mod, seq, tr=512):
    t, d = x2.shape
    per = seq // tr
    stat = pl.BlockSpec((tr, LANES), lambda i: (i, 0))
    return pl.pallas_call(
        _ln_mod_kernel,
        grid=(t // tr,),
        in_specs=[pl.BlockSpec((tr, d), lambda i: (i, 0)),
                  pl.BlockSpec((1, d), lambda i: (0, 0)),
                  pl.BlockSpec((1, d), lambda i: (0, 0)),
                  pl.BlockSpec((None, 6, d), lambda i: (i // per, 0, 0))],
        out_specs=[pl.BlockSpec((tr, d), lambda i: (i, 0)), stat, stat],
        out_shape=[jax.ShapeDtypeStruct((t, d), BF16), jax.ShapeDtypeStruct((t, LANES), F32),
                   jax.ShapeDtypeStruct((t, LANES), F32)],
        compiler_params=_cparams(("parallel",), 48),
        name="ln_mod",
    )(x2, g, b, mod)


def _mm_kernel(*refs, relu2, convert):
    if convert:
        *refs, ci_ref, o_ref, co_ref = refs
        co_ref[...] = ci_ref[...].astype(co_ref.dtype)
        refs = (*refs, o_ref)
    *a_refs, b_ref, o_ref = refs
    acc, off = None, 0
    for a_ref in a_refs:
        kd = a_ref.shape[1]
        part = jnp.dot(a_ref[...], b_ref[off:off + kd, :].astype(BF16), preferred_element_type=F32)
        acc = part if acc is None else acc + part
        off += kd
    if relu2:
        acc = jnp.square(jnp.maximum(acc, 0.0))
    o_ref[...] = acc.astype(o_ref.dtype)


def _mm_acc_kernel(a_ref, b_ref, o_ref, acc_ref):
    k = pl.program_id(2)

    @pl.when(k == 0)
    def _():
        acc_ref[...] = jnp.zeros_like(acc_ref)

    acc_ref[...] += jnp.dot(a_ref[...], b_ref[...].astype(BF16), preferred_element_type=F32)

    @pl.when(k == pl.num_programs(2) - 1)
    def _():
        o_ref[...] = acc_ref[...].astype(o_ref.dtype)


def _matmul(a, b, *, tm, tn, tk=None, out_dtype=F32, relu2=False, convert=None, name="matmul"):
    a = a if isinstance(a, (list, tuple)) else [a]
    m = a[0].shape[0]
    kd, n = b.shape
    tm = min(tm, m)
    nj = n // tn
    osz = jnp.dtype(out_dtype).itemsize
    bsz = jnp.dtype(b.dtype).itemsize
    bsz = 2 * bsz + (2 if bsz == 4 else 0)
    if tk is None or tk >= kd:
        vm = (2 * tm * kd * 2 + kd * tn * bsz + 2 * tm * tn * osz + tm * tn * 4) / 2 ** 20 + 8
        in_specs = ([pl.BlockSpec((tm, x.shape[1]), lambda i, j: (i, 0)) for x in a]
                    + [pl.BlockSpec((kd, tn), lambda i, j: (0, j))])
        out_specs = pl.BlockSpec((tm, tn), lambda i, j: (i, j))
        out_shape = jax.ShapeDtypeStruct((m, n), out_dtype)
        args = (*a, b)
        if convert is not None:
            cr, cc = convert.shape
            slab = cr // ((m // tm) * nj)
            assert slab * (m // tm) * nj == cr and slab % BF16_ROWS == 0
            cspec = pl.BlockSpec((slab, cc), lambda i, j: (i * nj + j, 0))
            in_specs, args = in_specs + [cspec], (*args, convert)
            out_specs = [out_specs, cspec]
            out_shape = [out_shape, jax.ShapeDtypeStruct((cr, cc), BF16)]
            vm += slab * cc * 12 / 2 ** 20
        return pl.pallas_call(
            functools.partial(_mm_kernel, relu2=relu2, convert=convert is not None),
            grid=(m // tm, nj),
            in_specs=in_specs,
            out_specs=out_specs,
            out_shape=out_shape,
            compiler_params=_cparams(("parallel", "parallel"), vm),
            name=name,
        )(*args)
    assert not relu2 and len(a) == 1 and convert is None
    vm = (2 * tm * tk * 2 + tk * tn * bsz + 2 * tm * tn * osz + 2 * tm * tn * 4) / 2 ** 20 + 8
    return pl.pallas_call(
        _mm_acc_kernel,
        grid=(m // tm, nj, kd // tk),
        in_specs=[pl.BlockSpec((tm, tk), lambda i, j, k: (i, k)),
                  pl.BlockSpec((tk, tn), lambda i, j, k: (k, j))],
        out_specs=pl.BlockSpec((tm, tn), lambda i, j, k: (i, j)),
        out_shape=jax.ShapeDtypeStruct((m, n), out_dtype),
        scratch_shapes=[pltpu.VMEM((tm, tn), F32)],
        compiler_params=_cparams(("parallel", "parallel", "arbitrary"), vm),
        name=name,
    )(a[0], b)


def _scan_chunk(ins, hts, consts, tick):
    m0, strict, incl, eye, tri = consts

    ticking = [False]

    def each(f, *ls):
        out = [f(*a) for a in zip(*ls)]
        if ticking[0]:
            tick()
        return out

    r, k, v, al, be, lw = (list(z) for z in zip(*ins))

    def cumsum(x):
        hi = x.astype(BF16)
        rem = x - hi.astype(F32)
        mid = rem.astype(BF16)
        lo = (rem - mid.astype(F32)).astype(BF16)
        c3 = jnp.dot(tri, jnp.concatenate([hi, mid, lo], axis=1), preferred_element_type=F32)
        return c3[:, 0:PAIR] + c3[:, PAIR:2 * PAIR] + c3[:, 2 * PAIR:3 * PAIR]

    def sm(x):
        return jnp.concatenate([jnp.where(m0, x, 0.0), jnp.where(m0, 0.0, x)], axis=0)

    b16 = lambda t_: t_.astype(BF16)
    nn = lambda a_, b_: jnp.dot(a_, b_, preferred_element_type=F32)
    c = each(cumsum, lw)
    pc = each(lambda c_: jnp.exp(c_[CHUNK - 1:CHUNK, :]), c)
    einv = each(lambda c_: jnp.exp(-c_), c)
    a_sm = each(lambda a_, c_, l_: b16(sm(a_ * jnp.exp(c_ - l_))), al, c, lw)
    r_sm = each(lambda r_, c_: sm(r_ * jnp.exp(c_)), r, c)
    v_sm = each(lambda v_: b16(sm(v_)), v)
    b_t = each(lambda b_, e_: b_ * e_, be, einv)
    k_t = each(lambda k_, e_: k_ * e_, k, einv)

    def scores(a_, r_, b_, k_):
        bb, kb = b16(b_), b16(k_)
        return _dot_nt(jnp.concatenate([a_, b16(r_)], axis=0),
                       jnp.concatenate([bb, bb, kb, kb], axis=0))

    s = each(scores, a_sm, r_sm, b_t, k_t)
    lab = each(lambda s_: jnp.where(strict, s_[0:PAIR, 0:PAIR], 0.0), s)
    mak = each(lambda s_: b16(jnp.where(strict, s_[0:PAIR, PAIR:], 0.0)), s)
    incl2 = jnp.concatenate([incl, incl], axis=1)
    mrbk = each(lambda s_: b16(jnp.where(incl2, s_[PAIR:, :], 0.0)), s)

    ticking[0] = True
    ldt = each(lambda l_: l_.T, lab)
    xt = each(lambda l_: jnp.where(eye, 1.0, l_), ldt)
    lt = each(lambda l_: nn(b16(l_), b16(l_)), ldt)

    def series_step(l_, x_):
        lb = b16(l_)
        return nn(lb, jnp.concatenate([b16(x_), lb], axis=1))

    for _ in range(4):
        xl = each(series_step, lt, xt)
        xt = each(lambda x_, p_: x_ + p_[:, 0:PAIR], xt, xl)
        lt = each(lambda p_: p_[:, PAIR:], xl)
    x = each(lambda x_, l_: b16((x_ + nn(b16(l_), b16(x_))).T), xt, lt)

    makv = each(nn, mak, v_sm)
    wu = each(lambda x_, a_, m_: nn(x_, jnp.concatenate([a_, b16(m_)], axis=1)), x, a_sm, makv)
    bigr = each(lambda wu_, v_: jnp.concatenate(
        [b16(wu_), jnp.concatenate([jnp.zeros_like(v_), v_], axis=1)], axis=0), wu, v_sm)
    bk = each(lambda b_, k_, p_: b16(jnp.concatenate([sm(b_ * p_), sm(k_ * p_)], axis=0).T),
              b_t, k_t, pc)
    gz = each(nn, bk, bigr)
    qy = each(nn, mrbk, bigr)
    hb = each(b16, hts)

    def new_state(ht, h_, g_, p_):
        return ht * p_ + _dot_nt(h_, g_[:, 0:PAIR]) + g_[:, PAIR:].T

    def output(r_, q_, h_):
        y_sm = _dot_nt(r_ + q_[:, 0:PAIR], h_) + q_[:, PAIR:]
        return y_sm[0:CHUNK, :] + y_sm[CHUNK:, :]

    return each(output, r_sm, qy, hb), each(new_state, hts, hb, gz, pc)


def _rwkv_kernel(pr_ref, pk_ref, pv_ref, pc_ref, qr_ref, qk_ref, qv_ref, qc_ref,
                 mr_ref, mk_ref, mv_ref, mc_ref, w0_ref, a0_ref, kk_ref, ka_ref, wl_ref,
                 rk_ref, lg_ref, lb_ref, o_ref,
                 h_ref, y_ref, r_s, k_s, v_s, al_s, be_s, lw_s, g_s, ud_s, *, nchunk):
    first = pl.program_id(1) == 0

    @pl.when(first)
    def _():
        h_ref[...] = jnp.zeros_like(h_ref)

    row = lax.broadcasted_iota(jnp.int32, (PAIR, PAIR), 0)
    col = lax.broadcasted_iota(jnp.int32, (PAIR, PAIR), 1)
    same = (row >> 6) == (col >> 6)
    tr_ = row & (CHUNK - 1)
    tc_ = col & (CHUNK - 1)
    strict = jnp.logical_and(same, tc_ < tr_)
    incl = jnp.logical_and(same, tc_ <= tr_)
    eye = row == col
    m0 = lax.broadcasted_iota(jnp.int32, (CHUNK, PAIR), 1) < HEAD
    tri = (lax.broadcasted_iota(jnp.int32, (CHUNK, CHUNK), 1)
           <= lax.broadcasted_iota(jnp.int32, (CHUNK, CHUNK), 0)).astype(BF16)
    consts = (m0, strict, incl, eye, tri)
    ones = _head_ones(2 * PAIR)
    slab = 4 * PAIR

    def prep_stages(cn):
        at_start = isinstance(cn, int)
        rows = pl.ds(0, CHUNK) if at_start else pl.ds(pl.multiple_of(cn * CHUNK, CHUNK), CHUNK)

        def shift(x_ref, q_ref, m_ref, ln):
            x = x_ref[rows, ln].astype(F32)
            if at_start:
                last = jnp.where(first, 0.0, q_ref[BF16_ROWS - 1:BF16_ROWS, ln].astype(F32))
            else:
                tail = pl.ds(pl.multiple_of(cn * CHUNK - BF16_ROWS, BF16_ROWS), BF16_ROWS)
                last = x_ref[tail, ln][BF16_ROWS - 1:BF16_ROWS, :].astype(F32)
            prev = pltpu.roll(x, 1, axis=0)
            r8 = lax.broadcasted_iota(jnp.int32, (8, x.shape[1]), 0)
            head = jnp.where(r8 == 0, jnp.broadcast_to(last, (8, x.shape[1])), prev[0:8, :])
            prev = jnp.concatenate([head, prev[8:, :]], axis=0)
            return x + (prev - x) * m_ref[:, ln]

        for j in range(D_A // slab):
            ln = slice(j * slab, (j + 1) * slab)
            r_s[rows, ln] = shift(pr_ref, qr_ref, mr_ref, ln)
            v_s[rows, ln] = shift(pv_ref, qv_ref, mv_ref, ln)
            yield
            k = shift(pk_ref, qk_ref, mk_ref, ln)
            kk = k * kk_ref[:, ln]
            kk2 = kk * kk
            ss = jnp.concatenate([_dot(kk2[:, i * 2 * PAIR:(i + 1) * 2 * PAIR], ones)
                                  for i in range(slab // (2 * PAIR))], axis=1)
            yield
            a = jax.nn.sigmoid(a0_ref[:, ln] + ud_s[rows, D_A + j * slab:D_A + (j + 1) * slab])
            kk = kk * lax.rsqrt(jnp.maximum(ss, 1e-24))
            k_s[rows, ln] = k * (1.0 + (a - 1.0) * ka_ref[:, ln])
            al_s[rows, ln] = -kk
            be_s[rows, ln] = kk * a
            lw_s[rows, ln] = -math.exp(-0.5) * jax.nn.sigmoid(w0_ref[:, ln] + ud_s[rows, ln])
            yield

    code = pc_ref[...].astype(F32)
    last = jnp.where(first, 0.0, qc_ref[BF16_ROWS - 1:BF16_ROWS, :].astype(F32))
    prev = pltpu.roll(code, 1, axis=0)
    r8 = lax.broadcasted_iota(jnp.int32, (8, LORA_COLS), 0)
    prev = jnp.concatenate([jnp.where(r8 == 0, jnp.broadcast_to(last, (8, LORA_COLS)), prev[0:8, :]),
                            prev[8:, :]], axis=0)
    code = code + (prev - code) * mc_ref[...]
    lane = lax.broadcasted_iota(jnp.int32, code.shape, 1)
    act = jnp.where(lane < DECAY_LORA, jnp.tanh(code),
                    jnp.where(lane < DECAY_LORA + ICLR_LORA, code, jax.nn.sigmoid(code))).astype(BF16)
    ud_s[:, 0:D_A] = jnp.dot(act, wl_ref[:, 0:D_A], preferred_element_type=F32)
    ud_s[:, D_A:] = jnp.dot(act, wl_ref[:, D_A:2 * D_A], preferred_element_type=F32)
    g_s[...] = jnp.dot(act, wl_ref[:, 2 * D_A:], preferred_element_type=F32)

    for _ in prep_stages(0):
        pass

    def chunk_body(ci, carry):
        rows = pl.ds(pl.multiple_of(ci * CHUNK, CHUNK), CHUNK)
        lanes = [slice(p * PAIR, (p + 1) * PAIR) for p in range(NPAIR)]
        ins = [tuple(s[rows, ln] for s in (r_s, k_s, v_s, al_s, be_s, lw_s)) for ln in lanes]
        hts = [h_ref[p] for p in range(NPAIR)]
        gen = prep_stages(jnp.minimum(ci + 1, nchunk - 1))
        ys, hns = _scan_chunk(ins, hts, consts, lambda: next(gen, None))
        for _ in gen:
            pass
        for p in range(NPAIR):
            h_ref[p] = hns[p]
            y_ref[rows, lanes[p]] = ys[p]
        return carry

    lax.fori_loop(0, nchunk, chunk_body, 0)

    tb = y_ref.shape[0]
    width = 2 * PAIR
    for half in range(2):
        lns = [slice(q * width, (q + 1) * width)
               for q in range(half * NPAIR // 4, (half + 1) * NPAIR // 4)]
        stack = lambda f: jnp.concatenate([f(ln) for ln in lns], axis=0)
        y = stack(lambda ln: y_ref[:, ln])
        sums = _dot(jnp.concatenate(
            [y, stack(lambda ln: r_s[:, ln] * k_s[:, ln] * rk_ref[:, ln])], axis=0), ones)
        yc = y - sums[0:len(lns) * tb] * (1.0 / HEAD)
        rstd = lax.rsqrt(_dot(yc * yc, ones) * (1.0 / HEAD) + LNX_EPS)
        for i, ln in enumerate(lns):
            rs = slice(i * tb, (i + 1) * tb)
            yn = yc[rs] * rstd[rs] * lg_ref[:, ln] + lb_ref[:, ln]
            rk = sums[len(lns) * tb + i * tb:len(lns) * tb + (i + 1) * tb]
            o_ref[:, ln] = ((yn + rk * v_s[:, ln]) * g_s[:, ln]).astype(o_ref.dtype)


def _rwkv(p, mu, w0, a0, k_k, k_a, w_lora, r_k, lnx_g, lnx_b, bsz, seq, tb=256):
    t = p.shape[0]
    per = seq // tb
    nchunk = tb // CHUNK
    assert nchunk >= 2
    cblk = OFF_W // LORA_COLS
    tail = tb // BF16_ROWS

    def cur(j):
        return pl.BlockSpec((tb, D_A), lambda b, s: (b * per + s, j))

    def prev(j):
        return pl.BlockSpec((BF16_ROWS, D_A),
                            lambda b, s: (jnp.maximum((b * per + s) * tail - 1, 0), j))

    def vec(j):
        return pl.BlockSpec((1, D_A), lambda b, s: (0, j))

    in_specs = [cur(0), cur(1), cur(2),
                pl.BlockSpec((tb, LORA_COLS), lambda b, s: (b * per + s, cblk)),
                prev(0), prev(1), prev(2),
                pl.BlockSpec((BF16_ROWS, LORA_COLS),
                             lambda b, s: (jnp.maximum((b * per + s) * tail - 1, 0), cblk)),
                vec(0), vec(1), vec(2),
                pl.BlockSpec((1, LORA_COLS), lambda b, s: (0, cblk)),
                vec(0), vec(0), vec(0), vec(0),
                pl.BlockSpec((LORA_COLS, 3 * D_A), lambda b, s: (0, 0)),
                vec(0), vec(0), vec(0)]
    blk = (tb, D_A)
    return pl.pallas_call(
        functools.partial(_rwkv_kernel, nchunk=nchunk),
        grid=(bsz, per),
        in_specs=in_specs,
        out_specs=pl.BlockSpec(blk, lambda b, s: (b * per + s, 0)),
        out_shape=jax.ShapeDtypeStruct((t, D_A), BF16),
        scratch_shapes=([pltpu.VMEM((NPAIR, PAIR, PAIR), F32)] + [pltpu.VMEM(blk, F32)] * 8
                        + [pltpu.VMEM((tb, 2 * D_A), F32)]),
        compiler_params=_cparams(("parallel", "arbitrary"), 52),
        name="rwkv",
    )(p, p, p, p, p, p, p, p, mu, mu, mu, mu, w0, a0, k_k, k_a, w_lora, r_k, lnx_g, lnx_b)


def _bucket_table():
    qi = np.arange(BLOCK)[:, None]
    kj = np.arange(BLOCK)[None, :]
    n = np.where(kj > qi, qi + BLOCK - kj, qi - kj)
    assert WINDOW == BLOCK and n.min() >= 0 and n.max() < WINDOW
    nf = np.maximum(n, 1).astype(np.float32)
    large = RPB_MAX_EXACT + (np.log(nf / np.float32(RPB_MAX_EXACT))
                             / np.float32(math.log(RPB_MAX_DIST / RPB_MAX_EXACT))
                             * np.float32(RPB_BUCKETS - RPB_MAX_EXACT)).astype(np.int32)
    large = np.minimum(large, RPB_BUCKETS - 1)
    return np.where(n < RPB_MAX_EXACT, n, large).astype(np.int32)


def _bias_kernel(tab_ref, bkt_ref, o_ref):
    h0 = pl.program_id(0) * GQA
    bkt = bkt_ref[...]
    from_prev = (lax.broadcasted_iota(jnp.int32, bkt.shape, 1)
                 > lax.broadcasted_iota(jnp.int32, bkt.shape, 0))
    for i in range(GQA):
        acc = jnp.zeros(bkt.shape, F32)
        for b in range(RPB_BUCKETS):
            acc = jnp.where(bkt == b, tab_ref[b, h0 + i], acc)
        o_ref[0, i] = acc
        o_ref[1, i] = jnp.where(from_prev, NEG, acc)


def _attn_bias(rpb_table):
    bkt = jnp.asarray(_bucket_table())
    return pl.pallas_call(
        _bias_kernel,
        grid=(H_KV,),
        in_specs=[pl.BlockSpec(memory_space=pltpu.SMEM),
                  pl.BlockSpec((BLOCK, BLOCK), lambda g: (0, 0))],
        out_specs=pl.BlockSpec((2, GQA, BLOCK, BLOCK), lambda g: (0, g, 0, 0)),
        out_shape=jax.ShapeDtypeStruct((2, H_Q, BLOCK, BLOCK), F32),
        compiler_params=_cparams(("arbitrary",), 16),
        name="attn_bias",
    )(rpb_table, bkt)


def _swa_kernel(sink_ref, *refs):
    *q_refs, kc_ref, kp_ref, vc_ref, vp_ref, bias_ref, o_ref = refs
    each = lambda f, *ls: [f(*a) for a in zip(*ls)]
    lo = lax.broadcasted_iota(jnp.int32, (BLOCK, PAIR), 1) < HEAD
    from_prev = (lax.broadcasted_iota(jnp.int32, (BLOCK, BLOCK), 1)
                 > lax.broadcasted_iota(jnp.int32, (BLOCK, BLOCK), 0))
    from_prev2 = jnp.concatenate([from_prev, from_prev], axis=0)

    def exps(s_, m_):
        parts = []
        for sh, mh in zip(s_, m_):
            e_ = jnp.exp(sh - mh)
            parts += [jnp.where(from_prev, e_, 0.0), jnp.where(from_prev, 0.0, e_)]
        return jnp.concatenate(parts, axis=1).astype(BF16)

    scale = HEAD ** -0.5
    zeros = jnp.zeros((2 * BLOCK, HEAD), BF16)
    ones = jnp.ones((2 * BLOCK, HEAD), BF16)
    npr = GQA // 2
    kcat = jnp.concatenate([kp_ref[...], kc_ref[...]], axis=0).astype(F32) * scale
    vcat = jnp.concatenate([vp_ref[...], vc_ref[...]], axis=0).astype(F32)
    for g in range(H_KV):
        gsl = slice(g * HEAD, (g + 1) * HEAD)
        kg = kcat[:, gsl].astype(BF16)
        vg = vcat[:, gsl].astype(BF16)
        kdup = jnp.concatenate([kg, kg], axis=1)
        rhs = jnp.concatenate([jnp.concatenate([vg, zeros, ones, zeros], axis=1),
                               jnp.concatenate([zeros, vg, zeros, ones], axis=1)], axis=0)
        heads = [g * GQA + 2 * i for i in range(npr)]
        lanes = [slice(h * HEAD, (h + 2) * HEAD) for h in heads]
        qp = [q_refs[h // 4][:, (h % 4) * HEAD:(h % 4 + 2) * HEAD] for h in heads]
        s2 = each(lambda q_: _dot_nt(jnp.concatenate([jnp.where(lo, q_, 0.0).astype(BF16),
                                                      jnp.where(lo, 0.0, q_).astype(BF16)], axis=0),
                                     kdup), qp)
        sf = each(lambda s_: jnp.where(from_prev2, s_[:, 0:BLOCK], s_[:, BLOCK:]), s2)
        s = [(s_[0:BLOCK] + bias_ref[h], s_[BLOCK:] + bias_ref[h + 1]) for s_, h in zip(sf, heads)]
        m = [(jnp.maximum(jnp.max(a, axis=-1, keepdims=True), sink_ref[0, h]),
              jnp.maximum(jnp.max(b, axis=-1, keepdims=True), sink_ref[0, h + 1]))
             for (a, b), h in zip(s, heads)]
        e = each(exps, s, m)
        od = each(lambda e_: jnp.dot(e_, rhs, preferred_element_type=F32), e)
        for o_, m_, h, ln in zip(od, m, heads, lanes):
            den = o_[:, PAIR:] + jnp.where(lo, jnp.exp(sink_ref[0, h] - m_[0]),
                                           jnp.exp(sink_ref[0, h + 1] - m_[1]))
            o_ref[:, ln] = (o_[:, 0:PAIR] / den).astype(o_ref.dtype)


def _swa(p, bias, sinks, bsz, seq):
    t = p.shape[0]
    nb = seq // BLOCK
    kvw = H_KV * HEAD
    nq = D_B // kvw

    def cur(c):
        return pl.BlockSpec((BLOCK, kvw), lambda b, n: (b * nb + n, c))

    def prev(c):
        return pl.BlockSpec((BLOCK, kvw), lambda b, n: (b * nb + jnp.maximum(n - 1, 0), c))

    kb, vb = OFF_KB // kvw, OFF_VB // kvw
    return pl.pallas_call(
        _swa_kernel,
        grid=(bsz, nb),
        in_specs=[pl.BlockSpec(memory_space=pltpu.SMEM)]
        + [cur(OFF_Q // kvw + i) for i in range(nq)]
        + [cur(kb), prev(kb), cur(vb), prev(vb),
           pl.BlockSpec((None, H_Q, BLOCK, BLOCK),
                        lambda b, n: (jnp.where(n == 0, 1, 0), 0, 0, 0))],
        out_specs=pl.BlockSpec((BLOCK, D_B), lambda b, n: (b * nb + n, 0)),
        out_shape=jax.ShapeDtypeStruct((t, D_B), BF16),
        compiler_params=_cparams(("parallel", "arbitrary"), 32),
        name="swa",
    )(sinks, *([p] * (nq + 4)), bias)


def _post_mix_kernel(x_ref, mu_ref, rstd_ref, mix_ref, ge_ref, be_ref, g1_ref, b1_ref, mod_ref,
                     x1_ref, u_ref):
    wide = lambda s_: jnp.concatenate([s_] * (x_ref.shape[1] // LANES), axis=1)
    axn = ((x_ref[...] - wide(mu_ref[...])) * wide(rstd_ref[...]) * (ALPHA * ge_ref[...])
           + ALPHA * be_ref[...])
    z = axn + (1.0 + mod_ref[2:3, :]) * mix_ref[...].astype(F32)
    x1 = _layer_norm(z, g1_ref[...], b1_ref[...])
    x1_ref[...] = x1
    u_ref[...] = (x1 * (1.0 + mod_ref[4:5, :]) + mod_ref[3:4, :]).astype(u_ref.dtype)


def _post_mix(x2, mu, rstd, mix, ge, be, g1, b1, mod, seq, tr=256):
    t, d = x2.shape
    per = seq // tr
    row = pl.BlockSpec((tr, d), lambda i: (i, 0))
    stat = pl.BlockSpec((tr, LANES), lambda i: (i, 0))
    vec = pl.BlockSpec((1, d), lambda i: (0, 0))
    return pl.pallas_call(
        _post_mix_kernel,
        grid=(t // tr,),
        in_specs=[row, stat, stat, row, vec, vec, vec, vec,
                  pl.BlockSpec((None, 6, d), lambda i: (i // per, 0, 0))],
        out_specs=[row, row],
        out_shape=[jax.ShapeDtypeStruct((t, d), F32), jax.ShapeDtypeStruct((t, d), BF16)],
        compiler_params=_cparams(("parallel",), 52),
        name="post_mix",
    )(x2, mu, rstd, mix, ge, be, g1, b1, mod)


def _final_kernel(x1_ref, h_ref, g2_ref, b2_ref, mod_ref, o_ref):
    z = ALPHA * x1_ref[...] + (1.0 + mod_ref[5:6, :]) * h_ref[...].astype(F32)
    o_ref[...] = _layer_norm(z, g2_ref[...], b2_ref[...])


def _final(x1, h, g2, b2, mod, seq, tr=256):
    t, d = x1.shape
    per = seq // tr
    row = pl.BlockSpec((tr, d), lambda i: (i, 0))
    vec = pl.BlockSpec((1, d), lambda i: (0, 0))
    return pl.pallas_call(
        _final_kernel,
        grid=(t // tr,),
        in_specs=[row, row, vec, vec, pl.BlockSpec((None, 6, d), lambda i: (i // per, 0, 0))],
        out_specs=row,
        out_shape=jax.ShapeDtypeStruct((t, d), F32),
        compiler_params=_cparams(("parallel",), 48),
        name="final_ln",
    )(x1, h, g2, b2, mod)


def _lora_weights(w_decay_up, w_iclr_up, w_gate_up):
    zd = jnp.zeros((LORA_COLS, D_A), F32)
    wd = zd.at[0:DECAY_LORA].set(w_decay_up)
    wa = zd.at[DECAY_LORA:DECAY_LORA + ICLR_LORA].set(w_iclr_up)
    wg = zd.at[DECAY_LORA + ICLR_LORA:].set(w_gate_up)
    return jnp.concatenate([wd, wa, wg], axis=1).astype(BF16)


def kernel(x, c, ln_emb_g, ln_emb_b, rpb_table, w_mod, b_mod, w_in, mu_shift, w0, w_decay_up, a0,
           w_iclr_up, w_gate_up, k_k, k_a, r_k, lnx_g, lnx_b, attn_sinks, w_out, ln1_g, ln1_b,
           w_up, w_down, ln2_g, ln2_b):
    bsz, seq, d = x.shape
    assert w_mod.shape[0] == DEPTH == 1 and d == D_MODEL and bsz <= 8
    t = bsz * seq
    row = lambda a: a.reshape(1, -1)
    x2 = x.reshape(t, d)
    c8 = jnp.pad(c, ((0, 8 - bsz), (0, 0)))
    bias = _attn_bias(rpb_table)
    mod = _modulation(c8, w_mod[0], row(b_mod[0]))[:bsz].reshape(bsz, 6, d)
    u1, mu, rstd = _ln_mod(x2, row(ln_emb_g), row(ln_emb_b), mod, seq)
    p = _matmul(u1, w_in[0].astype(BF16), tm=1024, tn=1280, out_dtype=BF16, name="in_proj")
    w_lora = _lora_weights(w_decay_up[0], w_iclr_up[0], w_gate_up[0])
    y_a = _rwkv(p, row(mu_shift[0]), row(w0[0]), row(a0[0]), row(k_k[0]), row(k_a[0]), w_lora,
                row(r_k[0]), row(lnx_g[0]), row(lnx_b[0]), bsz, seq)
    y_b = _swa(p, bias, row(attn_sinks[0]), bsz, seq)
    mix = _matmul([y_a, y_b], w_out[0], tm=1024, tn=512, out_dtype=BF16, name="out_proj")
    x1, u2 = _post_mix(x2, mu, rstd, mix, row(ln_emb_g), row(ln_emb_b), row(ln1_g[0]),
                       row(ln1_b[0]), mod, seq)
    hmid, wd = _matmul(u2, w_up[0], tm=1024, tn=512, out_dtype=BF16, relu2=True,
                       convert=w_down[0], name="mlp_up")
    hout = _matmul(hmid, wd, tm=1024, tn=1024, tk=4096, out_dtype=BF16, name="mlp_down")
    out = _final(x1, hout, row(ln2_g[0]), row(ln2_b[0]), mod, seq)
    return out.reshape(bsz, seq, d)
```

```python
import functools
import math

import numpy as np
import jax
import jax.numpy as jnp
from jax import lax
from jax.experimental import pallas as pl
from jax.experimental.pallas import tpu as pltpu

F32 = jnp.float32
BF16 = jnp.bfloat16

D_MODEL = 4096
HEAD = 64
D_A = D_MODEL // 2
D_B = D_MODEL - D_A
H_A = D_A // HEAD
H_Q = D_B // HEAD
GQA = 8
H_KV = H_Q // GQA
WINDOW = 128
BLOCK = 128
RPB_BUCKETS = 32
RPB_MAX_EXACT = RPB_BUCKETS // 2
RPB_MAX_DIST = 128
DECAY_LORA = max(32, int(round(D_A ** 0.5 * 1.8 / 32)) * 32)
ICLR_LORA = max(32, int(round(D_A ** 0.5 * 1.8 / 32)) * 32)
GATE_LORA = max(32, int(round(D_A ** 0.6 * 0.8 / 32)) * 32)
LORA_COLS = DECAY_LORA + ICLR_LORA + GATE_LORA
D_FF = 4 * D_MODEL
DEPTH = 1
ALPHA = (2.0 * DEPTH) ** 0.25
LN_EPS = 1e-5
LNX_EPS = 64e-5
OFF_W = 3 * D_A
RWKV_COLS = OFF_W + LORA_COLS
OFF_Q = RWKV_COLS
OFF_KB = OFF_Q + D_B
OFF_VB = OFF_KB + H_KV * HEAD
N_IN = OFF_VB + H_KV * HEAD
NEG = -1e30

CHUNK = 64
PAIR = 2 * HEAD
NPAIR = H_A // 2
BF16_ROWS = 16
LANES = 128
VMEM_CAP = 56 * 1024 * 1024


def _cparams(sem, vmem_mb):
    return pltpu.CompilerParams(dimension_semantics=sem,
                                vmem_limit_bytes=min(int(vmem_mb * 1024 * 1024), VMEM_CAP))


def _dot(a, b):
    return jnp.dot(a.astype(BF16), b.astype(BF16), preferred_element_type=F32)


def _dot_nt(a, b):
    return lax.dot_general(a.astype(BF16), b.astype(BF16), (((1,), (1,)), ((), ())),
                           preferred_element_type=F32)


def _split2(x):
    hi = x.astype(BF16)
    lo = (x - hi.astype(F32)).astype(BF16)
    return hi, lo


def _layer_norm(x, g, b):
    mu = jnp.mean(x, axis=-1, keepdims=True)
    xc = x - mu
    var = jnp.mean(xc * xc, axis=-1, keepdims=True)
    return xc * lax.rsqrt(var + LN_EPS) * g + b


def _ln_rows(src_ref, consume):
    groups = [slice(g * BF16_ROWS, (g + 1) * BF16_ROWS) for g in range(src_ref.shape[0] // BF16_ROWS)]
    mus = [jnp.mean(src_ref[rs, :], axis=-1, keepdims=True) for rs in groups]
    rstd = [lax.rsqrt(jnp.mean(jnp.square(src_ref[rs, :] - mu), axis=-1, keepdims=True) + LN_EPS)
            for rs, mu in zip(groups, mus)]
    for rs, mu, r in zip(groups, mus, rstd):
        consume(rs, (src_ref[rs, :] - mu) * r, mu, r)


def _head_ones(n):
    r = lax.broadcasted_iota(jnp.int32, (n, n), 0)
    c = lax.broadcasted_iota(jnp.int32, (n, n), 1)
    return ((r >> 6) == (c >> 6)).astype(BF16)


def _mod_kernel(c_ref, w_ref, b_ref, o_ref):
    c = c_ref[...]
    cond = c * jax.nn.sigmoid(c)
    ch, cl = _split2(cond)
    wh, wl = _split2(w_ref[...])
    rows = c.shape[0]
    both = jnp.dot(jnp.concatenate([ch, cl], axis=0), wh, preferred_element_type=F32)
    o_ref[...] = (both[0:rows] + both[rows:] + jnp.dot(ch, wl, preferred_element_type=F32)
                  + b_ref[...])


def _modulation(c8, w_mod, b_mod, tn=1024):
    d, n = w_mod.shape
    return pl.pallas_call(
        _mod_kernel,
        grid=(n // tn,),
        in_specs=[pl.BlockSpec((8, d), lambda j: (0, 0)),
                  pl.BlockSpec((d, tn), lambda j: (0, j)),
                  pl.BlockSpec((1, tn), lambda j: (0, j))],
        out_specs=pl.BlockSpec((8, tn), lambda j: (0, j)),
        out_shape=jax.ShapeDtypeStruct((8, n), F32),
        compiler_params=_cparams(("parallel",), 56),
        name="modulation",
    )(c8, w_mod, b_mod)


def _ln_mod_kernel(x_ref, g_ref, b_ref, mod_ref, u_ref, mu_ref, rstd_ref):
    gain = 1.0 + mod_ref[1:2, :]
    scale = g_ref[...] * gain
    shift = b_ref[...] * gain + mod_ref[0:1, :]

    def consume(rs, xh, mu, rstd):
        u_ref[rs, :] = (xh * scale + shift).astype(u_ref.dtype)
        mu_ref[rs, :] = jnp.broadcast_to(mu, (BF16_ROWS, LANES))
        rstd_ref[rs, :] = jnp.broadcast_to(rstd, (BF16_ROWS, LANES))

    _ln_rows(x_ref, consume)


def _ln_mod(x2, g, b, mod, seq, tr=512):
    t, d = x2.shape
    per = seq // tr
    stat = pl.BlockSpec((tr, LANES), lambda i: (i, 0))
    return pl.pallas_call(
        _ln_mod_kernel,
        grid=(t // tr,),
        in_specs=[pl.BlockSpec((tr, d), lambda i: (i, 0)),
                  pl.BlockSpec((1, d), lambda i: (0, 0)),
                  pl.BlockSpec((1, d), lambda i: (0, 0)),
                  pl.BlockSpec((None, 6, d), lambda i: (i // per, 0, 0))],
        out_specs=[pl.BlockSpec((tr, d), lambda i: (i, 0)), stat, stat],
        out_shape=[jax.ShapeDtypeStruct((t, d), BF16), jax.ShapeDtypeStruct((t, LANES), F32),
                   jax.ShapeDtypeStruct((t, LANES), F32)],
        compiler_params=_cparams(("parallel",), 48),
        name="ln_mod",
    )(x2, g, b, mod)


def _mm_kernel(*refs, relu2, convert):
    if convert:
        *refs, ci_ref, o_ref, co_ref = refs
        co_ref[...] = ci_ref[...].astype(co_ref.dtype)
        refs = (*refs, o_ref)
    *a_refs, b_ref, o_ref = refs
    acc, off = None, 0
    for a_ref in a_refs:
        kd = a_ref.shape[1]
        part = jnp.dot(a_ref[...], b_ref[off:off + kd, :].astype(BF16), preferred_element_type=F32)
        acc = part if acc is None else acc + part
        off += kd
    if relu2:
        acc = jnp.square(jnp.maximum(acc, 0.0))
    o_ref[...] = acc.astype(o_ref.dtype)


def _mm_acc_kernel(a_ref, b_ref, o_ref, acc_ref):
    k = pl.program_id(2)

    @pl.when(k == 0)
    def _():
        acc_ref[...] = jnp.zeros_like(acc_ref)

    acc_ref[...] += jnp.dot(a_ref[...], b_ref[...].astype(BF16), preferred_element_type=F32)

    @pl.when(k == pl.num_programs(2) - 1)
    def _():
        o_ref[...] = acc_ref[...].astype(o_ref.dtype)


def _matmul(a, b, *, tm, tn, tk=None, out_dtype=F32, relu2=False, convert=None, a_buffers=2,
            name="matmul"):
    a = a if isinstance(a, (list, tuple)) else [a]
    m = a[0].shape[0]
    kd, n = b.shape
    tm = min(tm, m)
    nj = n // tn
    osz = jnp.dtype(out_dtype).itemsize
    bsz = jnp.dtype(b.dtype).itemsize
    bsz = 2 * bsz + (2 if bsz == 4 else 0)
    if tk is None or tk >= kd:
        vm = (a_buffers * tm * kd * 2 + kd * tn * bsz + 2 * tm * tn * osz + tm * tn * 4) / 2 ** 20 + 8
        in_specs = ([pl.BlockSpec((tm, x.shape[1]), lambda i, j: (i, 0),
                                  pipeline_mode=pl.Buffered(a_buffers)) for x in a]
                    + [pl.BlockSpec((kd, tn), lambda i, j: (0, j))])
        out_specs = pl.BlockSpec((tm, tn), lambda i, j: (i, j))
        out_shape = jax.ShapeDtypeStruct((m, n), out_dtype)
        args = (*a, b)
        if convert is not None:
            cr, cc = convert.shape
            slab = cr // ((m // tm) * nj)
            assert slab * (m // tm) * nj == cr and slab % BF16_ROWS == 0
            cspec = pl.BlockSpec((slab, cc), lambda i, j: (i * nj + j, 0))
            in_specs, args = in_specs + [cspec], (*args, convert)
            out_specs = [out_specs, cspec]
            out_shape = [out_shape, jax.ShapeDtypeStruct((cr, cc), BF16)]
            vm += slab * cc * 12 / 2 ** 20
        return pl.pallas_call(
            functools.partial(_mm_kernel, relu2=relu2, convert=convert is not None),
            grid=(m // tm, nj),
            in_specs=in_specs,
            out_specs=out_specs,
            out_shape=out_shape,
            compiler_params=_cparams(("parallel", "parallel"), vm),
            name=name,
        )(*args)
    assert not relu2 and len(a) == 1 and convert is None
    vm = (2 * tm * tk * 2 + tk * tn * bsz + 2 * tm * tn * osz + 2 * tm * tn * 4) / 2 ** 20 + 8
    return pl.pallas_call(
        _mm_acc_kernel,
        grid=(m // tm, nj, kd // tk),
        in_specs=[pl.BlockSpec((tm, tk), lambda i, j, k: (i, k)),
                  pl.BlockSpec((tk, tn), lambda i, j, k: (k, j))],
        out_specs=pl.BlockSpec((tm, tn), lambda i, j, k: (i, j)),
        out_shape=jax.ShapeDtypeStruct((m, n), out_dtype),
        scratch_shapes=[pltpu.VMEM((tm, tn), F32)],
        compiler_params=_cparams(("parallel", "parallel", "arbitrary"), vm),
        name=name,
    )(a[0], b)


def _scan_chunk(ins, hts, consts, tick):
    m0, strict, incl, eye, tri = consts

    ticking = [False]

    def each(f, *ls):
        out = [f(*a) for a in zip(*ls)]
        if ticking[0]:
            tick()
        return out

    r, k, v, al, be, lw = (list(z) for z in zip(*ins))

    def cumsum(x):
        hi = x.astype(BF16)
        rem = x - hi.astype(F32)
        mid = rem.astype(BF16)
        lo = (rem - mid.astype(F32)).astype(BF16)
        c3 = jnp.dot(tri, jnp.concatenate([hi, mid, lo], axis=1), preferred_element_type=F32)
        return c3[:, 0:PAIR] + c3[:, PAIR:2 * PAIR] + c3[:, 2 * PAIR:3 * PAIR]

    def sm(x):
        return jnp.concatenate([jnp.where(m0, x, 0.0), jnp.where(m0, 0.0, x)], axis=0)

    b16 = lambda t_: t_.astype(BF16)
    nn = lambda a_, b_: jnp.dot(a_, b_, preferred_element_type=F32)
    c = each(cumsum, lw)
    pc = each(lambda c_: jnp.exp(c_[CHUNK - 1:CHUNK, :]), c)
    einv = each(lambda c_: jnp.exp(-c_), c)
    a_sm = each(lambda a_, c_, l_: b16(sm(a_ * jnp.exp(c_ - l_))), al, c, lw)
    r_sm = each(lambda r_, c_: sm(r_ * jnp.exp(c_)), r, c)
    v_sm = each(lambda v_: b16(sm(v_)), v)
    b_t = each(lambda b_, e_: b_ * e_, be, einv)
    k_t = each(lambda k_, e_: k_ * e_, k, einv)

    def scores(a_, r_, b_, k_):
        bb, kb = b16(b_), b16(k_)
        return _dot_nt(jnp.concatenate([a_, b16(r_)], axis=0),
                       jnp.concatenate([bb, bb, kb, kb], axis=0))

    s = each(scores, a_sm, r_sm, b_t, k_t)
    lab = each(lambda s_: jnp.where(strict, s_[0:PAIR, 0:PAIR], 0.0), s)
    mak = each(lambda s_: b16(jnp.where(strict, s_[0:PAIR, PAIR:], 0.0)), s)
    incl2 = jnp.concatenate([incl, incl], axis=1)
    mrbk = each(lambda s_: b16(jnp.where(incl2, s_[PAIR:, :], 0.0)), s)

    ticking[0] = True
    ldt = each(lambda l_: l_.T, lab)
    xt = each(lambda l_: jnp.where(eye, 1.0, l_), ldt)
    lt = each(lambda l_: nn(b16(l_), b16(l_)), ldt)

    def series_step(l_, x_):
        lb = b16(l_)
        return nn(lb, jnp.concatenate([b16(x_), lb], axis=1))

    for _ in range(4):
        xl = each(series_step, lt, xt)
        xt = each(lambda x_, p_: x_ + p_[:, 0:PAIR], xt, xl)
        lt = each(lambda p_: p_[:, PAIR:], xl)
    x = each(lambda x_, l_: b16((x_ + nn(b16(l_), b16(x_))).T), xt, lt)

    makv = each(nn, mak, v_sm)
    wu = each(lambda x_, a_, m_: nn(x_, jnp.concatenate([a_, b16(m_)], axis=1)), x, a_sm, makv)
    bigr = each(lambda wu_, v_: jnp.concatenate(
        [b16(wu_), jnp.concatenate([jnp.zeros_like(v_), v_], axis=1)], axis=0), wu, v_sm)
    bk = each(lambda b_, k_, p_: b16(jnp.concatenate([sm(b_ * p_), sm(k_ * p_)], axis=0).T),
              b_t, k_t, pc)
    gz = each(nn, bk, bigr)
    qy = each(nn, mrbk, bigr)
    hb = each(b16, hts)

    def new_state(ht, h_, g_, p_):
        return ht * p_ + _dot_nt(h_, g_[:, 0:PAIR]) + g_[:, PAIR:].T

    def output(r_, q_, h_):
        y_sm = _dot_nt(r_ + q_[:, 0:PAIR], h_) + q_[:, PAIR:]
        return y_sm[0:CHUNK, :] + y_sm[CHUNK:, :]

    return each(output, r_sm, qy, hb), each(new_state, hts, hb, gz, pc)


def _rwkv_kernel(pr_ref, pk_ref, pv_ref, pc_ref, qr_ref, qk_ref, qv_ref, qc_ref,
                 mr_ref, mk_ref, mv_ref, mc_ref, w0_ref, a0_ref, kk_ref, ka_ref, wl_ref,
                 rk_ref, lg_ref, lb_ref, o_ref,
                 h_ref, y_ref, r_s, k_s, v_s, al_s, be_s, lw_s, g_s, ud_s, *, nchunk):
    first = pl.program_id(1) == 0

    @pl.when(first)
    def _():
        h_ref[...] = jnp.zeros_like(h_ref)

    row = lax.broadcasted_iota(jnp.int32, (PAIR, PAIR), 0)
    col = lax.broadcasted_iota(jnp.int32, (PAIR, PAIR), 1)
    same = (row >> 6) == (col >> 6)
    tr_ = row & (CHUNK - 1)
    tc_ = col & (CHUNK - 1)
    strict = jnp.logical_and(same, tc_ < tr_)
    incl = jnp.logical_and(same, tc_ <= tr_)
    eye = row == col
    m0 = lax.broadcasted_iota(jnp.int32, (CHUNK, PAIR), 1) < HEAD
    tri = (lax.broadcasted_iota(jnp.int32, (CHUNK, CHUNK), 1)
           <= lax.broadcasted_iota(jnp.int32, (CHUNK, CHUNK), 0)).astype(BF16)
    consts = (m0, strict, incl, eye, tri)
    ones = _head_ones(2 * PAIR)
    slab = 4 * PAIR

    def prep_stages(cn):
        at_start = isinstance(cn, int)
        rows = pl.ds(0, CHUNK) if at_start else pl.ds(pl.multiple_of(cn * CHUNK, CHUNK), CHUNK)

        def shift(x_ref, q_ref, m_ref, ln):
            x = x_ref[rows, ln].astype(F32)
            if at_start:
                last = jnp.where(first, 0.0, q_ref[BF16_ROWS - 1:BF16_ROWS, ln].astype(F32))
            else:
                tail = pl.ds(pl.multiple_of(cn * CHUNK - BF16_ROWS, BF16_ROWS), BF16_ROWS)
                last = x_ref[tail, ln][BF16_ROWS - 1:BF16_ROWS, :].astype(F32)
            prev = pltpu.roll(x, 1, axis=0)
            r8 = lax.broadcasted_iota(jnp.int32, (8, x.shape[1]), 0)
            head = jnp.where(r8 == 0, jnp.broadcast_to(last, (8, x.shape[1])), prev[0:8, :])
            prev = jnp.concatenate([head, prev[8:, :]], axis=0)
            return x + (prev - x) * m_ref[:, ln]

        for j in range(D_A // slab):
            ln = slice(j * slab, (j + 1) * slab)
            r_s[rows, ln] = shift(pr_ref, qr_ref, mr_ref, ln)
            v_s[rows, ln] = shift(pv_ref, qv_ref, mv_ref, ln)
            yield
            k = shift(pk_ref, qk_ref, mk_ref, ln)
            kk = k * kk_ref[:, ln]
            kk2 = kk * kk
            ss = jnp.concatenate([_dot(kk2[:, i * 2 * PAIR:(i + 1) * 2 * PAIR], ones)
                                  for i in range(slab // (2 * PAIR))], axis=1)
            yield
            a = jax.nn.sigmoid(a0_ref[:, ln] + ud_s[rows, D_A + j * slab:D_A + (j + 1) * slab])
            kk = kk * lax.rsqrt(jnp.maximum(ss, 1e-24))
            k_s[rows, ln] = k * (1.0 + (a - 1.0) * ka_ref[:, ln])
            al_s[rows, ln] = -kk
            be_s[rows, ln] = kk * a
            lw_s[rows, ln] = -math.exp(-0.5) * jax.nn.sigmoid(w0_ref[:, ln] + ud_s[rows, ln])
            yield

    code = pc_ref[...].astype(F32)
    last = jnp.where(first, 0.0, qc_ref[BF16_ROWS - 1:BF16_ROWS, :].astype(F32))
    prev = pltpu.roll(code, 1, axis=0)
    r8 = lax.broadcasted_iota(jnp.int32, (8, LORA_COLS), 0)
    prev = jnp.concatenate([jnp.where(r8 == 0, jnp.broadcast_to(last, (8, LORA_COLS)), prev[0:8, :]),
                            prev[8:, :]], axis=0)
    code = code + (prev - code) * mc_ref[...]
    lane = lax.broadcasted_iota(jnp.int32, code.shape, 1)
    act = jnp.where(lane < DECAY_LORA, jnp.tanh(code),
                    jnp.where(lane < DECAY_LORA + ICLR_LORA, code, jax.nn.sigmoid(code))).astype(BF16)
    ud_s[:, 0:D_A] = jnp.dot(act, wl_ref[:, 0:D_A], preferred_element_type=F32)
    ud_s[:, D_A:] = jnp.dot(act, wl_ref[:, D_A:2 * D_A], preferred_element_type=F32)
    g_s[...] = jnp.dot(act, wl_ref[:, 2 * D_A:], preferred_element_type=F32)

    for _ in prep_stages(0):
        pass

    def chunk_body(ci, carry):
        rows = pl.ds(pl.multiple_of(ci * CHUNK, CHUNK), CHUNK)
        lanes = [slice(p * PAIR, (p + 1) * PAIR) for p in range(NPAIR)]
        ins = [tuple(s[rows, ln] for s in (r_s, k_s, v_s, al_s, be_s, lw_s)) for ln in lanes]
        hts = [h_ref[p] for p in range(NPAIR)]
        gen = prep_stages(jnp.minimum(ci + 1, nchunk - 1))
        ys, hns = _scan_chunk(ins, hts, consts, lambda: next(gen, None))
        for _ in gen:
            pass
        for p in range(NPAIR):
            h_ref[p] = hns[p]
            y_ref[rows, lanes[p]] = ys[p]
        return carry

    lax.fori_loop(0, nchunk, chunk_body, 0)

    tb = y_ref.shape[0]
    width = 2 * PAIR
    for half in range(2):
        lns = [slice(q * width, (q + 1) * width)
               for q in range(half * NPAIR // 4, (half + 1) * NPAIR // 4)]
        stack = lambda f: jnp.concatenate([f(ln) for ln in lns], axis=0)
        y = stack(lambda ln: y_ref[:, ln])
        sums = _dot(jnp.concatenate(
            [y, stack(lambda ln: r_s[:, ln] * k_s[:, ln] * rk_ref[:, ln])], axis=0), ones)
        yc = y - sums[0:len(lns) * tb] * (1.0 / HEAD)
        rstd = lax.rsqrt(_dot(yc * yc, ones) * (1.0 / HEAD) + LNX_EPS)
        for i, ln in enumerate(lns):
            rs = slice(i * tb, (i + 1) * tb)
            yn = yc[rs] * rstd[rs] * lg_ref[:, ln] + lb_ref[:, ln]
            rk = sums[len(lns) * tb + i * tb:len(lns) * tb + (i + 1) * tb]
            o_ref[:, ln] = ((yn + rk * v_s[:, ln]) * g_s[:, ln]).astype(o_ref.dtype)


def _rwkv(p, mu, w0, a0, k_k, k_a, w_lora, r_k, lnx_g, lnx_b, bsz, seq, tb=256):
    t = p.shape[0]
    per = seq // tb
    nchunk = tb // CHUNK
    assert nchunk >= 2
    cblk = OFF_W // LORA_COLS
    tail = tb // BF16_ROWS

    def cur(j):
        return pl.BlockSpec((tb, D_A), lambda b, s: (b * per + s, j))

    def prev(j):
        return pl.BlockSpec((BF16_ROWS, D_A),
                            lambda b, s: (jnp.maximum((b * per + s) * tail - 1, 0), j))

    def vec(j):
        return pl.BlockSpec((1, D_A), lambda b, s: (0, j))

    in_specs = [cur(0), cur(1), cur(2),
                pl.BlockSpec((tb, LORA_COLS), lambda b, s: (b * per + s, cblk)),
                prev(0), prev(1), prev(2),
                pl.BlockSpec((BF16_ROWS, LORA_COLS),
                             lambda b, s: (jnp.maximum((b * per + s) * tail - 1, 0), cblk)),
                vec(0), vec(1), vec(2),
                pl.BlockSpec((1, LORA_COLS), lambda b, s: (0, cblk)),
                vec(0), vec(0), vec(0), vec(0),
                pl.BlockSpec((LORA_COLS, 3 * D_A), lambda b, s: (0, 0)),
                vec(0), vec(0), vec(0)]
    blk = (tb, D_A)
    return pl.pallas_call(
        functools.partial(_rwkv_kernel, nchunk=nchunk),
        grid=(bsz, per),
        in_specs=in_specs,
        out_specs=pl.BlockSpec(blk, lambda b, s: (b * per + s, 0)),
        out_shape=jax.ShapeDtypeStruct((t, D_A), BF16),
        scratch_shapes=([pltpu.VMEM((NPAIR, PAIR, PAIR), F32)] + [pltpu.VMEM(blk, F32)] * 8
                        + [pltpu.VMEM((tb, 2 * D_A), F32)]),
        compiler_params=_cparams(("parallel", "arbitrary"), 52),
        name="rwkv",
    )(p, p, p, p, p, p, p, p, mu, mu, mu, mu, w0, a0, k_k, k_a, w_lora, r_k, lnx_g, lnx_b)


def _bucket_table():
    qi = np.arange(BLOCK)[:, None]
    kj = np.arange(BLOCK)[None, :]
    n = np.where(kj > qi, qi + BLOCK - kj, qi - kj)
    assert WINDOW == BLOCK and n.min() >= 0 and n.max() < WINDOW
    nf = np.maximum(n, 1).astype(np.float32)
    large = RPB_MAX_EXACT + (np.log(nf / np.float32(RPB_MAX_EXACT))
                             / np.float32(math.log(RPB_MAX_DIST / RPB_MAX_EXACT))
                             * np.float32(RPB_BUCKETS - RPB_MAX_EXACT)).astype(np.int32)
    large = np.minimum(large, RPB_BUCKETS - 1)
    return np.where(n < RPB_MAX_EXACT, n, large).astype(np.int32)


def _bias_kernel(tab_ref, bkt_ref, o_ref):
    h0 = pl.program_id(0) * GQA
    bkt = bkt_ref[...]
    from_prev = (lax.broadcasted_iota(jnp.int32, bkt.shape, 1)
                 > lax.broadcasted_iota(jnp.int32, bkt.shape, 0))
    for i in range(GQA):
        acc = jnp.zeros(bkt.shape, F32)
        for b in range(RPB_BUCKETS):
            acc = jnp.where(bkt == b, tab_ref[b, h0 + i], acc)
        o_ref[0, i] = acc
        o_ref[1, i] = jnp.where(from_prev, NEG, acc)


def _attn_bias(rpb_table):
    bkt = jnp.asarray(_bucket_table())
    return pl.pallas_call(
        _bias_kernel,
        grid=(H_KV,),
        in_specs=[pl.BlockSpec(memory_space=pltpu.SMEM),
                  pl.BlockSpec((BLOCK, BLOCK), lambda g: (0, 0))],
        out_specs=pl.BlockSpec((2, GQA, BLOCK, BLOCK), lambda g: (0, g, 0, 0)),
        out_shape=jax.ShapeDtypeStruct((2, H_Q, BLOCK, BLOCK), F32),
        compiler_params=_cparams(("arbitrary",), 16),
        name="attn_bias",
    )(rpb_table, bkt)


def _swa_kernel(sink_ref, *refs):
    *q_refs, kc_ref, kp_ref, vc_ref, vp_ref, bias_ref, o_ref = refs
    each = lambda f, *ls: [f(*a) for a in zip(*ls)]
    lo = lax.broadcasted_iota(jnp.int32, (BLOCK, PAIR), 1) < HEAD
    from_prev = (lax.broadcasted_iota(jnp.int32, (BLOCK, BLOCK), 1)
                 > lax.broadcasted_iota(jnp.int32, (BLOCK, BLOCK), 0))
    from_prev2 = jnp.concatenate([from_prev, from_prev], axis=0)

    def exps(s_, m_):
        parts = []
        for sh, mh in zip(s_, m_):
            e_ = jnp.exp(sh - mh)
            parts += [jnp.where(from_prev, e_, 0.0), jnp.where(from_prev, 0.0, e_)]
        return jnp.concatenate(parts, axis=1).astype(BF16)

    scale = HEAD ** -0.5
    zeros = jnp.zeros((2 * BLOCK, HEAD), BF16)
    ones = jnp.ones((2 * BLOCK, HEAD), BF16)
    npr = GQA // 2
    kcat = jnp.concatenate([kp_ref[...], kc_ref[...]], axis=0).astype(F32) * scale
    vcat = jnp.concatenate([vp_ref[...], vc_ref[...]], axis=0).astype(F32)
    for g in range(H_KV):
        gsl = slice(g * HEAD, (g + 1) * HEAD)
        kg = kcat[:, gsl].astype(BF16)
        vg = vcat[:, gsl].astype(BF16)
        kdup = jnp.concatenate([kg, kg], axis=1)
        rhs = jnp.concatenate([jnp.concatenate([vg, zeros, ones, zeros], axis=1),
                               jnp.concatenate([zeros, vg, zeros, ones], axis=1)], axis=0)
        heads = [g * GQA + 2 * i for i in range(npr)]
        lanes = [slice(h * HEAD, (h + 2) * HEAD) for h in heads]
        qp = [q_refs[h // 4][:, (h % 4) * HEAD:(h % 4 + 2) * HEAD] for h in heads]
        s2 = each(lambda q_: _dot_nt(jnp.concatenate([jnp.where(lo, q_, 0.0).astype(BF16),
                                                      jnp.where(lo, 0.0, q_).astype(BF16)], axis=0),
                                     kdup), qp)
        sf = each(lambda s_: jnp.where(from_prev2, s_[:, 0:BLOCK], s_[:, BLOCK:]), s2)
        s = [(s_[0:BLOCK] + bias_ref[h], s_[BLOCK:] + bias_ref[h + 1]) for s_, h in zip(sf, heads)]
        m = [(jnp.maximum(jnp.max(a, axis=-1, keepdims=True), sink_ref[0, h]),
              jnp.maximum(jnp.max(b, axis=-1, keepdims=True), sink_ref[0, h + 1]))
             for (a, b), h in zip(s, heads)]
        e = each(exps, s, m)
        od = each(lambda e_: jnp.dot(e_, rhs, preferred_element_type=F32), e)
        for o_, m_, h, ln in zip(od, m, heads, lanes):
            den = o_[:, PAIR:] + jnp.where(lo, jnp.exp(sink_ref[0, h] - m_[0]),
                                           jnp.exp(sink_ref[0, h + 1] - m_[1]))
            o_ref[:, ln] = (o_[:, 0:PAIR] / den).astype(o_ref.dtype)


def _swa(p, bias, sinks, bsz, seq):
    t = p.shape[0]
    nb = seq // BLOCK
    kvw = H_KV * HEAD
    nq = D_B // kvw

    def cur(c):
        return pl.BlockSpec((BLOCK, kvw), lambda b, n: (b * nb + n, c))

    def prev(c):
        return pl.BlockSpec((BLOCK, kvw), lambda b, n: (b * nb + jnp.maximum(n - 1, 0), c))

    kb, vb = OFF_KB // kvw, OFF_VB // kvw
    return pl.pallas_call(
        _swa_kernel,
        grid=(bsz, nb),
        in_specs=[pl.BlockSpec(memory_space=pltpu.SMEM)]
        + [cur(OFF_Q // kvw + i) for i in range(nq)]
        + [cur(kb), prev(kb), cur(vb), prev(vb),
           pl.BlockSpec((None, H_Q, BLOCK, BLOCK),
                        lambda b, n: (jnp.where(n == 0, 1, 0), 0, 0, 0))],
        out_specs=pl.BlockSpec((BLOCK, D_B), lambda b, n: (b * nb + n, 0)),
        out_shape=jax.ShapeDtypeStruct((t, D_B), BF16),
        compiler_params=_cparams(("parallel", "arbitrary"), 32),
        name="swa",
    )(sinks, *([p] * (nq + 4)), bias)


def _post_mix_kernel(x_ref, mu_ref, rstd_ref, mix_ref, ge_ref, be_ref, g1_ref, b1_ref, mod_ref,
                     x1_ref, u_ref):
    wide = lambda s_: jnp.concatenate([s_] * (x_ref.shape[1] // LANES), axis=1)
    axn = ((x_ref[...] - wide(mu_ref[...])) * wide(rstd_ref[...]) * (ALPHA * ge_ref[...])
           + ALPHA * be_ref[...])
    z = axn + (1.0 + mod_ref[2:3, :]) * mix_ref[...].astype(F32)
    x1 = _layer_norm(z, g1_ref[...], b1_ref[...])
    x1_ref[...] = x1
    u_ref[...] = (x1 * (1.0 + mod_ref[4:5, :]) + mod_ref[3:4, :]).astype(u_ref.dtype)


def _post_mix(x2, mu, rstd, mix, ge, be, g1, b1, mod, seq, tr=256):
    t, d = x2.shape
    per = seq // tr
    row = pl.BlockSpec((tr, d), lambda i: (i, 0))
    stat = pl.BlockSpec((tr, LANES), lambda i: (i, 0))
    vec = pl.BlockSpec((1, d), lambda i: (0, 0))
    return pl.pallas_call(
        _post_mix_kernel,
        grid=(t // tr,),
        in_specs=[row, stat, stat, row, vec, vec, vec, vec,
                  pl.BlockSpec((None, 6, d), lambda i: (i // per, 0, 0))],
        out_specs=[row, row],
        out_shape=[jax.ShapeDtypeStruct((t, d), F32), jax.ShapeDtypeStruct((t, d), BF16)],
        compiler_params=_cparams(("parallel",), 52),
        name="post_mix",
    )(x2, mu, rstd, mix, ge, be, g1, b1, mod)


def _final_kernel(x1_ref, h_ref, g2_ref, b2_ref, mod_ref, o_ref):
    z = ALPHA * x1_ref[...] + (1.0 + mod_ref[5:6, :]) * h_ref[...].astype(F32)
    o_ref[...] = _layer_norm(z, g2_ref[...], b2_ref[...])


def _final(x1, h, g2, b2, mod, seq, tr=256):
    t, d = x1.shape
    per = seq // tr
    row = pl.BlockSpec((tr, d), lambda i: (i, 0))
    vec = pl.BlockSpec((1, d), lambda i: (0, 0))
    return pl.pallas_call(
        _final_kernel,
        grid=(t // tr,),
        in_specs=[row, row, vec, vec, pl.BlockSpec((None, 6, d), lambda i: (i // per, 0, 0))],
        out_specs=row,
        out_shape=jax.ShapeDtypeStruct((t, d), F32),
        compiler_params=_cparams(("parallel",), 48),
        name="final_ln",
    )(x1, h, g2, b2, mod)


def _lora_weights(w_decay_up, w_iclr_up, w_gate_up):
    zd = jnp.zeros((LORA_COLS, D_A), F32)
    wd = zd.at[0:DECAY_LORA].set(w_decay_up)
    wa = zd.at[DECAY_LORA:DECAY_LORA + ICLR_LORA].set(w_iclr_up)
    wg = zd.at[DECAY_LORA + ICLR_LORA:].set(w_gate_up)
    return jnp.concatenate([wd, wa, wg], axis=1).astype(BF16)


def kernel(x, c, ln_emb_g, ln_emb_b, rpb_table, w_mod, b_mod, w_in, mu_shift, w0, w_decay_up, a0,
           w_iclr_up, w_gate_up, k_k, k_a, r_k, lnx_g, lnx_b, attn_sinks, w_out, ln1_g, ln1_b,
           w_up, w_down, ln2_g, ln2_b):
    bsz, seq, d = x.shape
    assert w_mod.shape[0] == DEPTH == 1 and d == D_MODEL and bsz <= 8
    t = bsz * seq
    row = lambda a: a.reshape(1, -1)
    x2 = x.reshape(t, d)
    c8 = jnp.pad(c, ((0, 8 - bsz), (0, 0)))
    bias = _attn_bias(rpb_table)
    mod = _modulation(c8, w_mod[0], row(b_mod[0]))[:bsz].reshape(bsz, 6, d)
    u1, mu, rstd = _ln_mod(x2, row(ln_emb_g), row(ln_emb_b), mod, seq)
    p = _matmul(u1, w_in[0].astype(BF16), tm=1024, tn=1280, out_dtype=BF16, name="in_proj")
    w_lora = _lora_weights(w_decay_up[0], w_iclr_up[0], w_gate_up[0])
    y_a = _rwkv(p, row(mu_shift[0]), row(w0[0]), row(a0[0]), row(k_k[0]), row(k_a[0]), w_lora,
                row(r_k[0]), row(lnx_g[0]), row(lnx_b[0]), bsz, seq)
    y_b = _swa(p, bias, row(attn_sinks[0]), bsz, seq)
    mix = _matmul([y_a, y_b], w_out[0], tm=1024, tn=512, out_dtype=BF16, name="out_proj")
    x1, u2 = _post_mix(x2, mu, rstd, mix, row(ln_emb_g), row(ln_emb_b), row(ln1_g[0]),
                       row(ln1_b[0]), mod, seq)
    hmid, wd = _matmul(u2, w_up[0], tm=2048, tn=512, out_dtype=BF16, relu2=True,
                       convert=w_down[0], a_buffers=1, name="mlp_up")
    hout = _matmul(hmid, wd, tm=1024, tn=1024, tk=4096, out_dtype=BF16, name="mlp_down")
    out = _final(x1, hout, row(ln2_g[0]), row(ln2_b[0]), mod, seq)
    return out.reshape(bsz, seq, d)
```

```python
import functools
import math

import numpy as np
import jax
import jax.numpy as jnp
from jax import lax
from jax.experimental import pallas as pl
from jax.experimental.pallas import tpu as pltpu

F32 = jnp.float32
BF16 = jnp.bfloat16

D_MODEL = 4096
HEAD = 64
D_A = D_MODEL // 2
D_B = D_MODEL - D_A
H_A = D_A // HEAD
H_Q = D_B // HEAD
GQA = 8
H_KV = H_Q // GQA
WINDOW = 128
BLOCK = 128
RPB_BUCKETS = 32
RPB_MAX_EXACT = RPB_BUCKETS // 2
RPB_MAX_DIST = 128
DECAY_LORA = max(32, int(round(D_A ** 0.5 * 1.8 / 32)) * 32)
ICLR_LORA = max(32, int(round(D_A ** 0.5 * 1.8 / 32)) * 32)
GATE_LORA = max(32, int(round(D_A ** 0.6 * 0.8 / 32)) * 32)
LORA_COLS = DECAY_LORA + ICLR_LORA + GATE_LORA
D_FF = 4 * D_MODEL
DEPTH = 1
ALPHA = (2.0 * DEPTH) ** 0.25
LN_EPS = 1e-5
LNX_EPS = 64e-5
OFF_W = 3 * D_A
RWKV_COLS = OFF_W + LORA_COLS
OFF_Q = RWKV_COLS
OFF_KB = OFF_Q + D_B
OFF_VB = OFF_KB + H_KV * HEAD
N_IN = OFF_VB + H_KV * HEAD
NEG = -1e30

CHUNK = 64
PAIR = 2 * HEAD
NPAIR = H_A // 2
BF16_ROWS = 16
LANES = 128
VMEM_CAP = 56 * 1024 * 1024


def _cparams(sem, vmem_mb):
    return pltpu.CompilerParams(dimension_semantics=sem,
                                vmem_limit_bytes=min(int(vmem_mb * 1024 * 1024), VMEM_CAP))


def _dot(a, b):
    return jnp.dot(a.astype(BF16), b.astype(BF16), preferred_element_type=F32)


def _dot_nt(a, b):
    return lax.dot_general(a.astype(BF16), b.astype(BF16), (((1,), (1,)), ((), ())),
                           preferred_element_type=F32)


def _split2(x):
    hi = x.astype(BF16)
    lo = (x - hi.astype(F32)).astype(BF16)
    return hi, lo


def _layer_norm(x, g, b):
    mu = jnp.mean(x, axis=-1, keepdims=True)
    xc = x - mu
    var = jnp.mean(xc * xc, axis=-1, keepdims=True)
    return xc * lax.rsqrt(var + LN_EPS) * g + b


def _ln_rows(src_ref, consume):
    groups = [slice(g * BF16_ROWS, (g + 1) * BF16_ROWS) for g in range(src_ref.shape[0] // BF16_ROWS)]
    mus = [jnp.mean(src_ref[rs, :], axis=-1, keepdims=True) for rs in groups]
    rstd = [lax.rsqrt(jnp.mean(jnp.square(src_ref[rs, :] - mu), axis=-1, keepdims=True) + LN_EPS)
            for rs, mu in zip(groups, mus)]
    for rs, mu, r in zip(groups, mus, rstd):
        consume(rs, (src_ref[rs, :] - mu) * r, mu, r)


def _head_ones(n):
    r = lax.broadcasted_iota(jnp.int32, (n, n), 0)
    c = lax.broadcasted_iota(jnp.int32, (n, n), 1)
    return ((r >> 6) == (c >> 6)).astype(BF16)


def _mod_kernel(c_ref, w_ref, b_ref, o_ref):
    c = c_ref[...]
    cond = c * jax.nn.sigmoid(c)
    ch, cl = _split2(cond)
    wh, wl = _split2(w_ref[...])
    rows = c.shape[0]
    both = jnp.dot(jnp.concatenate([ch, cl], axis=0), wh, preferred_element_type=F32)
    o_ref[...] = (both[0:rows] + both[rows:] + jnp.dot(ch, wl, preferred_element_type=F32)
                  + b_ref[...])


def _modulation(c8, w_mod, b_mod, tn=1024):
    d, n = w_mod.shape
    return pl.pallas_call(
        _mod_kernel,
        grid=(n // tn,),
        in_specs=[pl.BlockSpec((8, d), lambda j: (0, 0)),
                  pl.BlockSpec((d, tn), lambda j: (0, j)),
                  pl.BlockSpec((1, tn), lambda j: (0, j))],
        out_specs=pl.BlockSpec((8, tn), lambda j: (0, j)),
        out_shape=jax.ShapeDtypeStruct((8, n), F32),
        compiler_params=_cparams(("parallel",), 56),
        name="modulation",
    )(c8, w_mod, b_mod)


def _ln_mod_kernel(x_ref, g_ref, b_ref, mod_ref, u_ref, mu_ref, rstd_ref):
    gain = 1.0 + mod_ref[1:2, :]
    scale = g_ref[...] * gain
    shift = b_ref[...] * gain + mod_ref[0:1, :]

    def consume(rs, xh, mu, rstd):
        u_ref[rs, :] = (xh * scale + shift).astype(u_ref.dtype)
        mu_ref[rs, :] = jnp.broadcast_to(mu, (BF16_ROWS, LANES))
        rstd_ref[rs, :] = jnp.broadcast_to(rstd, (BF16_ROWS, LANES))

    _ln_rows(x_ref, consume)


def _ln_mod(x2, g, b, mod, seq, tr=512):
    t, d = x2.shape
    per = seq // tr
    stat = pl.BlockSpec((tr, LANES), lambda i: (i, 0))
    return pl.pallas_call(
        _ln_mod_kernel,
        grid=(t // tr,),
        in_specs=[pl.BlockSpec((tr, d), lambda i: (i, 0)),
                  pl.BlockSpec((1, d), lambda i: (0, 0)),
                  pl.BlockSpec((1, d), lambda i: (0, 0)),
                  pl.BlockSpec((None, 6, d), lambda i: (i // per, 0, 0))],
        out_specs=[pl.BlockSpec((tr, d), lambda i: (i, 0)), stat, stat],
        out_shape=[jax.ShapeDtypeStruct((t, d), BF16), jax.ShapeDtypeStruct((t, LANES), F32),
                   jax.ShapeDtypeStruct((t, LANES), F32)],
        compiler_params=_cparams(("parallel",), 48),
        name="ln_mod",
    )(x2, g, b, mod)


def _mm_kernel(*refs, relu2, convert):
    if convert:
        *refs, ci_ref, o_ref, co_ref = refs
        co_ref[...] = ci_ref[...].astype(co_ref.dtype)
        refs = (*refs, o_ref)
    *a_refs, b_ref, o_ref = refs
    acc, off = None, 0
    for a_ref in a_refs:
        kd = a_ref.shape[1]
        part = jnp.dot(a_ref[...], b_ref[off:off + kd, :].astype(BF16), preferred_element_type=F32)
        acc = part if acc is None else acc + part
        off += kd
    if relu2:
        acc = jnp.square(jnp.maximum(acc, 0.0))
    o_ref[...] = acc.astype(o_ref.dtype)


def _mm_acc_kernel(a_ref, b_ref, o_ref, acc_ref):
    k = pl.program_id(2)

    @pl.when(k == 0)
    def _():
        acc_ref[...] = jnp.zeros_like(acc_ref)

    acc_ref[...] += jnp.dot(a_ref[...], b_ref[...].astype(BF16), preferred_element_type=F32)

    @pl.when(k == pl.num_programs(2) - 1)
    def _():
        o_ref[...] = acc_ref[...].astype(o_ref.dtype)


def _matmul(a, b, *, tm, tn, tk=None, out_dtype=F32, relu2=False, convert=None, name="matmul"):
    a = a if isinstance(a, (list, tuple)) else [a]
    m = a[0].shape[0]
    kd, n = b.shape
    tm = min(tm, m)
    nj = n // tn
    osz = jnp.dtype(out_dtype).itemsize
    bsz = jnp.dtype(b.dtype).itemsize
    bsz = 2 * bsz + (2 if bsz == 4 else 0)
    if tk is None or tk >= kd:
        vm = (2 * tm * kd * 2 + kd * tn * bsz + 2 * tm * tn * osz + tm * tn * 4) / 2 ** 20 + 8
        in_specs = ([pl.BlockSpec((tm, x.shape[1]), lambda i, j: (i, 0)) for x in a]
                    + [pl.BlockSpec((kd, tn), lambda i, j: (0, j))])
        out_specs = pl.BlockSpec((tm, tn), lambda i, j: (i, j))
        out_shape = jax.ShapeDtypeStruct((m, n), out_dtype)
        args = (*a, b)
        if convert is not None:
            cr, cc = convert.shape
            slab = cr // ((m // tm) * nj)
            assert slab * (m // tm) * nj == cr and slab % BF16_ROWS == 0
            cspec = pl.BlockSpec((slab, cc), lambda i, j: (i * nj + j, 0))
            in_specs, args = in_specs + [cspec], (*args, convert)
            out_specs = [out_specs, cspec]
            out_shape = [out_shape, jax.ShapeDtypeStruct((cr, cc), BF16)]
            vm += slab * cc * 12 / 2 ** 20
        return pl.pallas_call(
            functools.partial(_mm_kernel, relu2=relu2, convert=convert is not None),
            grid=(m // tm, nj),
            in_specs=in_specs,
            out_specs=out_specs,
            out_shape=out_shape,
            compiler_params=_cparams(("parallel", "parallel"), vm),
            name=name,
        )(*args)
    assert not relu2 and len(a) == 1 and convert is None
    vm = (2 * tm * tk * 2 + tk * tn * bsz + 2 * tm * tn * osz + 2 * tm * tn * 4) / 2 ** 20 + 8
    return pl.pallas_call(
        _mm_acc_kernel,
        grid=(m // tm, nj, kd // tk),
        in_specs=[pl.BlockSpec((tm, tk), lambda i, j, k: (i, k)),
                  pl.BlockSpec((tk, tn), lambda i, j, k: (k, j))],
        out_specs=pl.BlockSpec((tm, tn), lambda i, j, k: (i, j)),
        out_shape=jax.ShapeDtypeStruct((m, n), out_dtype),
        scratch_shapes=[pltpu.VMEM((tm, tn), F32)],
        compiler_params=_cparams(("parallel", "parallel", "arbitrary"), vm),
        name=name,
    )(a[0], b)


def _scan_chunk(ins, hts, consts, tick):
    m0, strict, incl, eye, tri = consts

    ticking = [False]

    def each(f, *ls):
        out = [f(*a) for a in zip(*ls)]
        if ticking[0]:
            tick()
        return out

    r, k, v, al, be, lw = (list(z) for z in zip(*ins))

    def cumsum(x):
        hi = x.astype(BF16)
        rem = x - hi.astype(F32)
        mid = rem.astype(BF16)
        lo = (rem - mid.astype(F32)).astype(BF16)
        c3 = jnp.dot(tri, jnp.concatenate([hi, mid, lo], axis=1), preferred_element_type=F32)
        return c3[:, 0:PAIR] + c3[:, PAIR:2 * PAIR] + c3[:, 2 * PAIR:3 * PAIR]

    def sm(x):
        return jnp.concatenate([jnp.where(m0, x, 0.0), jnp.where(m0, 0.0, x)], axis=0)

    b16 = lambda t_: t_.astype(BF16)
    nn = lambda a_, b_: jnp.dot(a_, b_, preferred_element_type=F32)
    c = each(cumsum, lw)
    pc = each(lambda c_: jnp.exp(c_[CHUNK - 1:CHUNK, :]), c)
    einv = each(lambda c_: jnp.exp(-c_), c)
    a_sm = each(lambda a_, c_, l_: b16(sm(a_ * jnp.exp(c_ - l_))), al, c, lw)
    r_sm = each(lambda r_, c_: sm(r_ * jnp.exp(c_)), r, c)
    v_sm = each(lambda v_: b16(sm(v_)), v)
    b_t = each(lambda b_, e_: b_ * e_, be, einv)
    k_t = each(lambda k_, e_: k_ * e_, k, einv)

    def scores(a_, r_, b_, k_):
        bb, kb = b16(b_), b16(k_)
        return _dot_nt(jnp.concatenate([a_, b16(r_)], axis=0),
                       jnp.concatenate([bb, bb, kb, kb], axis=0))

    s = each(scores, a_sm, r_sm, b_t, k_t)
    lab = each(lambda s_: jnp.where(strict, s_[0:PAIR, 0:PAIR], 0.0), s)
    mak = each(lambda s_: b16(jnp.where(strict, s_[0:PAIR, PAIR:], 0.0)), s)
    incl2 = jnp.concatenate([incl, incl], axis=1)
    mrbk = each(lambda s_: b16(jnp.where(incl2, s_[PAIR:, :], 0.0)), s)

    ticking[0] = True
    ldt = each(lambda l_: l_.T, lab)
    xt = each(lambda l_: jnp.where(eye, 1.0, l_), ldt)
    lt = each(lambda l_: nn(b16(l_), b16(l_)), ldt)

    def series_step(l_, x_):
        lb = b16(l_)
        return nn(lb, jnp.concatenate([b16(x_), lb], axis=1))

    for _ in range(4):
        xl = each(series_step, lt, xt)
        xt = each(lambda x_, p_: x_ + p_[:, 0:PAIR], xt, xl)
        lt = each(lambda p_: p_[:, PAIR:], xl)
    x = each(lambda x_, l_: b16((x_ + nn(b16(l_), b16(x_))).T), xt, lt)

    makv = each(nn, mak, v_sm)
    wu = each(lambda x_, a_, m_: nn(x_, jnp.concatenate([a_, b16(m_)], axis=1)), x, a_sm, makv)
    bigr = each(lambda wu_, v_: jnp.concatenate(
        [b16(wu_), jnp.concatenate([jnp.zeros_like(v_), v_], axis=1)], axis=0), wu, v_sm)
    bk = each(lambda b_, k_, p_: b16(jnp.concatenate([sm(b_ * p_), sm(k_ * p_)], axis=0).T),
              b_t, k_t, pc)
    gz = each(nn, bk, bigr)
    qy = each(nn, mrbk, bigr)
    hb = each(b16, hts)

    def new_state(ht, h_, g_, p_):
        return ht * p_ + _dot_nt(h_, g_[:, 0:PAIR]) + g_[:, PAIR:].T

    def output(r_, q_, h_):
        y_sm = _dot_nt(r_ + q_[:, 0:PAIR], h_) + q_[:, PAIR:]
        return y_sm[0:CHUNK, :] + y_sm[CHUNK:, :]

    return each(output, r_sm, qy, hb), each(new_state, hts, hb, gz, pc)


def _rwkv_kernel(pr_ref, pk_ref, pv_ref, pc_ref, qr_ref, qk_ref, qv_ref, qc_ref,
                 mr_ref, mk_ref, mv_ref, mc_ref, w0_ref, a0_ref, kk_ref, ka_ref, wl_ref,
                 rk_ref, lg_ref, lb_ref, ci_ref, o_ref, co_ref,
                 h_ref, y_ref, r_s, k_s, v_s, al_s, be_s, lw_s, g_s, ud_s, *, nchunk):
    co_ref[...] = ci_ref[...].astype(co_ref.dtype)
    first = pl.program_id(1) == 0

    @pl.when(first)
    def _():
        h_ref[...] = jnp.zeros_like(h_ref)

    row = lax.broadcasted_iota(jnp.int32, (PAIR, PAIR), 0)
    col = lax.broadcasted_iota(jnp.int32, (PAIR, PAIR), 1)
    same = (row >> 6) == (col >> 6)
    tr_ = row & (CHUNK - 1)
    tc_ = col & (CHUNK - 1)
    strict = jnp.logical_and(same, tc_ < tr_)
    incl = jnp.logical_and(same, tc_ <= tr_)
    eye = row == col
    m0 = lax.broadcasted_iota(jnp.int32, (CHUNK, PAIR), 1) < HEAD
    tri = (lax.broadcasted_iota(jnp.int32, (CHUNK, CHUNK), 1)
           <= lax.broadcasted_iota(jnp.int32, (CHUNK, CHUNK), 0)).astype(BF16)
    consts = (m0, strict, incl, eye, tri)
    ones = _head_ones(2 * PAIR)
    slab = 4 * PAIR

    def prep_stages(cn):
        at_start = isinstance(cn, int)
        rows = pl.ds(0, CHUNK) if at_start else pl.ds(pl.multiple_of(cn * CHUNK, CHUNK), CHUNK)

        def shift(x_ref, q_ref, m_ref, ln):
            x = x_ref[rows, ln].astype(F32)
            if at_start:
                last = jnp.where(first, 0.0, q_ref[BF16_ROWS - 1:BF16_ROWS, ln].astype(F32))
            else:
                tail = pl.ds(pl.multiple_of(cn * CHUNK - BF16_ROWS, BF16_ROWS), BF16_ROWS)
                last = x_ref[tail, ln][BF16_ROWS - 1:BF16_ROWS, :].astype(F32)
            prev = pltpu.roll(x, 1, axis=0)
            r8 = lax.broadcasted_iota(jnp.int32, (8, x.shape[1]), 0)
            head = jnp.where(r8 == 0, jnp.broadcast_to(last, (8, x.shape[1])), prev[0:8, :])
            prev = jnp.concatenate([head, prev[8:, :]], axis=0)
            return x + (prev - x) * m_ref[:, ln]

        for j in range(D_A // slab):
            ln = slice(j * slab, (j + 1) * slab)
            r_s[rows, ln] = shift(pr_ref, qr_ref, mr_ref, ln)
            v_s[rows, ln] = shift(pv_ref, qv_ref, mv_ref, ln)
            yield
            k = shift(pk_ref, qk_ref, mk_ref, ln)
            kk = k * kk_ref[:, ln]
            kk2 = kk * kk
            ss = jnp.concatenate([_dot(kk2[:, i * 2 * PAIR:(i + 1) * 2 * PAIR], ones)
                                  for i in range(slab // (2 * PAIR))], axis=1)
            yield
            a = jax.nn.sigmoid(a0_ref[:, ln] + ud_s[rows, D_A + j * slab:D_A + (j + 1) * slab])
            kk = kk * lax.rsqrt(jnp.maximum(ss, 1e-24))
            k_s[rows, ln] = k * (1.0 + (a - 1.0) * ka_ref[:, ln])
            al_s[rows, ln] = -kk
            be_s[rows, ln] = kk * a
            lw_s[rows, ln] = -math.exp(-0.5) * jax.nn.sigmoid(w0_ref[:, ln] + ud_s[rows, ln])
            yield

    code = pc_ref[...].astype(F32)
    last = jnp.where(first, 0.0, qc_ref[BF16_ROWS - 1:BF16_ROWS, :].astype(F32))
    prev = pltpu.roll(code, 1, axis=0)
    r8 = lax.broadcasted_iota(jnp.int32, (8, LORA_COLS), 0)
    prev = jnp.concatenate([jnp.where(r8 == 0, jnp.broadcast_to(last, (8, LORA_COLS)), prev[0:8, :]),
                            prev[8:, :]], axis=0)
    code = code + (prev - code) * mc_ref[...]
    lane = lax.broadcasted_iota(jnp.int32, code.shape, 1)
    act = jnp.where(lane < DECAY_LORA, jnp.tanh(code),
                    jnp.where(lane < DECAY_LORA + ICLR_LORA, code, jax.nn.sigmoid(code))).astype(BF16)
    ud_s[:, 0:D_A] = jnp.dot(act, wl_ref[:, 0:D_A], preferred_element_type=F32)
    ud_s[:, D_A:] = jnp.dot(act, wl_ref[:, D_A:2 * D_A], preferred_element_type=F32)
    g_s[...] = jnp.dot(act, wl_ref[:, 2 * D_A:], preferred_element_type=F32)

    for _ in prep_stages(0):
        pass

    def chunk_body(ci, carry):
        rows = pl.ds(pl.multiple_of(ci * CHUNK, CHUNK), CHUNK)
        lanes = [slice(p * PAIR, (p + 1) * PAIR) for p in range(NPAIR)]
        ins = [tuple(s[rows, ln] for s in (r_s, k_s, v_s, al_s, be_s, lw_s)) for ln in lanes]
        hts = [h_ref[p] for p in range(NPAIR)]
        gen = prep_stages(jnp.minimum(ci + 1, nchunk - 1))
        ys, hns = _scan_chunk(ins, hts, consts, lambda: next(gen, None))
        for _ in gen:
            pass
        for p in range(NPAIR):
            h_ref[p] = hns[p]
            y_ref[rows, lanes[p]] = ys[p]
        return carry

    lax.fori_loop(0, nchunk, chunk_body, 0)

    tb = y_ref.shape[0]
    width = 2 * PAIR
    for half in range(2):
        lns = [slice(q * width, (q + 1) * width)
               for q in range(half * NPAIR // 4, (half + 1) * NPAIR // 4)]
        stack = lambda f: jnp.concatenate([f(ln) for ln in lns], axis=0)
        y = stack(lambda ln: y_ref[:, ln])
        sums = _dot(jnp.concatenate(
            [y, stack(lambda ln: r_s[:, ln] * k_s[:, ln] * rk_ref[:, ln])], axis=0), ones)
        yc = y - sums[0:len(lns) * tb] * (1.0 / HEAD)
        rstd = lax.rsqrt(_dot(yc * yc, ones) * (1.0 / HEAD) + LNX_EPS)
        for i, ln in enumerate(lns):
            rs = slice(i * tb, (i + 1) * tb)
            yn = yc[rs] * rstd[rs] * lg_ref[:, ln] + lb_ref[:, ln]
            rk = sums[len(lns) * tb + i * tb:len(lns) * tb + (i + 1) * tb]
            o_ref[:, ln] = ((yn + rk * v_s[:, ln]) * g_s[:, ln]).astype(o_ref.dtype)


def _rwkv(p, mu, w0, a0, k_k, k_a, w_lora, r_k, lnx_g, lnx_b, convert, bsz, seq, tb=256):
    t = p.shape[0]
    per = seq // tb
    slab = convert.shape[0] // (bsz * per)
    assert slab * bsz * per == convert.shape[0] and slab % BF16_ROWS == 0
    cspec = pl.BlockSpec((slab, convert.shape[1]), lambda b, s: (b * per + s, 0))
    nchunk = tb // CHUNK
    assert nchunk >= 2
    cblk = OFF_W // LORA_COLS
    tail = tb // BF16_ROWS

    def cur(j):
        return pl.BlockSpec((tb, D_A), lambda b, s: (b * per + s, j))

    def prev(j):
        return pl.BlockSpec((BF16_ROWS, D_A),
                            lambda b, s: (jnp.maximum((b * per + s) * tail - 1, 0), j))

    def vec(j):
        return pl.BlockSpec((1, D_A), lambda b, s: (0, j))

    in_specs = [cur(0), cur(1), cur(2),
                pl.BlockSpec((tb, LORA_COLS), lambda b, s: (b * per + s, cblk)),
                prev(0), prev(1), prev(2),
                pl.BlockSpec((BF16_ROWS, LORA_COLS),
                             lambda b, s: (jnp.maximum((b * per + s) * tail - 1, 0), cblk)),
                vec(0), vec(1), vec(2),
                pl.BlockSpec((1, LORA_COLS), lambda b, s: (0, cblk)),
                vec(0), vec(0), vec(0), vec(0),
                pl.BlockSpec((LORA_COLS, 3 * D_A), lambda b, s: (0, 0)),
                vec(0), vec(0), vec(0), cspec]
    blk = (tb, D_A)
    return pl.pallas_call(
        functools.partial(_rwkv_kernel, nchunk=nchunk),
        grid=(bsz, per),
        in_specs=in_specs,
        out_specs=[pl.BlockSpec(blk, lambda b, s: (b * per + s, 0)), cspec],
        out_shape=[jax.ShapeDtypeStruct((t, D_A), BF16), jax.ShapeDtypeStruct(convert.shape, BF16)],
        scratch_shapes=([pltpu.VMEM((NPAIR, PAIR, PAIR), F32)] + [pltpu.VMEM(blk, F32)] * 8
                        + [pltpu.VMEM((tb, 2 * D_A), F32)]),
        compiler_params=_cparams(("parallel", "arbitrary"), 52),
        name="rwkv",
    )(p, p, p, p, p, p, p, p, mu, mu, mu, mu, w0, a0, k_k, k_a, w_lora, r_k, lnx_g, lnx_b, convert)


def _bucket_table():
    qi = np.arange(BLOCK)[:, None]
    kj = np.arange(BLOCK)[None, :]
    n = np.where(kj > qi, qi + BLOCK - kj, qi - kj)
    assert WINDOW == BLOCK and n.min() >= 0 and n.max() < WINDOW
    nf = np.maximum(n, 1).astype(np.float32)
    large = RPB_MAX_EXACT + (np.log(nf / np.float32(RPB_MAX_EXACT))
                             / np.float32(math.log(RPB_MAX_DIST / RPB_MAX_EXACT))
                             * np.float32(RPB_BUCKETS - RPB_MAX_EXACT)).astype(np.int32)
    large = np.minimum(large, RPB_BUCKETS - 1)
    return np.where(n < RPB_MAX_EXACT, n, large).astype(np.int32)


def _bias_kernel(tab_ref, bkt_ref, o_ref):
    h0 = pl.program_id(0) * GQA
    bkt = bkt_ref[...]
    from_prev = (lax.broadcasted_iota(jnp.int32, bkt.shape, 1)
                 > lax.broadcasted_iota(jnp.int32, bkt.shape, 0))
    for i in range(GQA):
        acc = jnp.zeros(bkt.shape, F32)
        for b in range(RPB_BUCKETS):
            acc = jnp.where(bkt == b, tab_ref[b, h0 + i], acc)
        o_ref[0, i] = acc
        o_ref[1, i] = jnp.where(from_prev, NEG, acc)


def _attn_bias(rpb_table):
    bkt = jnp.asarray(_bucket_table())
    return pl.pallas_call(
        _bias_kernel,
        grid=(H_KV,),
        in_specs=[pl.BlockSpec(memory_space=pltpu.SMEM),
                  pl.BlockSpec((BLOCK, BLOCK), lambda g: (0, 0))],
        out_specs=pl.BlockSpec((2, GQA, BLOCK, BLOCK), lambda g: (0, g, 0, 0)),
        out_shape=jax.ShapeDtypeStruct((2, H_Q, BLOCK, BLOCK), F32),
        compiler_params=_cparams(("arbitrary",), 16),
        name="attn_bias",
    )(rpb_table, bkt)


def _swa_kernel(sink_ref, *refs):
    *q_refs, kc_ref, kp_ref, vc_ref, vp_ref, bias_ref, o_ref = refs
    each = lambda f, *ls: [f(*a) for a in zip(*ls)]
    lo = lax.broadcasted_iota(jnp.int32, (BLOCK, PAIR), 1) < HEAD
    from_prev = (lax.broadcasted_iota(jnp.int32, (BLOCK, BLOCK), 1)
                 > lax.broadcasted_iota(jnp.int32, (BLOCK, BLOCK), 0))
    from_prev2 = jnp.concatenate([from_prev, from_prev], axis=0)

    def exps(s_, m_):
        parts = []
        for sh, mh in zip(s_, m_):
            e_ = jnp.exp(sh - mh)
            parts += [jnp.where(from_prev, e_, 0.0), jnp.where(from_prev, 0.0, e_)]
        return jnp.concatenate(parts, axis=1).astype(BF16)

    scale = HEAD ** -0.5
    zeros = jnp.zeros((2 * BLOCK, HEAD), BF16)
    ones = jnp.ones((2 * BLOCK, HEAD), BF16)
    npr = GQA // 2
    kcat = jnp.concatenate([kp_ref[...], kc_ref[...]], axis=0).astype(F32) * scale
    vcat = jnp.concatenate([vp_ref[...], vc_ref[...]], axis=0).astype(F32)
    for g in range(H_KV):
        gsl = slice(g * HEAD, (g + 1) * HEAD)
        kg = kcat[:, gsl].astype(BF16)
        vg = vcat[:, gsl].astype(BF16)
        kdup = jnp.concatenate([kg, kg], axis=1)
        rhs = jnp.concatenate([jnp.concatenate([vg, zeros, ones, zeros], axis=1),
                               jnp.concatenate([zeros, vg, zeros, ones], axis=1)], axis=0)
        heads = [g * GQA + 2 * i for i in range(npr)]
        lanes = [slice(h * HEAD, (h + 2) * HEAD) for h in heads]
        qp = [q_refs[h // 4][:, (h % 4) * HEAD:(h % 4 + 2) * HEAD] for h in heads]
        s2 = each(lambda q_: _dot_nt(jnp.concatenate([jnp.where(lo, q_, 0.0).astype(BF16),
                                                      jnp.where(lo, 0.0, q_).astype(BF16)], axis=0),
                                     kdup), qp)
        sf = each(lambda s_: jnp.where(from_prev2, s_[:, 0:BLOCK], s_[:, BLOCK:]), s2)
        s = [(s_[0:BLOCK] + bias_ref[h], s_[BLOCK:] + bias_ref[h + 1]) for s_, h in zip(sf, heads)]
        m = [(jnp.maximum(jnp.max(a, axis=-1, keepdims=True), sink_ref[0, h]),
              jnp.maximum(jnp.max(b, axis=-1, keepdims=True), sink_ref[0, h + 1]))
             for (a, b), h in zip(s, heads)]
        e = each(exps, s, m)
        od = each(lambda e_: jnp.dot(e_, rhs, preferred_element_type=F32), e)
        for o_, m_, h, ln in zip(od, m, heads, lanes):
            den = o_[:, PAIR:] + jnp.where(lo, jnp.exp(sink_ref[0, h] - m_[0]),
                                           jnp.exp(sink_ref[0, h + 1] - m_[1]))
            o_ref[:, ln] = (o_[:, 0:PAIR] / den).astype(o_ref.dtype)


def _swa(p, bias, sinks, bsz, seq):
    t = p.shape[0]
    nb = seq // BLOCK
    kvw = H_KV * HEAD
    nq = D_B // kvw

    def cur(c):
        return pl.BlockSpec((BLOCK, kvw), lambda b, n: (b * nb + n, c))

    def prev(c):
        return pl.BlockSpec((BLOCK, kvw), lambda b, n: (b * nb + jnp.maximum(n - 1, 0), c))

    kb, vb = OFF_KB // kvw, OFF_VB // kvw
    return pl.pallas_call(
        _swa_kernel,
        grid=(bsz, nb),
        in_specs=[pl.BlockSpec(memory_space=pltpu.SMEM)]
        + [cur(OFF_Q // kvw + i) for i in range(nq)]
        + [cur(kb), prev(kb), cur(vb), prev(vb),
           pl.BlockSpec((None, H_Q, BLOCK, BLOCK),
                        lambda b, n: (jnp.where(n == 0, 1, 0), 0, 0, 0))],
        out_specs=pl.BlockSpec((BLOCK, D_B), lambda b, n: (b * nb + n, 0)),
        out_shape=jax.ShapeDtypeStruct((t, D_B), BF16),
        compiler_params=_cparams(("parallel", "arbitrary"), 32),
        name="swa",
    )(sinks, *([p] * (nq + 4)), bias)


def _post_mix_kernel(x_ref, mu_ref, rstd_ref, mix_ref, ge_ref, be_ref, g1_ref, b1_ref, mod_ref,
                     x1_ref, u_ref):
    wide = lambda s_: jnp.concatenate([s_] * (x_ref.shape[1] // LANES), axis=1)
    axn = ((x_ref[...] - wide(mu_ref[...])) * wide(rstd_ref[...]) * (ALPHA * ge_ref[...])
           + ALPHA * be_ref[...])
    z = axn + (1.0 + mod_ref[2:3, :]) * mix_ref[...].astype(F32)
    x1 = _layer_norm(z, g1_ref[...], b1_ref[...])
    x1_ref[...] = x1
    u_ref[...] = (x1 * (1.0 + mod_ref[4:5, :]) + mod_ref[3:4, :]).astype(u_ref.dtype)


def _post_mix(x2, mu, rstd, mix, ge, be, g1, b1, mod, seq, tr=256):
    t, d = x2.shape
    per = seq // tr
    row = pl.BlockSpec((tr, d), lambda i: (i, 0))
    stat = pl.BlockSpec((tr, LANES), lambda i: (i, 0))
    vec = pl.BlockSpec((1, d), lambda i: (0, 0))
    return pl.pallas_call(
        _post_mix_kernel,
        grid=(t // tr,),
        in_specs=[row, stat, stat, row, vec, vec, vec, vec,
                  pl.BlockSpec((None, 6, d), lambda i: (i // per, 0, 0))],
        out_specs=[row, row],
        out_shape=[jax.ShapeDtypeStruct((t, d), F32), jax.ShapeDtypeStruct((t, d), BF16)],
        compiler_params=_cparams(("parallel",), 52),
        name="post_mix",
    )(x2, mu, rstd, mix, ge, be, g1, b1, mod)


def _final_kernel(x1_ref, h_ref, g2_ref, b2_ref, mod_ref, o_ref):
    z = ALPHA * x1_ref[...] + (1.0 + mod_ref[5:6, :]) * h_ref[...].astype(F32)
    o_ref[...] = _layer_norm(z, g2_ref[...], b2_ref[...])


def _final(x1, h, g2, b2, mod, seq, tr=256):
    t, d = x1.shape
    per = seq // tr
    row = pl.BlockSpec((tr, d), lambda i: (i, 0))
    vec = pl.BlockSpec((1, d), lambda i: (0, 0))
    return pl.pallas_call(
        _final_kernel,
        grid=(t // tr,),
        in_specs=[row, row, vec, vec, pl.BlockSpec((None, 6, d), lambda i: (i // per, 0, 0))],
        out_specs=row,
        out_shape=jax.ShapeDtypeStruct((t, d), F32),
        compiler_params=_cparams(("parallel",), 48),
        name="final_ln",
    )(x1, h, g2, b2, mod)


def _lora_weights(w_decay_up, w_iclr_up, w_gate_up):
    zd = jnp.zeros((LORA_COLS, D_A), F32)
    wd = zd.at[0:DECAY_LORA].set(w_decay_up)
    wa = zd.at[DECAY_LORA:DECAY_LORA + ICLR_LORA].set(w_iclr_up)
    wg = zd.at[DECAY_LORA + ICLR_LORA:].set(w_gate_up)
    return jnp.concatenate([wd, wa, wg], axis=1).astype(BF16)


def kernel(x, c, ln_emb_g, ln_emb_b, rpb_table, w_mod, b_mod, w_in, mu_shift, w0, w_decay_up, a0,
           w_iclr_up, w_gate_up, k_k, k_a, r_k, lnx_g, lnx_b, attn_sinks, w_out, ln1_g, ln1_b,
           w_up, w_down, ln2_g, ln2_b):
    bsz, seq, d = x.shape
    assert w_mod.shape[0] == DEPTH == 1 and d == D_MODEL and bsz <= 8
    t = bsz * seq
    row = lambda a: a.reshape(1, -1)
    x2 = x.reshape(t, d)
    c8 = jnp.pad(c, ((0, 8 - bsz), (0, 0)))
    bias = _attn_bias(rpb_table)
    mod = _modulation(c8, w_mod[0], row(b_mod[0]))[:bsz].reshape(bsz, 6, d)
    u1, mu, rstd = _ln_mod(x2, row(ln_emb_g), row(ln_emb_b), mod, seq)
    p = _matmul(u1, w_in[0].astype(BF16), tm=1024, tn=1280, out_dtype=BF16, name="in_proj")
    w_lora = _lora_weights(w_decay_up[0], w_iclr_up[0], w_gate_up[0])
    y_a, wo = _rwkv(p, row(mu_shift[0]), row(w0[0]), row(a0[0]), row(k_k[0]), row(k_a[0]), w_lora,
                    row(r_k[0]), row(lnx_g[0]), row(lnx_b[0]), w_out[0], bsz, seq)
    y_b = _swa(p, bias, row(attn_sinks[0]), bsz, seq)
    mix = _matmul([y_a, y_b], wo, tm=1024, tn=1024, out_dtype=BF16, name="out_proj")
    x1, u2 = _post_mix(x2, mu, rstd, mix, row(ln_emb_g), row(ln_emb_b), row(ln1_g[0]),
                       row(ln1_b[0]), mod, seq)
    hmid, wd = _matmul(u2, w_up[0], tm=1024, tn=512, out_dtype=BF16, relu2=True,
                       convert=w_down[0], name="mlp_up")
    hout = _matmul(hmid, wd, tm=1024, tn=1024, tk=4096, out_dtype=BF16, name="mlp_down")
    out = _final(x1, hout, row(ln2_g[0]), row(ln2_b[0]), mod, seq)
    return out.reshape(bsz, seq, d)
```

```python
import functools
import math

import numpy as np
import jax
import jax.numpy as jnp
from jax import lax
from jax.experimental import pallas as pl
from jax.experimental.pallas import tpu as pltpu

F32 = jnp.float32
BF16 = jnp.bfloat16

D_MODEL = 4096
HEAD = 64
D_A = D_MODEL // 2
D_B = D_MODEL - D_A
H_A = D_A // HEAD
H_Q = D_B // HEAD
GQA = 8
H_KV = H_Q // GQA
WINDOW = 128
BLOCK = 128
RPB_BUCKETS = 32
RPB_MAX_EXACT = RPB_BUCKETS // 2
RPB_MAX_DIST = 128
DECAY_LORA = max(32, int(round(D_A ** 0.5 * 1.8 / 32)) * 32)
ICLR_LORA = max(32, int(round(D_A ** 0.5 * 1.8 / 32)) * 32)
GATE_LORA = max(32, int(round(D_A ** 0.6 * 0.8 / 32)) * 32)
LORA_COLS = DECAY_LORA + ICLR_LORA + GATE_LORA
D_FF = 4 * D_MODEL
DEPTH = 1
ALPHA = (2.0 * DEPTH) ** 0.25
LN_EPS = 1e-5
LNX_EPS = 64e-5
OFF_W = 3 * D_A
RWKV_COLS = OFF_W + LORA_COLS
OFF_Q = RWKV_COLS
OFF_KB = OFF_Q + D_B
OFF_VB = OFF_KB + H_KV * HEAD
N_IN = OFF_VB + H_KV * HEAD
NEG = -1e30

CHUNK = 64
PAIR = 2 * HEAD
NPAIR = H_A // 2
BF16_ROWS = 16
LANES = 128
VMEM_CAP = 56 * 1024 * 1024


def _cparams(sem, vmem_mb):
    return pltpu.CompilerParams(dimension_semantics=sem,
                                vmem_limit_bytes=min(int(vmem_mb * 1024 * 1024), VMEM_CAP))


def _dot(a, b):
    return jnp.dot(a.astype(BF16), b.astype(BF16), preferred_element_type=F32)


def _dot_nt(a, b):
    return lax.dot_general(a.astype(BF16), b.astype(BF16), (((1,), (1,)), ((), ())),
                           preferred_element_type=F32)


def _split2(x):
    hi = x.astype(BF16)
    lo = (x - hi.astype(F32)).astype(BF16)
    return hi, lo


def _layer_norm(x, g, b):
    mu = jnp.mean(x, axis=-1, keepdims=True)
    xc = x - mu
    var = jnp.mean(xc * xc, axis=-1, keepdims=True)
    return xc * lax.rsqrt(var + LN_EPS) * g + b


def _ln_rows(src_ref, consume):
    groups = [slice(g * BF16_ROWS, (g + 1) * BF16_ROWS) for g in range(src_ref.shape[0] // BF16_ROWS)]
    mus = [jnp.mean(src_ref[rs, :], axis=-1, keepdims=True) for rs in groups]
    rstd = [lax.rsqrt(jnp.mean(jnp.square(src_ref[rs, :] - mu), axis=-1, keepdims=True) + LN_EPS)
            for rs, mu in zip(groups, mus)]
    for rs, mu, r in zip(groups, mus, rstd):
        consume(rs, (src_ref[rs, :] - mu) * r, mu, r)


def _head_ones(n):
    r = lax.broadcasted_iota(jnp.int32, (n, n), 0)
    c = lax.broadcasted_iota(jnp.int32, (n, n), 1)
    return ((r >> 6) == (c >> 6)).astype(BF16)


def _mod_kernel(c_ref, w_ref, b_ref, o_ref):
    c = c_ref[...]
    cond = c * jax.nn.sigmoid(c)
    ch, cl = _split2(cond)
    wh, wl = _split2(w_ref[...])
    rows = c.shape[0]
    both = jnp.dot(jnp.concatenate([ch, cl], axis=0), wh, preferred_element_type=F32)
    o_ref[...] = (both[0:rows] + both[rows:] + jnp.dot(ch, wl, preferred_element_type=F32)
                  + b_ref[...])


def _modulation(c8, w_mod, b_mod, tn=1024):
    d, n = w_mod.shape
    return pl.pallas_call(
        _mod_kernel,
        grid=(n // tn,),
        in_specs=[pl.BlockSpec((8, d), lambda j: (0, 0)),
                  pl.BlockSpec((d, tn), lambda j: (0, j)),
                  pl.BlockSpec((1, tn), lambda j: (0, j))],
        out_specs=pl.BlockSpec((8, tn), lambda j: (0, j)),
        out_shape=jax.ShapeDtypeStruct((8, n), F32),
        compiler_params=_cparams(("parallel",), 56),
        name="modulation",
    )(c8, w_mod, b_mod)


def _ln_mod_kernel(x_ref, g_ref, b_ref, mod_ref, u_ref, mu_ref, rstd_ref):
    gain = 1.0 + mod_ref[1:2, :]
    scale = g_ref[...] * gain
    shift = b_ref[...] * gain + mod_ref[0:1, :]

    def consume(rs, xh, mu, rstd):
        u_ref[rs, :] = (xh * scale + shift).astype(u_ref.dtype)
        mu_ref[rs, :] = jnp.broadcast_to(mu, (BF16_ROWS, LANES))
        rstd_ref[rs, :] = jnp.broadcast_to(rstd, (BF16_ROWS, LANES))

    _ln_rows(x_ref, consume)


def _ln_mod(x2, g, b, mod, seq, tr=512):
    t, d = x2.shape
    per = seq // tr
    stat = pl.BlockSpec((tr, LANES), lambda i: (i, 0))
    return pl.pallas_call(
        _ln_mod_kernel,
        grid=(t // tr,),
        in_specs=[pl.BlockSpec((tr, d), lambda i: (i, 0)),
                  pl.BlockSpec((1, d), lambda i: (0, 0)),
                  pl.BlockSpec((1, d), lambda i: (0, 0)),
                  pl.BlockSpec((None, 6, d), lambda i: (i // per, 0, 0))],
        out_specs=[pl.BlockSpec((tr, d), lambda i: (i, 0)), stat, stat],
        out_shape=[jax.ShapeDtypeStruct((t, d), BF16), jax.ShapeDtypeStruct((t, LANES), F32),
                   jax.ShapeDtypeStruct((t, LANES), F32)],
        compiler_params=_cparams(("parallel",), 48),
        name="ln_mod",
    )(x2, g, b, mod)


def _mm_kernel(*refs, relu2, convert):
    if convert:
        *refs, ci_ref, o_ref, co_ref = refs
        co_ref[...] = ci_ref[...].astype(co_ref.dtype)
        refs = (*refs, o_ref)
    *a_refs, b_ref, o_ref = refs
    acc, off = None, 0
    for a_ref in a_refs:
        kd = a_ref.shape[1]
        part = jnp.dot(a_ref[...], b_ref[off:off + kd, :].astype(BF16), preferred_element_type=F32)
        acc = part if acc is None else acc + part
        off += kd
    if relu2:
        acc = jnp.square(jnp.maximum(acc, 0.0))
    o_ref[...] = acc.astype(o_ref.dtype)


def _mm_acc_kernel(a_ref, b_ref, o_ref, acc_ref):
    k = pl.program_id(2)

    @pl.when(k == 0)
    def _():
        acc_ref[...] = jnp.zeros_like(acc_ref)

    acc_ref[...] += jnp.dot(a_ref[...], b_ref[...].astype(BF16), preferred_element_type=F32)

    @pl.when(k == pl.num_programs(2) - 1)
    def _():
        o_ref[...] = acc_ref[...].astype(o_ref.dtype)


def _matmul(a, b, *, tm, tn, tk=None, out_dtype=F32, relu2=False, convert=None, name="matmul"):
    a = a if isinstance(a, (list, tuple)) else [a]
    m = a[0].shape[0]
    kd, n = b.shape
    tm = min(tm, m)
    nj = n // tn
    osz = jnp.dtype(out_dtype).itemsize
    bsz = jnp.dtype(b.dtype).itemsize
    bsz = 2 * bsz + (2 if bsz == 4 else 0)
    if tk is None or tk >= kd:
        vm = (2 * tm * kd * 2 + kd * tn * bsz + 2 * tm * tn * osz + tm * tn * 4) / 2 ** 20 + 8
        in_specs = ([pl.BlockSpec((tm, x.shape[1]), lambda i, j: (i, 0)) for x in a]
                    + [pl.BlockSpec((kd, tn), lambda i, j: (0, j))])
        out_specs = pl.BlockSpec((tm, tn), lambda i, j: (i, j))
        out_shape = jax.ShapeDtypeStruct((m, n), out_dtype)
        args = (*a, b)
        if convert is not None:
            cr, cc = convert.shape
            slab = cr // ((m // tm) * nj)
            assert slab * (m // tm) * nj == cr and slab % BF16_ROWS == 0
            cspec = pl.BlockSpec((slab, cc), lambda i, j: (i * nj + j, 0))
            in_specs, args = in_specs + [cspec], (*args, convert)
            out_specs = [out_specs, cspec]
            out_shape = [out_shape, jax.ShapeDtypeStruct((cr, cc), BF16)]
            vm += slab * cc * 12 / 2 ** 20
        return pl.pallas_call(
            functools.partial(_mm_kernel, relu2=relu2, convert=convert is not None),
            grid=(m // tm, nj),
            in_specs=in_specs,
            out_specs=out_specs,
            out_shape=out_shape,
            compiler_params=_cparams(("parallel", "parallel"), vm),
            name=name,
        )(*args)
    assert not relu2 and len(a) == 1 and convert is None
    vm = (2 * tm * tk * 2 + tk * tn * bsz + 2 * tm * tn * osz + 2 * tm * tn * 4) / 2 ** 20 + 8
    return pl.pallas_call(
        _mm_acc_kernel,
        grid=(m // tm, nj, kd // tk),
        in_specs=[pl.BlockSpec((tm, tk), lambda i, j, k: (i, k)),
                  pl.BlockSpec((tk, tn), lambda i, j, k: (k, j))],
        out_specs=pl.BlockSpec((tm, tn), lambda i, j, k: (i, j)),
        out_shape=jax.ShapeDtypeStruct((m, n), out_dtype),
        scratch_shapes=[pltpu.VMEM((tm, tn), F32)],
        compiler_params=_cparams(("parallel", "parallel", "arbitrary"), vm),
        name=name,
    )(a[0], b)


def _scan_chunk(ins, hts, consts, tick):
    m0, strict, incl, eye, tri = consts

    ticking = [False]

    def each(f, *ls):
        out = [f(*a) for a in zip(*ls)]
        if ticking[0]:
            tick()
        return out

    r, k, v, al, be, lw = (list(z) for z in zip(*ins))

    def cumsum(x):
        hi = x.astype(BF16)
        rem = x - hi.astype(F32)
        mid = rem.astype(BF16)
        lo = (rem - mid.astype(F32)).astype(BF16)
        c3 = jnp.dot(tri, jnp.concatenate([hi, mid, lo], axis=1), preferred_element_type=F32)
        return c3[:, 0:PAIR] + c3[:, PAIR:2 * PAIR] + c3[:, 2 * PAIR:3 * PAIR]

    def sm(x):
        return jnp.concatenate([jnp.where(m0, x, 0.0), jnp.where(m0, 0.0, x)], axis=0)

    b16 = lambda t_: t_.astype(BF16)
    nn = lambda a_, b_: jnp.dot(a_, b_, preferred_element_type=F32)
    c = each(cumsum, lw)
    pc = each(lambda c_: jnp.exp(c_[CHUNK - 1:CHUNK, :]), c)
    einv = each(lambda c_: jnp.exp(-c_), c)
    a_sm = each(lambda a_, c_, l_: b16(sm(a_ * jnp.exp(c_ - l_))), al, c, lw)
    r_sm = each(lambda r_, c_: sm(r_ * jnp.exp(c_)), r, c)
    v_sm = each(lambda v_: b16(sm(v_)), v)
    b_t = each(lambda b_, e_: b_ * e_, be, einv)
    k_t = each(lambda k_, e_: k_ * e_, k, einv)

    def scores(a_, r_, b_, k_):
        bb, kb = b16(b_), b16(k_)
        return _dot_nt(jnp.concatenate([a_, b16(r_)], axis=0),
                       jnp.concatenate([bb, bb, kb, kb], axis=0))

    s = each(scores, a_sm, r_sm, b_t, k_t)
    lab = each(lambda s_: jnp.where(strict, s_[0:PAIR, 0:PAIR], 0.0), s)
    mak = each(lambda s_: b16(jnp.where(strict, s_[0:PAIR, PAIR:], 0.0)), s)
    incl2 = jnp.concatenate([incl, incl], axis=1)
    mrbk = each(lambda s_: b16(jnp.where(incl2, s_[PAIR:, :], 0.0)), s)

    ticking[0] = True
    ldt = each(lambda l_: l_.T, lab)
    xt = each(lambda l_: jnp.where(eye, 1.0, l_), ldt)
    lt = each(lambda l_: nn(b16(l_), b16(l_)), ldt)

    def series_step(l_, x_):
        lb = b16(l_)
        return nn(lb, jnp.concatenate([b16(x_), lb], axis=1))

    for _ in range(4):
        xl = each(series_step, lt, xt)
        xt = each(lambda x_, p_: x_ + p_[:, 0:PAIR], xt, xl)
        lt = each(lambda p_: p_[:, PAIR:], xl)
    x = each(lambda x_, l_: b16((x_ + nn(b16(l_), b16(x_))).T), xt, lt)

    makv = each(nn, mak, v_sm)
    wu = each(lambda x_, a_, m_: nn(x_, jnp.concatenate([a_, b16(m_)], axis=1)), x, a_sm, makv)
    bigr = each(lambda wu_, v_: jnp.concatenate(
        [b16(wu_), jnp.concatenate([jnp.zeros_like(v_), v_], axis=1)], axis=0), wu, v_sm)
    bk = each(lambda b_, k_, p_: b16(jnp.concatenate([sm(b_ * p_), sm(k_ * p_)], axis=0).T),
              b_t, k_t, pc)
    gz = each(nn, bk, bigr)
    qy = each(nn, mrbk, bigr)
    hb = each(b16, hts)

    def new_state(ht, h_, g_, p_):
        return ht * p_ + _dot_nt(h_, g_[:, 0:PAIR]) + g_[:, PAIR:].T

    def output(r_, q_, h_):
        y_sm = _dot_nt(r_ + q_[:, 0:PAIR], h_) + q_[:, PAIR:]
        return y_sm[0:CHUNK, :] + y_sm[CHUNK:, :]

    return each(output, r_sm, qy, hb), each(new_state, hts, hb, gz, pc)


def _rwkv_kernel(pr_ref, pk_ref, pv_ref, pc_ref, qr_ref, qk_ref, qv_ref, qc_ref,
                 mr_ref, mk_ref, mv_ref, mc_ref, w0_ref, a0_ref, kk_ref, ka_ref, wl_ref,
                 rk_ref, lg_ref, lb_ref, ci_ref, o_ref, co_ref,
                 h_ref, y_ref, r_s, k_s, v_s, al_s, be_s, lw_s, g_s, ud_s, *, nchunk):
    co_ref[...] = ci_ref[...].astype(co_ref.dtype)
    first = pl.program_id(1) == 0

    @pl.when(first)
    def _():
        h_ref[...] = jnp.zeros_like(h_ref)

    row = lax.broadcasted_iota(jnp.int32, (PAIR, PAIR), 0)
    col = lax.broadcasted_iota(jnp.int32, (PAIR, PAIR), 1)
    same = (row >> 6) == (col >> 6)
    tr_ = row & (CHUNK - 1)
    tc_ = col & (CHUNK - 1)
    strict = jnp.logical_and(same, tc_ < tr_)
    incl = jnp.logical_and(same, tc_ <= tr_)
    eye = row == col
    m0 = lax.broadcasted_iota(jnp.int32, (CHUNK, PAIR), 1) < HEAD
    tri = (lax.broadcasted_iota(jnp.int32, (CHUNK, CHUNK), 1)
           <= lax.broadcasted_iota(jnp.int32, (CHUNK, CHUNK), 0)).astype(BF16)
    consts = (m0, strict, incl, eye, tri)
    ones = _head_ones(2 * PAIR)
    slab = 4 * PAIR

    def prep_stages(cn):
        at_start = isinstance(cn, int)
        rows = pl.ds(0, CHUNK) if at_start else pl.ds(pl.multiple_of(cn * CHUNK, CHUNK), CHUNK)

        def shift(x_ref, q_ref, m_ref, ln):
            x = x_ref[rows, ln].astype(F32)
            if at_start:
                last = jnp.where(first, 0.0, q_ref[BF16_ROWS - 1:BF16_ROWS, ln].astype(F32))
            else:
                tail = pl.ds(pl.multiple_of(cn * CHUNK - BF16_ROWS, BF16_ROWS), BF16_ROWS)
                last = x_ref[tail, ln][BF16_ROWS - 1:BF16_ROWS, :].astype(F32)
            prev = pltpu.roll(x, 1, axis=0)
            r8 = lax.broadcasted_iota(jnp.int32, (8, x.shape[1]), 0)
            head = jnp.where(r8 == 0, jnp.broadcast_to(last, (8, x.shape[1])), prev[0:8, :])
            prev = jnp.concatenate([head, prev[8:, :]], axis=0)
            return x + (prev - x) * m_ref[:, ln]

        for j in range(D_A // slab):
            ln = slice(j * slab, (j + 1) * slab)
            r_s[rows, ln] = shift(pr_ref, qr_ref, mr_ref, ln)
            v_s[rows, ln] = shift(pv_ref, qv_ref, mv_ref, ln)
            yield
            k = shift(pk_ref, qk_ref, mk_ref, ln)
            kk = k * kk_ref[:, ln]
            kk2 = kk * kk
            ss = jnp.concatenate([_dot(kk2[:, i * 2 * PAIR:(i + 1) * 2 * PAIR], ones)
                                  for i in range(slab // (2 * PAIR))], axis=1)
            yield
            a = jax.nn.sigmoid(a0_ref[:, ln] + ud_s[rows, D_A + j * slab:D_A + (j + 1) * slab])
            kk = kk * lax.rsqrt(jnp.maximum(ss, 1e-24))
            k_s[rows, ln] = k * (1.0 + (a - 1.0) * ka_ref[:, ln])
            al_s[rows, ln] = -kk
            be_s[rows, ln] = kk * a
            lw_s[rows, ln] = -math.exp(-0.5) * jax.nn.sigmoid(w0_ref[:, ln] + ud_s[rows, ln])
            yield

    code = pc_ref[...].astype(F32)
    last = jnp.where(first, 0.0, qc_ref[BF16_ROWS - 1:BF16_ROWS, :].astype(F32))
    prev = pltpu.roll(code, 1, axis=0)
    r8 = lax.broadcasted_iota(jnp.int32, (8, LORA_COLS), 0)
    prev = jnp.concatenate([jnp.where(r8 == 0, jnp.broadcast_to(last, (8, LORA_COLS)), prev[0:8, :]),
                            prev[8:, :]], axis=0)
    code = code + (prev - code) * mc_ref[...]
    lane = lax.broadcasted_iota(jnp.int32, code.shape, 1)
    act = jnp.where(lane < DECAY_LORA, jnp.tanh(code),
                    jnp.where(lane < DECAY_LORA + ICLR_LORA, code, jax.nn.sigmoid(code))).astype(BF16)
    ud_s[:, 0:D_A] = jnp.dot(act, wl_ref[:, 0:D_A], preferred_element_type=F32)
    ud_s[:, D_A:] = jnp.dot(act, wl_ref[:, D_A:2 * D_A], preferred_element_type=F32)
    g_s[...] = jnp.dot(act, wl_ref[:, 2 * D_A:], preferred_element_type=F32)

    for _ in prep_stages(0):
        pass

    def chunk_body(ci, carry):
        rows = pl.ds(pl.multiple_of(ci * CHUNK, CHUNK), CHUNK)
        lanes = [slice(p * PAIR, (p + 1) * PAIR) for p in range(NPAIR)]
        ins = [tuple(s[rows, ln] for s in (r_s, k_s, v_s, al_s, be_s, lw_s)) for ln in lanes]
        hts = [h_ref[p] for p in range(NPAIR)]
        gen = prep_stages(jnp.minimum(ci + 1, nchunk - 1))
        ys, hns = _scan_chunk(ins, hts, consts, lambda: next(gen, None))
        for _ in gen:
            pass
        for p in range(NPAIR):
            h_ref[p] = hns[p]
            y_ref[rows, lanes[p]] = ys[p]
        return carry

    lax.fori_loop(0, nchunk, chunk_body, 0)

    tb = y_ref.shape[0]
    width = 2 * PAIR
    for half in range(2):
        lns = [slice(q * width, (q + 1) * width)
               for q in range(half * NPAIR // 4, (half + 1) * NPAIR // 4)]
        stack = lambda f: jnp.concatenate([f(ln) for ln in lns], axis=0)
        y = stack(lambda ln: y_ref[:, ln])
        sums = _dot(jnp.concatenate(
            [y, stack(lambda ln: r_s[:, ln] * k_s[:, ln] * rk_ref[:, ln])], axis=0), ones)
        yc = y - sums[0:len(lns) * tb] * (1.0 / HEAD)
        rstd = lax.rsqrt(_dot(yc * yc, ones) * (1.0 / HEAD) + LNX_EPS)
        for i, ln in enumerate(lns):
            rs = slice(i * tb, (i + 1) * tb)
            yn = yc[rs] * rstd[rs] * lg_ref[:, ln] + lb_ref[:, ln]
            rk = sums[len(lns) * tb + i * tb:len(lns) * tb + (i + 1) * tb]
            o_ref[:, ln] = ((yn + rk * v_s[:, ln]) * g_s[:, ln]).astype(o_ref.dtype)


def _rwkv(p, mu, w0, a0, k_k, k_a, w_lora, r_k, lnx_g, lnx_b, convert, bsz, seq, tb=256):
    t = p.shape[0]
    per = seq // tb
    slab = convert.shape[0] // (bsz * per)
    assert slab * bsz * per == convert.shape[0] and slab % BF16_ROWS == 0
    cspec = pl.BlockSpec((slab, convert.shape[1]), lambda b, s: (b * per + s, 0))
    nchunk = tb // CHUNK
    assert nchunk >= 2
    cblk = OFF_W // LORA_COLS
    tail = tb // BF16_ROWS

    def cur(j):
        return pl.BlockSpec((tb, D_A), lambda b, s: (b * per + s, j))

    def prev(j):
        return pl.BlockSpec((BF16_ROWS, D_A),
                            lambda b, s: (jnp.maximum((b * per + s) * tail - 1, 0), j))

    def vec(j):
        return pl.BlockSpec((1, D_A), lambda b, s: (0, j))

    in_specs = [cur(0), cur(1), cur(2),
                pl.BlockSpec((tb, LORA_COLS), lambda b, s: (b * per + s, cblk)),
                prev(0), prev(1), prev(2),
                pl.BlockSpec((BF16_ROWS, LORA_COLS),
                             lambda b, s: (jnp.maximum((b * per + s) * tail - 1, 0), cblk)),
                vec(0), vec(1), vec(2),
                pl.BlockSpec((1, LORA_COLS), lambda b, s: (0, cblk)),
                vec(0), vec(0), vec(0), vec(0),
                pl.BlockSpec((LORA_COLS, 3 * D_A), lambda b, s: (0, 0)),
                vec(0), vec(0), vec(0), cspec]
    blk = (tb, D_A)
    return pl.pallas_call(
        functools.partial(_rwkv_kernel, nchunk=nchunk),
        grid=(bsz, per),
        in_specs=in_specs,
        out_specs=[pl.BlockSpec(blk, lambda b, s: (b * per + s, 0)), cspec],
        out_shape=[jax.ShapeDtypeStruct((t, D_A), BF16), jax.ShapeDtypeStruct(convert.shape, BF16)],
        scratch_shapes=([pltpu.VMEM((NPAIR, PAIR, PAIR), F32)] + [pltpu.VMEM(blk, F32)] * 8
                        + [pltpu.VMEM((tb, 2 * D_A), F32)]),
        compiler_params=_cparams(("parallel", "arbitrary"), 52),
        name="rwkv",
    )(p, p, p, p, p, p, p, p, mu, mu, mu, mu, w0, a0, k_k, k_a, w_lora, r_k, lnx_g, lnx_b, convert)


def _bucket_table():
    qi = np.arange(BLOCK)[:, None]
    kj = np.arange(BLOCK)[None, :]
    n = np.where(kj > qi, qi + BLOCK - kj, qi - kj)
    assert WINDOW == BLOCK and n.min() >= 0 and n.max() < WINDOW
    nf = np.maximum(n, 1).astype(np.float32)
    large = RPB_MAX_EXACT + (np.log(nf / np.float32(RPB_MAX_EXACT))
                             / np.float32(math.log(RPB_MAX_DIST / RPB_MAX_EXACT))
                             * np.float32(RPB_BUCKETS - RPB_MAX_EXACT)).astype(np.int32)
    large = np.minimum(large, RPB_BUCKETS - 1)
    return np.where(n < RPB_MAX_EXACT, n, large).astype(np.int32)


def _bias_kernel(tab_ref, bkt_ref, o_ref):
    h0 = pl.program_id(0) * GQA
    bkt = bkt_ref[...]
    from_prev = (lax.broadcasted_iota(jnp.int32, bkt.shape, 1)
                 > lax.broadcasted_iota(jnp.int32, bkt.shape, 0))
    for i in range(GQA):
        acc = jnp.zeros(bkt.shape, F32)
        for b in range(RPB_BUCKETS):
            acc = jnp.where(bkt == b, tab_ref[b, h0 + i], acc)
        o_ref[0, i] = acc
        o_ref[1, i] = jnp.where(from_prev, NEG, acc)


def _attn_bias(rpb_table):
    bkt = jnp.asarray(_bucket_table())
    return pl.pallas_call(
        _bias_kernel,
        grid=(H_KV,),
        in_specs=[pl.BlockSpec(memory_space=pltpu.SMEM),
                  pl.BlockSpec((BLOCK, BLOCK), lambda g: (0, 0))],
        out_specs=pl.BlockSpec((2, GQA, BLOCK, BLOCK), lambda g: (0, g, 0, 0)),
        out_shape=jax.ShapeDtypeStruct((2, H_Q, BLOCK, BLOCK), F32),
        compiler_params=_cparams(("arbitrary",), 16),
        name="attn_bias",
    )(rpb_table, bkt)


def _swa_kernel(sink_ref, *refs):
    *q_refs, kc_ref, kp_ref, vc_ref, vp_ref, bias_ref, o_ref = refs
    each = lambda f, *ls: [f(*a) for a in zip(*ls)]
    lo = lax.broadcasted_iota(jnp.int32, (BLOCK, PAIR), 1) < HEAD
    from_prev = (lax.broadcasted_iota(jnp.int32, (BLOCK, BLOCK), 1)
                 > lax.broadcasted_iota(jnp.int32, (BLOCK, BLOCK), 0))
    from_prev2 = jnp.concatenate([from_prev, from_prev], axis=0)

    def exps(s_, m_):
        parts = []
        for sh, mh in zip(s_, m_):
            e_ = jnp.exp(sh - mh)
            parts += [jnp.where(from_prev, e_, 0.0), jnp.where(from_prev, 0.0, e_)]
        return jnp.concatenate(parts, axis=1).astype(BF16)

    scale = HEAD ** -0.5
    zeros = jnp.zeros((2 * BLOCK, HEAD), BF16)
    ones = jnp.ones((2 * BLOCK, HEAD), BF16)
    npr = GQA // 2
    kcat = jnp.concatenate([kp_ref[...], kc_ref[...]], axis=0).astype(F32) * scale
    vcat = jnp.concatenate([vp_ref[...], vc_ref[...]], axis=0).astype(F32)
    for g in range(H_KV):
        gsl = slice(g * HEAD, (g + 1) * HEAD)
        kg = kcat[:, gsl].astype(BF16)
        vg = vcat[:, gsl].astype(BF16)
        kdup = jnp.concatenate([kg, kg], axis=1)
        rhs = jnp.concatenate([jnp.concatenate([vg, zeros, ones, zeros], axis=1),
                               jnp.concatenate([zeros, vg, zeros, ones], axis=1)], axis=0)
        heads = [g * GQA + 2 * i for i in range(npr)]
        lanes = [slice(h * HEAD, (h + 2) * HEAD) for h in heads]
        qp = [q_refs[h // 4][:, (h % 4) * HEAD:(h % 4 + 2) * HEAD] for h in heads]
        s2 = each(lambda q_: _dot_nt(jnp.concatenate([jnp.where(lo, q_, 0.0).astype(BF16),
                                                      jnp.where(lo, 0.0, q_).astype(BF16)], axis=0),
                                     kdup), qp)
        sf = each(lambda s_: jnp.where(from_prev2, s_[:, 0:BLOCK], s_[:, BLOCK:]), s2)
        s = [(s_[0:BLOCK] + bias_ref[h], s_[BLOCK:] + bias_ref[h + 1]) for s_, h in zip(sf, heads)]
        m = [(jnp.maximum(jnp.max(a, axis=-1, keepdims=True), sink_ref[0, h]),
              jnp.maximum(jnp.max(b, axis=-1, keepdims=True), sink_ref[0, h + 1]))
             for (a, b), h in zip(s, heads)]
        e = each(exps, s, m)
        od = each(lambda e_: jnp.dot(e_, rhs, preferred_element_type=F32), e)
        for o_, m_, h, ln in zip(od, m, heads, lanes):
            den = o_[:, PAIR:] + jnp.where(lo, jnp.exp(sink_ref[0, h] - m_[0]),
                                           jnp.exp(sink_ref[0, h + 1] - m_[1]))
            o_ref[:, ln] = (o_[:, 0:PAIR] / den).astype(o_ref.dtype)


def _swa(p, bias, sinks, bsz, seq):
    t = p.shape[0]
    nb = seq // BLOCK
    kvw = H_KV * HEAD
    nq = D_B // kvw

    def cur(c):
        return pl.BlockSpec((BLOCK, kvw), lambda b, n: (b * nb + n, c))

    def prev(c):
        return pl.BlockSpec((BLOCK, kvw), lambda b, n: (b * nb + jnp.maximum(n - 1, 0), c))

    kb, vb = OFF_KB // kvw, OFF_VB // kvw
    return pl.pallas_call(
        _swa_kernel,
        grid=(bsz, nb),
        in_specs=[pl.BlockSpec(memory_space=pltpu.SMEM)]
        + [cur(OFF_Q // kvw + i) for i in range(nq)]
        + [cur(kb), prev(kb), cur(vb), prev(vb),
           pl.BlockSpec((None, H_Q, BLOCK, BLOCK),
                        lambda b, n: (jnp.where(n == 0, 1, 0), 0, 0, 0))],
        out_specs=pl.BlockSpec((BLOCK, D_B), lambda b, n: (b * nb + n, 0)),
        out_shape=jax.ShapeDtypeStruct((t, D_B), BF16),
        compiler_params=_cparams(("parallel", "arbitrary"), 32),
        name="swa",
    )(sinks, *([p] * (nq + 4)), bias)


def _post_mix_kernel(x_ref, mu_ref, rstd_ref, mix_ref, ge_ref, be_ref, g1_ref, b1_ref, mod_ref,
                     x1_ref, u_ref):
    wide = lambda s_: jnp.concatenate([s_] * (x_ref.shape[1] // LANES), axis=1)
    axn = ((x_ref[...] - wide(mu_ref[...])) * wide(rstd_ref[...]) * (ALPHA * ge_ref[...])
           + ALPHA * be_ref[...])
    z = axn + (1.0 + mod_ref[2:3, :]) * mix_ref[...].astype(F32)
    x1 = _layer_norm(z, g1_ref[...], b1_ref[...])
    x1_ref[...] = x1
    u_ref[...] = (x1 * (1.0 + mod_ref[4:5, :]) + mod_ref[3:4, :]).astype(u_ref.dtype)


def _post_mix(x2, mu, rstd, mix, ge, be, g1, b1, mod, seq, tr=256):
    t, d = x2.shape
    per = seq // tr
    row = pl.BlockSpec((tr, d), lambda i: (i, 0))
    stat = pl.BlockSpec((tr, LANES), lambda i: (i, 0))
    vec = pl.BlockSpec((1, d), lambda i: (0, 0))
    return pl.pallas_call(
        _post_mix_kernel,
        grid=(t // tr,),
        in_specs=[row, stat, stat, row, vec, vec, vec, vec,
                  pl.BlockSpec((None, 6, d), lambda i: (i // per, 0, 0))],
        out_specs=[row, row],
        out_shape=[jax.ShapeDtypeStruct((t, d), F32), jax.ShapeDtypeStruct((t, d), BF16)],
        compiler_params=_cparams(("parallel",), 52),
        name="post_mix",
    )(x2, mu, rstd, mix, ge, be, g1, b1, mod)


def _final_kernel(x1_ref, h_ref, g2_ref, b2_ref, mod_ref, o_ref):
    z = ALPHA * x1_ref[...] + (1.0 + mod_ref[5:6, :]) * h_ref[...].astype(F32)
    o_ref[...] = _layer_norm(z, g2_ref[...], b2_ref[...])


def _final(x1, h, g2, b2, mod, seq, tr=256):
    t, d = x1.shape
    per = seq // tr
    row = pl.BlockSpec((tr, d), lambda i: (i, 0))
    vec = pl.BlockSpec((1, d), lambda i: (0, 0))
    return pl.pallas_call(
        _final_kernel,
        grid=(t // tr,),
        in_specs=[row, row, vec, vec, pl.BlockSpec((None, 6, d), lambda i: (i // per, 0, 0))],
        out_specs=row,
        out_shape=jax.ShapeDtypeStruct((t, d), F32),
        compiler_params=_cparams(("parallel",), 48),
        name="final_ln",
    )(x1, h, g2, b2, mod)


def _lora_weights(w_decay_up, w_iclr_up, w_gate_up):
    zd = jnp.zeros((LORA_COLS, D_A), F32)
    wd = zd.at[0:DECAY_LORA].set(w_decay_up)
    wa = zd.at[DECAY_LORA:DECAY_LORA + ICLR_LORA].set(w_iclr_up)
    wg = zd.at[DECAY_LORA + ICLR_LORA:].set(w_gate_up)
    return jnp.concatenate([wd, wa, wg], axis=1).astype(BF16)


def kernel(x, c, ln_emb_g, ln_emb_b, rpb_table, w_mod, b_mod, w_in, mu_shift, w0, w_decay_up, a0,
           w_iclr_up, w_gate_up, k_k, k_a, r_k, lnx_g, lnx_b, attn_sinks, w_out, ln1_g, ln1_b,
           w_up, w_down, ln2_g, ln2_b):
    bsz, seq, d = x.shape
    assert w_mod.shape[0] == DEPTH == 1 and d == D_MODEL and bsz <= 8
    t = bsz * seq
    row = lambda a: a.reshape(1, -1)
    x2 = x.reshape(t, d)
    c8 = jnp.pad(c, ((0, 8 - bsz), (0, 0)))
    bias = _attn_bias(rpb_table)
    mod = _modulation(c8, w_mod[0], row(b_mod[0]))[:bsz].reshape(bsz, 6, d)
    u1, mu, rstd = _ln_mod(x2, row(ln_emb_g), row(ln_emb_b), mod, seq)
    p = _matmul(u1, w_in[0].astype(BF16), tm=1024, tn=1280, out_dtype=BF16, name="in_proj")
    w_lora = _lora_weights(w_decay_up[0], w_iclr_up[0], w_gate_up[0])
    y_a, wo = _rwkv(p, row(mu_shift[0]), row(w0[0]), row(a0[0]), row(k_k[0]), row(k_a[0]), w_lora,
                    row(r_k[0]), row(lnx_g[0]), row(lnx_b[0]), w_out[0], bsz, seq)
    y_b = _swa(p, bias, row(attn_sinks[0]), bsz, seq)
    mix, wu = _matmul([y_a, y_b], wo, tm=1024, tn=512, out_dtype=BF16, convert=w_up[0],
                      name="out_proj")
    x1, u2 = _post_mix(x2, mu, rstd, mix, row(ln_emb_g), row(ln_emb_b), row(ln1_g[0]),
                       row(ln1_b[0]), mod, seq)
    hmid, wd = _matmul(u2, wu, tm=1024, tn=1024, out_dtype=BF16, relu2=True,
                       convert=w_down[0], name="mlp_up")
    hout = _matmul(hmid, wd, tm=1024, tn=1024, tk=4096, out_dtype=BF16, name="mlp_down")
    out = _final(x1, hout, row(ln2_g[0]), row(ln2_b[0]), mod, seq)
    return out.reshape(bsz, seq, d)
```

```python
import functools
import math

import numpy as np
import jax
import jax.numpy as jnp
from jax import lax
from jax.experimental import pallas as pl
from jax.experimental.pallas import tpu as pltpu

F32 = jnp.float32
BF16 = jnp.bfloat16

D_MODEL = 4096
HEAD = 64
D_A = D_MODEL // 2
D_B = D_MODEL - D_A
H_A = D_A // HEAD
H_Q = D_B // HEAD
GQA = 8
H_KV = H_Q // GQA
WINDOW = 128
BLOCK = 128
RPB_BUCKETS = 32
RPB_MAX_EXACT = RPB_BUCKETS // 2
RPB_MAX_DIST = 128
DECAY_LORA = max(32, int(round(D_A ** 0.5 * 1.8 / 32)) * 32)
ICLR_LORA = max(32, int(round(D_A ** 0.5 * 1.8 / 32)) * 32)
GATE_LORA = max(32, int(round(D_A ** 0.6 * 0.8 / 32)) * 32)
LORA_COLS = DECAY_LORA + ICLR_LORA + GATE_LORA
D_FF = 4 * D_MODEL
DEPTH = 1
ALPHA = (2.0 * DEPTH) ** 0.25
LN_EPS = 1e-5
LNX_EPS = 64e-5
OFF_W = 3 * D_A
RWKV_COLS = OFF_W + LORA_COLS
OFF_Q = RWKV_COLS
OFF_KB = OFF_Q + D_B
OFF_VB = OFF_KB + H_KV * HEAD
N_IN = OFF_VB + H_KV * HEAD
NEG = -1e30

CHUNK = 64
PAIR = 2 * HEAD
NPAIR = H_A // 2
BF16_ROWS = 16
LANES = 128
VMEM_CAP = 56 * 1024 * 1024


def _cparams(sem, vmem_mb):
    return pltpu.CompilerParams(dimension_semantics=sem,
                                vmem_limit_bytes=min(int(vmem_mb * 1024 * 1024), VMEM_CAP))


def _dot(a, b):
    return jnp.dot(a.astype(BF16), b.astype(BF16), preferred_element_type=F32)


def _dot_nt(a, b):
    return lax.dot_general(a.astype(BF16), b.astype(BF16), (((1,), (1,)), ((), ())),
                           preferred_element_type=F32)


def _split2(x):
    hi = x.astype(BF16)
    lo = (x - hi.astype(F32)).astype(BF16)
    return hi, lo


def _layer_norm(x, g, b):
    mu = jnp.mean(x, axis=-1, keepdims=True)
    xc = x - mu
    var = jnp.mean(xc * xc, axis=-1, keepdims=True)
    return xc * lax.rsqrt(var + LN_EPS) * g + b


def _ln_rows(src_ref, consume):
    groups = [slice(g * BF16_ROWS, (g + 1) * BF16_ROWS) for g in range(src_ref.shape[0] // BF16_ROWS)]
    mus = [jnp.mean(src_ref[rs, :], axis=-1, keepdims=True) for rs in groups]
    rstd = [lax.rsqrt(jnp.mean(jnp.square(src_ref[rs, :] - mu), axis=-1, keepdims=True) + LN_EPS)
            for rs, mu in zip(groups, mus)]
    for rs, mu, r in zip(groups, mus, rstd):
        consume(rs, (src_ref[rs, :] - mu) * r, mu, r)


def _head_ones(n):
    r = lax.broadcasted_iota(jnp.int32, (n, n), 0)
    c = lax.broadcasted_iota(jnp.int32, (n, n), 1)
    return ((r >> 6) == (c >> 6)).astype(BF16)


def _mod_kernel(c_ref, w_ref, b_ref, o_ref):
    c = c_ref[...]
    cond = c * jax.nn.sigmoid(c)
    ch, cl = _split2(cond)
    wh, wl = _split2(w_ref[...])
    rows = c.shape[0]
    both = jnp.dot(jnp.concatenate([ch, cl], axis=0), wh, preferred_element_type=F32)
    o_ref[...] = (both[0:rows] + both[rows:] + jnp.dot(ch, wl, preferred_element_type=F32)
                  + b_ref[...])


def _modulation(c8, w_mod, b_mod, tn=1024):
    d, n = w_mod.shape
    return pl.pallas_call(
        _mod_kernel,
        grid=(n // tn,),
        in_specs=[pl.BlockSpec((8, d), lambda j: (0, 0)),
                  pl.BlockSpec((d, tn), lambda j: (0, j)),
                  pl.BlockSpec((1, tn), lambda j: (0, j))],
        out_specs=pl.BlockSpec((8, tn), lambda j: (0, j)),
        out_shape=jax.ShapeDtypeStruct((8, n), F32),
        compiler_params=_cparams(("parallel",), 56),
        name="modulation",
    )(c8, w_mod, b_mod)


def _ln_mod_kernel(x_ref, g_ref, b_ref, mod_ref, u_ref, mu_ref, rstd_ref):
    gain = 1.0 + mod_ref[1:2, :]
    scale = g_ref[...] * gain
    shift = b_ref[...] * gain + mod_ref[0:1, :]

    def consume(rs, xh, mu, rstd):
        u_ref[rs, :] = (xh * scale + shift).astype(u_ref.dtype)
        mu_ref[rs, :] = jnp.broadcast_to(mu, (BF16_ROWS, LANES))
        rstd_ref[rs, :] = jnp.broadcast_to(rstd, (BF16_ROWS, LANES))

    _ln_rows(x_ref, consume)


def _ln_mod(x2, g, b, mod, seq, tr=512):
    t, d = x2.shape
    per = seq // tr
    stat = pl.BlockSpec((tr, LANES), lambda i: (i, 0))
    return pl.pallas_call(
        _ln_mod_kernel,
        grid=(t // tr,),
        in_specs=[pl.BlockSpec((tr, d), lambda i: (i, 0)),
                  pl.BlockSpec((1, d), lambda i: (0, 0)),
                  pl.BlockSpec((1, d), lambda i: (0, 0)),
                  pl.BlockSpec((None, 6, d), lambda i: (i // per, 0, 0))],
        out_specs=[pl.BlockSpec((tr, d), lambda i: (i, 0)), stat, stat],
        out_shape=[jax.ShapeDtypeStruct((t, d), BF16), jax.ShapeDtypeStruct((t, LANES), F32),
                   jax.ShapeDtypeStruct((t, LANES), F32)],
        compiler_params=_cparams(("parallel",), 48),
        name="ln_mod",
    )(x2, g, b, mod)


def _mm_kernel(*refs, relu2, convert):
    if convert:
        *refs, ci_ref, o_ref, co_ref = refs
        co_ref[...] = ci_ref[...].astype(co_ref.dtype)
        refs = (*refs, o_ref)
    *a_refs, b_ref, o_ref = refs
    acc, off = None, 0
    for a_ref in a_refs:
        kd = a_ref.shape[1]
        part = jnp.dot(a_ref[...], b_ref[off:off + kd, :].astype(BF16), preferred_element_type=F32)
        acc = part if acc is None else acc + part
        off += kd
    if relu2:
        acc = jnp.square(jnp.maximum(acc, 0.0))
    o_ref[...] = acc.astype(o_ref.dtype)


def _mm_acc_kernel(a_ref, b_ref, o_ref, acc_ref):
    k = pl.program_id(2)

    @pl.when(k == 0)
    def _():
        acc_ref[...] = jnp.zeros_like(acc_ref)

    acc_ref[...] += jnp.dot(a_ref[...], b_ref[...].astype(BF16), preferred_element_type=F32)

    @pl.when(k == pl.num_programs(2) - 1)
    def _():
        o_ref[...] = acc_ref[...].astype(o_ref.dtype)


def _matmul(a, b, *, tm, tn, tk=None, out_dtype=F32, relu2=False, convert=None, name="matmul"):
    a = a if isinstance(a, (list, tuple)) else [a]
    m = a[0].shape[0]
    kd, n = b.shape
    tm = min(tm, m)
    nj = n // tn
    osz = jnp.dtype(out_dtype).itemsize
    bsz = jnp.dtype(b.dtype).itemsize
    bsz = 2 * bsz + (2 if bsz == 4 else 0)
    if tk is None or tk >= kd:
        vm = (2 * tm * kd * 2 + kd * tn * bsz + 2 * tm * tn * osz + tm * tn * 4) / 2 ** 20 + 8
        in_specs = ([pl.BlockSpec((tm, x.shape[1]), lambda i, j: (i, 0)) for x in a]
                    + [pl.BlockSpec((kd, tn), lambda i, j: (0, j))])
        out_specs = pl.BlockSpec((tm, tn), lambda i, j: (i, j))
        out_shape = jax.ShapeDtypeStruct((m, n), out_dtype)
        args = (*a, b)
        if convert is not None:
            cr, cc = convert.shape
            slab = cr // ((m // tm) * nj)
            assert slab * (m // tm) * nj == cr and slab % BF16_ROWS == 0
            cspec = pl.BlockSpec((slab, cc), lambda i, j: (i * nj + j, 0))
            in_specs, args = in_specs + [cspec], (*args, convert)
            out_specs = [out_specs, cspec]
            out_shape = [out_shape, jax.ShapeDtypeStruct((cr, cc), BF16)]
            vm += slab * cc * 12 / 2 ** 20
        return pl.pallas_call(
            functools.partial(_mm_kernel, relu2=relu2, convert=convert is not None),
            grid=(m // tm, nj),
            in_specs=in_specs,
            out_specs=out_specs,
            out_shape=out_shape,
            compiler_params=_cparams(("parallel", "parallel"), vm),
            name=name,
        )(*args)
    assert not relu2 and len(a) == 1 and convert is None
    vm = (2 * tm * tk * 2 + tk * tn * bsz + 2 * tm * tn * osz + 2 * tm * tn * 4) / 2 ** 20 + 8
    return pl.pallas_call(
        _mm_acc_kernel,
        grid=(m // tm, nj, kd // tk),
        in_specs=[pl.BlockSpec((tm, tk), lambda i, j, k: (i, k)),
                  pl.BlockSpec((tk, tn), lambda i, j, k: (k, j))],
        out_specs=pl.BlockSpec((tm, tn), lambda i, j, k: (i, j)),
        out_shape=jax.ShapeDtypeStruct((m, n), out_dtype),
        scratch_shapes=[pltpu.VMEM((tm, tn), F32)],
        compiler_params=_cparams(("parallel", "parallel", "arbitrary"), vm),
        name=name,
    )(a[0], b)


def _scan_chunk(ins, hts, consts, tick):
    m0, strict, incl, eye, tri = consts

    ticking = [False]

    def each(f, *ls):
        out = [f(*a) for a in zip(*ls)]
        if ticking[0]:
            tick()
        return out

    r, k, v, al, be, lw = (list(z) for z in zip(*ins))

    def cumsum(x):
        hi = x.astype(BF16)
        rem = x - hi.astype(F32)
        mid = rem.astype(BF16)
        lo = (rem - mid.astype(F32)).astype(BF16)
        c3 = jnp.dot(tri, jnp.concatenate([hi, mid, lo], axis=1), preferred_element_type=F32)
        return c3[:, 0:PAIR] + c3[:, PAIR:2 * PAIR] + c3[:, 2 * PAIR:3 * PAIR]

    def sm(x):
        return jnp.concatenate([jnp.where(m0, x, 0.0), jnp.where(m0, 0.0, x)], axis=0)

    b16 = lambda t_: t_.astype(BF16)
    nn = lambda a_, b_: jnp.dot(a_, b_, preferred_element_type=F32)
    c = each(cumsum, lw)
    pc = each(lambda c_: jnp.exp(c_[CHUNK - 1:CHUNK, :]), c)
    einv = each(lambda c_: jnp.exp(-c_), c)
    a_sm = each(lambda a_, c_, l_: b16(sm(a_ * jnp.exp(c_ - l_))), al, c, lw)
    r_sm = each(lambda r_, c_: sm(r_ * jnp.exp(c_)), r, c)
    v_sm = each(lambda v_: b16(sm(v_)), v)
    b_t = each(lambda b_, e_: b_ * e_, be, einv)
    k_t = each(lambda k_, e_: k_ * e_, k, einv)

    def scores(a_, r_, b_, k_):
        bb, kb = b16(b_), b16(k_)
        return _dot_nt(jnp.concatenate([a_, b16(r_)], axis=0),
                       jnp.concatenate([bb, bb, kb, kb], axis=0))

    s = each(scores, a_sm, r_sm, b_t, k_t)
    lab = each(lambda s_: jnp.where(strict, s_[0:PAIR, 0:PAIR], 0.0), s)
    mak = each(lambda s_: b16(jnp.where(strict, s_[0:PAIR, PAIR:], 0.0)), s)
    incl2 = jnp.concatenate([incl, incl], axis=1)
    mrbk = each(lambda s_: b16(jnp.where(incl2, s_[PAIR:, :], 0.0)), s)

    ticking[0] = True
    ldt = each(lambda l_: l_.T, lab)
    xt = each(lambda l_: jnp.where(eye, 1.0, l_), ldt)
    lt = each(lambda l_: nn(b16(l_), b16(l_)), ldt)

    def series_step(l_, x_):
        lb = b16(l_)
        return nn(lb, jnp.concatenate([b16(x_), lb], axis=1))

    for _ in range(4):
        xl = each(series_step, lt, xt)
        xt = each(lambda x_, p_: x_ + p_[:, 0:PAIR], xt, xl)
        lt = each(lambda p_: p_[:, PAIR:], xl)
    x = each(lambda x_, l_: b16((x_ + nn(b16(l_), b16(x_))).T), xt, lt)

    makv = each(nn, mak, v_sm)
    wu = each(lambda x_, a_, m_: nn(x_, jnp.concatenate([a_, b16(m_)], axis=1)), x, a_sm, makv)
    bigr = each(lambda wu_, v_: jnp.concatenate(
        [b16(wu_), jnp.concatenate([jnp.zeros_like(v_), v_], axis=1)], axis=0), wu, v_sm)
    bk = each(lambda b_, k_, p_: b16(jnp.concatenate([sm(b_ * p_), sm(k_ * p_)], axis=0).T),
              b_t, k_t, pc)
    gz = each(nn, bk, bigr)
    qy = each(nn, mrbk, bigr)
    hb = each(b16, hts)

    def new_state(ht, h_, g_, p_):
        return ht * p_ + _dot_nt(h_, g_[:, 0:PAIR]) + g_[:, PAIR:].T

    def output(r_, q_, h_):
        y_sm = _dot_nt(r_ + q_[:, 0:PAIR], h_) + q_[:, PAIR:]
        return y_sm[0:CHUNK, :] + y_sm[CHUNK:, :]

    return each(output, r_sm, qy, hb), each(new_state, hts, hb, gz, pc)


def _rwkv_kernel(pr_ref, pk_ref, pv_ref, pc_ref, qr_ref, qk_ref, qv_ref, qc_ref,
                 mr_ref, mk_ref, mv_ref, mc_ref, w0_ref, a0_ref, kk_ref, ka_ref, wl_ref,
                 rk_ref, lg_ref, lb_ref, ci_ref, o_ref, co_ref,
                 h_ref, y_ref, r_s, k_s, v_s, al_s, be_s, lw_s, g_s, ud_s, *, nchunk):
    co_ref[...] = ci_ref[...].astype(co_ref.dtype)
    first = pl.program_id(1) == 0

    @pl.when(first)
    def _():
        h_ref[...] = jnp.zeros_like(h_ref)

    row = lax.broadcasted_iota(jnp.int32, (PAIR, PAIR), 0)
    col = lax.broadcasted_iota(jnp.int32, (PAIR, PAIR), 1)
    same = (row >> 6) == (col >> 6)
    tr_ = row & (CHUNK - 1)
    tc_ = col & (CHUNK - 1)
    strict = jnp.logical_and(same, tc_ < tr_)
    incl = jnp.logical_and(same, tc_ <= tr_)
    eye = row == col
    m0 = lax.broadcasted_iota(jnp.int32, (CHUNK, PAIR), 1) < HEAD
    tri = (lax.broadcasted_iota(jnp.int32, (CHUNK, CHUNK), 1)
           <= lax.broadcasted_iota(jnp.int32, (CHUNK, CHUNK), 0)).astype(BF16)
    consts = (m0, strict, incl, eye, tri)
    ones = _head_ones(2 * PAIR)
    slab = 4 * PAIR

    def prep_stages(cn):
        at_start = isinstance(cn, int)
        rows = pl.ds(0, CHUNK) if at_start else pl.ds(pl.multiple_of(cn * CHUNK, CHUNK), CHUNK)

        def shift(x_ref, q_ref, m_ref, ln):
            x = x_ref[rows, ln].astype(F32)
            if at_start:
                last = jnp.where(first, 0.0, q_ref[BF16_ROWS - 1:BF16_ROWS, ln].astype(F32))
            else:
                tail = pl.ds(pl.multiple_of(cn * CHUNK - BF16_ROWS, BF16_ROWS), BF16_ROWS)
                last = x_ref[tail, ln][BF16_ROWS - 1:BF16_ROWS, :].astype(F32)
            prev = pltpu.roll(x, 1, axis=0)
            r8 = lax.broadcasted_iota(jnp.int32, (8, x.shape[1]), 0)
            head = jnp.where(r8 == 0, jnp.broadcast_to(last, (8, x.shape[1])), prev[0:8, :])
            prev = jnp.concatenate([head, prev[8:, :]], axis=0)
            return x + (prev - x) * m_ref[:, ln]

        for j in range(D_A // slab):
            ln = slice(j * slab, (j + 1) * slab)
            r_s[rows, ln] = shift(pr_ref, qr_ref, mr_ref, ln)
            v_s[rows, ln] = shift(pv_ref, qv_ref, mv_ref, ln)
            yield
            k = shift(pk_ref, qk_ref, mk_ref, ln)
            kk = k * kk_ref[:, ln]
            kk2 = kk * kk
            ss = jnp.concatenate([_dot(kk2[:, i * 2 * PAIR:(i + 1) * 2 * PAIR], ones)
                                  for i in range(slab // (2 * PAIR))], axis=1)
            yield
            a = jax.nn.sigmoid(a0_ref[:, ln] + ud_s[rows, D_A + j * slab:D_A + (j + 1) * slab])
            kk = kk * lax.rsqrt(jnp.maximum(ss, 1e-24))
            k_s[rows, ln] = k * (1.0 + (a - 1.0) * ka_ref[:, ln])
            al_s[rows, ln] = -kk
            be_s[rows, ln] = kk * a
            lw_s[rows, ln] = -math.exp(-0.5) * jax.nn.sigmoid(w0_ref[:, ln] + ud_s[rows, ln])
            yield

    code = pc_ref[...].astype(F32)
    last = jnp.where(first, 0.0, qc_ref[BF16_ROWS - 1:BF16_ROWS, :].astype(F32))
    prev = pltpu.roll(code, 1, axis=0)
    r8 = lax.broadcasted_iota(jnp.int32, (8, LORA_COLS), 0)
    prev = jnp.concatenate([jnp.where(r8 == 0, jnp.broadcast_to(last, (8, LORA_COLS)), prev[0:8, :]),
                            prev[8:, :]], axis=0)
    code = code + (prev - code) * mc_ref[...]
    lane = lax.broadcasted_iota(jnp.int32, code.shape, 1)
    act = jnp.where(lane < DECAY_LORA, jnp.tanh(code),
                    jnp.where(lane < DECAY_LORA + ICLR_LORA, code, jax.nn.sigmoid(code))).astype(BF16)
    ud_s[:, 0:D_A] = jnp.dot(act, wl_ref[:, 0:D_A], preferred_element_type=F32)
    ud_s[:, D_A:] = jnp.dot(act, wl_ref[:, D_A:2 * D_A], preferred_element_type=F32)
    g_s[...] = jnp.dot(act, wl_ref[:, 2 * D_A:], preferred_element_type=F32)

    for _ in prep_stages(0):
        pass

    def chunk_step(ci, gen):
        start = ci * CHUNK if isinstance(ci, int) else pl.multiple_of(ci * CHUNK, CHUNK)
        rows = pl.ds(start, CHUNK)
        lanes = [slice(p * PAIR, (p + 1) * PAIR) for p in range(NPAIR)]
        ins = [tuple(s[rows, ln] for s in (r_s, k_s, v_s, al_s, be_s, lw_s)) for ln in lanes]
        hts = [h_ref[p] for p in range(NPAIR)]
        ys, hns = _scan_chunk(ins, hts, consts, lambda: next(gen, None))
        for _ in gen:
            pass
        for p in range(NPAIR):
            h_ref[p] = hns[p]
            y_ref[rows, lanes[p]] = ys[p]

    def finish_stages(rows):
        n = rows.stop - rows.start
        width = 2 * PAIR
        for half in range(2):
            lns = [slice(q * width, (q + 1) * width)
                   for q in range(half * NPAIR // 4, (half + 1) * NPAIR // 4)]
            stack = lambda f: jnp.concatenate([f(ln) for ln in lns], axis=0)
            y = stack(lambda ln: y_ref[rows, ln])
            sums = _dot(jnp.concatenate(
                [y, stack(lambda ln: r_s[rows, ln] * k_s[rows, ln] * rk_ref[:, ln])], axis=0), ones)
            yield
            yc = y - sums[0:len(lns) * n] * (1.0 / HEAD)
            rstd = lax.rsqrt(_dot(yc * yc, ones) * (1.0 / HEAD) + LNX_EPS)
            yield
            for i, ln in enumerate(lns):
                rs = slice(i * n, (i + 1) * n)
                yn = yc[rs] * rstd[rs] * lg_ref[:, ln] + lb_ref[:, ln]
                rk = sums[len(lns) * n + i * n:len(lns) * n + (i + 1) * n]
                o_ref[rows, ln] = ((yn + rk * v_s[rows, ln]) * g_s[rows, ln]).astype(o_ref.dtype)
                yield

    def chunk_body(ci, carry):
        chunk_step(ci, prep_stages(ci + 1))
        return carry

    last = nchunk - 1
    lax.fori_loop(0, last, chunk_body, 0)
    chunk_step(last, finish_stages(slice(0, last * CHUNK)))
    for _ in finish_stages(slice(last * CHUNK, nchunk * CHUNK)):
        pass


def _rwkv(p, mu, w0, a0, k_k, k_a, w_lora, r_k, lnx_g, lnx_b, convert, bsz, seq, tb=256):
    t = p.shape[0]
    per = seq // tb
    slab = convert.shape[0] // (bsz * per)
    assert slab * bsz * per == convert.shape[0] and slab % BF16_ROWS == 0
    cspec = pl.BlockSpec((slab, convert.shape[1]), lambda b, s: (b * per + s, 0))
    nchunk = tb // CHUNK
    assert nchunk >= 2
    cblk = OFF_W // LORA_COLS
    tail = tb // BF16_ROWS

    def cur(j):
        return pl.BlockSpec((tb, D_A), lambda b, s: (b * per + s, j))

    def prev(j):
        return pl.BlockSpec((BF16_ROWS, D_A),
                            lambda b, s: (jnp.maximum((b * per + s) * tail - 1, 0), j))

    def vec(j):
        return pl.BlockSpec((1, D_A), lambda b, s: (0, j))

    in_specs = [cur(0), cur(1), cur(2),
                pl.BlockSpec((tb, LORA_COLS), lambda b, s: (b * per + s, cblk)),
                prev(0), prev(1), prev(2),
                pl.BlockSpec((BF16_ROWS, LORA_COLS),
                             lambda b, s: (jnp.maximum((b * per + s) * tail - 1, 0), cblk)),
                vec(0), vec(1), vec(2),
                pl.BlockSpec((1, LORA_COLS), lambda b, s: (0, cblk)),
                vec(0), vec(0), vec(0), vec(0),
                pl.BlockSpec((LORA_COLS, 3 * D_A), lambda b, s: (0, 0)),
                vec(0), vec(0), vec(0), cspec]
    blk = (tb, D_A)
    return pl.pallas_call(
        functools.partial(_rwkv_kernel, nchunk=nchunk),
        grid=(bsz, per),
        in_specs=in_specs,
        out_specs=[pl.BlockSpec(blk, lambda b, s: (b * per + s, 0)), cspec],
        out_shape=[jax.ShapeDtypeStruct((t, D_A), BF16), jax.ShapeDtypeStruct(convert.shape, BF16)],
        scratch_shapes=([pltpu.VMEM((NPAIR, PAIR, PAIR), F32)] + [pltpu.VMEM(blk, F32)] * 8
                        + [pltpu.VMEM((tb, 2 * D_A), F32)]),
        compiler_params=_cparams(("parallel", "arbitrary"), 56),
        name="rwkv",
    )(p, p, p, p, p, p, p, p, mu, mu, mu, mu, w0, a0, k_k, k_a, w_lora, r_k, lnx_g, lnx_b, convert)


def _bucket_table():
    qi = np.arange(BLOCK)[:, None]
    kj = np.arange(BLOCK)[None, :]
    n = np.where(kj > qi, qi + BLOCK - kj, qi - kj)
    assert WINDOW == BLOCK and n.min() >= 0 and n.max() < WINDOW
    nf = np.maximum(n, 1).astype(np.float32)
    large = RPB_MAX_EXACT + (np.log(nf / np.float32(RPB_MAX_EXACT))
                             / np.float32(math.log(RPB_MAX_DIST / RPB_MAX_EXACT))
                             * np.float32(RPB_BUCKETS - RPB_MAX_EXACT)).astype(np.int32)
    large = np.minimum(large, RPB_BUCKETS - 1)
    return np.where(n < RPB_MAX_EXACT, n, large).astype(np.int32)


def _bias_kernel(tab_ref, bkt_ref, o_ref):
    h0 = pl.program_id(0) * GQA
    bkt = bkt_ref[...]
    from_prev = (lax.broadcasted_iota(jnp.int32, bkt.shape, 1)
                 > lax.broadcasted_iota(jnp.int32, bkt.shape, 0))
    for i in range(GQA):
        acc = jnp.zeros(bkt.shape, F32)
        for b in range(RPB_BUCKETS):
            acc = jnp.where(bkt == b, tab_ref[b, h0 + i], acc)
        o_ref[0, i] = acc
        o_ref[1, i] = jnp.where(from_prev, NEG, acc)


def _attn_bias(rpb_table):
    bkt = jnp.asarray(_bucket_table())
    return pl.pallas_call(
        _bias_kernel,
        grid=(H_KV,),
        in_specs=[pl.BlockSpec(memory_space=pltpu.SMEM),
                  pl.BlockSpec((BLOCK, BLOCK), lambda g: (0, 0))],
        out_specs=pl.BlockSpec((2, GQA, BLOCK, BLOCK), lambda g: (0, g, 0, 0)),
        out_shape=jax.ShapeDtypeStruct((2, H_Q, BLOCK, BLOCK), F32),
        compiler_params=_cparams(("arbitrary",), 16),
        name="attn_bias",
    )(rpb_table, bkt)


def _swa_kernel(sink_ref, *refs):
    *q_refs, kc_ref, kp_ref, vc_ref, vp_ref, bias_ref, o_ref = refs
    each = lambda f, *ls: [f(*a) for a in zip(*ls)]
    lo = lax.broadcasted_iota(jnp.int32, (BLOCK, PAIR), 1) < HEAD
    from_prev = (lax.broadcasted_iota(jnp.int32, (BLOCK, BLOCK), 1)
                 > lax.broadcasted_iota(jnp.int32, (BLOCK, BLOCK), 0))
    from_prev2 = jnp.concatenate([from_prev, from_prev], axis=0)

    def exps(s_, m_):
        parts = []
        for sh, mh in zip(s_, m_):
            e_ = jnp.exp(sh - mh)
            parts += [jnp.where(from_prev, e_, 0.0), jnp.where(from_prev, 0.0, e_)]
        return jnp.concatenate(parts, axis=1).astype(BF16)

    scale = HEAD ** -0.5
    zeros = jnp.zeros((2 * BLOCK, HEAD), BF16)
    ones = jnp.ones((2 * BLOCK, HEAD), BF16)
    npr = GQA // 2
    kcat = jnp.concatenate([kp_ref[...], kc_ref[...]], axis=0).astype(F32) * scale
    vcat = jnp.concatenate([vp_ref[...], vc_ref[...]], axis=0).astype(F32)
    for g in range(H_KV):
        gsl = slice(g * HEAD, (g + 1) * HEAD)
        kg = kcat[:, gsl].astype(BF16)
        vg = vcat[:, gsl].astype(BF16)
        kdup = jnp.concatenate([kg, kg], axis=1)
        rhs = jnp.concatenate([jnp.concatenate([vg, zeros, ones, zeros], axis=1),
                               jnp.concatenate([zeros, vg, zeros, ones], axis=1)], axis=0)
        heads = [g * GQA + 2 * i for i in range(npr)]
        lanes = [slice(h * HEAD, (h + 2) * HEAD) for h in heads]
        qp = [q_refs[h // 4][:, (h % 4) * HEAD:(h % 4 + 2) * HEAD] for h in heads]
        s2 = each(lambda q_: _dot_nt(jnp.concatenate([jnp.where(lo, q_, 0.0).astype(BF16),
                                                      jnp.where(lo, 0.0, q_).astype(BF16)], axis=0),
                                     kdup), qp)
        sf = each(lambda s_: jnp.where(from_prev2, s_[:, 0:BLOCK], s_[:, BLOCK:]), s2)
        s = [(s_[0:BLOCK] + bias_ref[h], s_[BLOCK:] + bias_ref[h + 1]) for s_, h in zip(sf, heads)]
        m = [(jnp.maximum(jnp.max(a, axis=-1, keepdims=True), sink_ref[0, h]),
              jnp.maximum(jnp.max(b, axis=-1, keepdims=True), sink_ref[0, h + 1]))
             for (a, b), h in zip(s, heads)]
        e = each(exps, s, m)
        od = each(lambda e_: jnp.dot(e_, rhs, preferred_element_type=F32), e)
        for o_, m_, h, ln in zip(od, m, heads, lanes):
            den = o_[:, PAIR:] + jnp.where(lo, jnp.exp(sink_ref[0, h] - m_[0]),
                                           jnp.exp(sink_ref[0, h + 1] - m_[1]))
            o_ref[:, ln] = (o_[:, 0:PAIR] / den).astype(o_ref.dtype)


def _swa(p, bias, sinks, bsz, seq):
    t = p.shape[0]
    nb = seq // BLOCK
    kvw = H_KV * HEAD
    nq = D_B // kvw

    def cur(c):
        return pl.BlockSpec((BLOCK, kvw), lambda b, n: (b * nb + n, c))

    def prev(c):
        return pl.BlockSpec((BLOCK, kvw), lambda b, n: (b * nb + jnp.maximum(n - 1, 0), c))

    kb, vb = OFF_KB // kvw, OFF_VB // kvw
    return pl.pallas_call(
        _swa_kernel,
        grid=(bsz, nb),
        in_specs=[pl.BlockSpec(memory_space=pltpu.SMEM)]
        + [cur(OFF_Q // kvw + i) for i in range(nq)]
        + [cur(kb), prev(kb), cur(vb), prev(vb),
           pl.BlockSpec((None, H_Q, BLOCK, BLOCK),
                        lambda b, n: (jnp.where(n == 0, 1, 0), 0, 0, 0))],
        out_specs=pl.BlockSpec((BLOCK, D_B), lambda b, n: (b * nb + n, 0)),
        out_shape=jax.ShapeDtypeStruct((t, D_B), BF16),
        compiler_params=_cparams(("parallel", "arbitrary"), 32),
        name="swa",
    )(sinks, *([p] * (nq + 4)), bias)


def _post_mix_kernel(x_ref, mu_ref, rstd_ref, mix_ref, ge_ref, be_ref, g1_ref, b1_ref, mod_ref,
                     x1_ref, u_ref):
    wide = lambda s_: jnp.concatenate([s_] * (x_ref.shape[1] // LANES), axis=1)
    axn = ((x_ref[...] - wide(mu_ref[...])) * wide(rstd_ref[...]) * (ALPHA * ge_ref[...])
           + ALPHA * be_ref[...])
    z = axn + (1.0 + mod_ref[2:3, :]) * mix_ref[...].astype(F32)
    x1 = _layer_norm(z, g1_ref[...], b1_ref[...])
    x1_ref[...] = x1
    u_ref[...] = (x1 * (1.0 + mod_ref[4:5, :]) + mod_ref[3:4, :]).astype(u_ref.dtype)


def _post_mix(x2, mu, rstd, mix, ge, be, g1, b1, mod, seq, tr=256):
    t, d = x2.shape
    per = seq // tr
    row = pl.BlockSpec((tr, d), lambda i: (i, 0))
    stat = pl.BlockSpec((tr, LANES), lambda i: (i, 0))
    vec = pl.BlockSpec((1, d), lambda i: (0, 0))
    return pl.pallas_call(
        _post_mix_kernel,
        grid=(t // tr,),
        in_specs=[row, stat, stat, row, vec, vec, vec, vec,
                  pl.BlockSpec((None, 6, d), lambda i: (i // per, 0, 0))],
        out_specs=[row, row],
        out_shape=[jax.ShapeDtypeStruct((t, d), F32), jax.ShapeDtypeStruct((t, d), BF16)],
        compiler_params=_cparams(("parallel",), 52),
        name="post_mix",
    )(x2, mu, rstd, mix, ge, be, g1, b1, mod)


def _final_kernel(x1_ref, h_ref, g2_ref, b2_ref, mod_ref, o_ref):
    z = ALPHA * x1_ref[...] + (1.0 + mod_ref[5:6, :]) * h_ref[...].astype(F32)
    o_ref[...] = _layer_norm(z, g2_ref[...], b2_ref[...])


def _final(x1, h, g2, b2, mod, seq, tr=256):
    t, d = x1.shape
    per = seq // tr
    row = pl.BlockSpec((tr, d), lambda i: (i, 0))
    vec = pl.BlockSpec((1, d), lambda i: (0, 0))
    return pl.pallas_call(
        _final_kernel,
        grid=(t // tr,),
        in_specs=[row, row, vec, vec, pl.BlockSpec((None, 6, d), lambda i: (i // per, 0, 0))],
        out_specs=row,
        out_shape=jax.ShapeDtypeStruct((t, d), F32),
        compiler_params=_cparams(("parallel",), 48),
        name="final_ln",
    )(x1, h, g2, b2, mod)


def _lora_weights(w_decay_up, w_iclr_up, w_gate_up):
    zd = jnp.zeros((LORA_COLS, D_A), F32)
    wd = zd.at[0:DECAY_LORA].set(w_decay_up)
    wa = zd.at[DECAY_LORA:DECAY_LORA + ICLR_LORA].set(w_iclr_up)
    wg = zd.at[DECAY_LORA + ICLR_LORA:].set(w_gate_up)
    return jnp.concatenate([wd, wa, wg], axis=1).astype(BF16)


def kernel(x, c, ln_emb_g, ln_emb_b, rpb_table, w_mod, b_mod, w_in, mu_shift, w0, w_decay_up, a0,
           w_iclr_up, w_gate_up, k_k, k_a, r_k, lnx_g, lnx_b, attn_sinks, w_out, ln1_g, ln1_b,
           w_up, w_down, ln2_g, ln2_b):
    bsz, seq, d = x.shape
    assert w_mod.shape[0] == DEPTH == 1 and d == D_MODEL and bsz <= 8
    t = bsz * seq
    row = lambda a: a.reshape(1, -1)
    x2 = x.reshape(t, d)
    c8 = jnp.pad(c, ((0, 8 - bsz), (0, 0)))
    bias = _attn_bias(rpb_table)
    mod = _modulation(c8, w_mod[0], row(b_mod[0]))[:bsz].reshape(bsz, 6, d)
    u1, mu, rstd = _ln_mod(x2, row(ln_emb_g), row(ln_emb_b), mod, seq)
    p = _matmul(u1, w_in[0].astype(BF16), tm=1024, tn=1280, out_dtype=BF16, name="in_proj")
    w_lora = _lora_weights(w_decay_up[0], w_iclr_up[0], w_gate_up[0])
    y_a, wo = _rwkv(p, row(mu_shift[0]), row(w0[0]), row(a0[0]), row(k_k[0]), row(k_a[0]), w_lora,
                    row(r_k[0]), row(lnx_g[0]), row(lnx_b[0]), w_out[0], bsz, seq)
    y_b = _swa(p, bias, row(attn_sinks[0]), bsz, seq)
    mix, wu = _matmul([y_a, y_b], wo, tm=1024, tn=512, out_dtype=BF16, convert=w_up[0],
                      name="out_proj")
    x1, u2 = _post_mix(x2, mu, rstd, mix, row(ln_emb_g), row(ln_emb_b), row(ln1_g[0]),
                       row(ln1_b[0]), mod, seq)
    hmid, wd = _matmul(u2, wu, tm=1024, tn=1024, out_dtype=BF16, relu2=True,
                       convert=w_down[0], name="mlp_up")
    hout = _matmul(hmid, wd, tm=1024, tn=1024, tk=4096, out_dtype=BF16, name="mlp_down")
    out = _final(x1, hout, row(ln2_g[0]), row(ln2_b[0]), mod, seq)
    return out.reshape(bsz, seq, d)
```

```python
import functools
import math

import numpy as np
import jax
import jax.numpy as jnp
from jax import lax
from jax.experimental import pallas as pl
from jax.experimental.pallas import tpu as pltpu

F32 = jnp.float32
BF16 = jnp.bfloat16

D_MODEL = 4096
HEAD = 64
D_A = D_MODEL // 2
D_B = D_MODEL - D_A
H_A = D_A // HEAD
H_Q = D_B // HEAD
GQA = 8
H_KV = H_Q // GQA
WINDOW = 128
BLOCK = 128
RPB_BUCKETS = 32
RPB_MAX_EXACT = RPB_BUCKETS // 2
RPB_MAX_DIST = 128
DECAY_LORA = max(32, int(round(D_A ** 0.5 * 1.8 / 32)) * 32)
ICLR_LORA = max(32, int(round(D_A ** 0.5 * 1.8 / 32)) * 32)
GATE_LORA = max(32, int(round(D_A ** 0.6 * 0.8 / 32)) * 32)
LORA_COLS = DECAY_LORA + ICLR_LORA + GATE_LORA
D_FF = 4 * D_MODEL
DEPTH = 1
ALPHA = (2.0 * DEPTH) ** 0.25
LN_EPS = 1e-5
LNX_EPS = 64e-5
OFF_W = 3 * D_A
RWKV_COLS = OFF_W + LORA_COLS
OFF_Q = RWKV_COLS
OFF_KB = OFF_Q + D_B
OFF_VB = OFF_KB + H_KV * HEAD
N_IN = OFF_VB + H_KV * HEAD
NEG = -1e30

CHUNK = 64
PAIR = 2 * HEAD
NPAIR = H_A // 2
BF16_ROWS = 16
LANES = 128
VMEM_CAP = 56 * 1024 * 1024


def _cparams(sem, vmem_mb):
    return pltpu.CompilerParams(dimension_semantics=sem,
                                vmem_limit_bytes=min(int(vmem_mb * 1024 * 1024), VMEM_CAP))


def _dot(a, b):
    return jnp.dot(a.astype(BF16), b.astype(BF16), preferred_element_type=F32)


def _dot_nt(a, b):
    return lax.dot_general(a.astype(BF16), b.astype(BF16), (((1,), (1,)), ((), ())),
                           preferred_element_type=F32)


def _split2(x):
    hi = x.astype(BF16)
    lo = (x - hi.astype(F32)).astype(BF16)
    return hi, lo


def _layer_norm(x, g, b):
    mu = jnp.mean(x, axis=-1, keepdims=True)
    xc = x - mu
    var = jnp.mean(xc * xc, axis=-1, keepdims=True)
    return xc * lax.rsqrt(var + LN_EPS) * g + b


def _ln_rows(src_ref, consume):
    groups = [slice(g * BF16_ROWS, (g + 1) * BF16_ROWS) for g in range(src_ref.shape[0] // BF16_ROWS)]
    mus = [jnp.mean(src_ref[rs, :], axis=-1, keepdims=True) for rs in groups]
    rstd = [lax.rsqrt(jnp.mean(jnp.square(src_ref[rs, :] - mu), axis=-1, keepdims=True) + LN_EPS)
            for rs, mu in zip(groups, mus)]
    for rs, mu, r in zip(groups, mus, rstd):
        consume(rs, (src_ref[rs, :] - mu) * r, mu, r)


def _head_ones(n):
    r = lax.broadcasted_iota(jnp.int32, (n, n), 0)
    c = lax.broadcasted_iota(jnp.int32, (n, n), 1)
    return ((r >> 6) == (c >> 6)).astype(BF16)


def _mod_kernel(c_ref, w_ref, b_ref, o_ref):
    c = c_ref[...]
    cond = c * jax.nn.sigmoid(c)
    ch, cl = _split2(cond)
    wh, wl = _split2(w_ref[...])
    rows = c.shape[0]
    both = jnp.dot(jnp.concatenate([ch, cl], axis=0), wh, preferred_element_type=F32)
    o_ref[...] = (both[0:rows] + both[rows:] + jnp.dot(ch, wl, preferred_element_type=F32)
                  + b_ref[...])


def _modulation(c8, w_mod, b_mod, tn=1024):
    d, n = w_mod.shape
    return pl.pallas_call(
        _mod_kernel,
        grid=(n // tn,),
        in_specs=[pl.BlockSpec((8, d), lambda j: (0, 0)),
                  pl.BlockSpec((d, tn), lambda j: (0, j)),
                  pl.BlockSpec((1, tn), lambda j: (0, j))],
        out_specs=pl.BlockSpec((8, tn), lambda j: (0, j)),
        out_shape=jax.ShapeDtypeStruct((8, n), F32),
        compiler_params=_cparams(("parallel",), 56),
        name="modulation",
    )(c8, w_mod, b_mod)


def _ln_mod_kernel(x_ref, g_ref, b_ref, mod_ref, u_ref, mu_ref, rstd_ref):
    gain = 1.0 + mod_ref[1:2, :]
    scale = g_ref[...] * gain
    shift = b_ref[...] * gain + mod_ref[0:1, :]

    def consume(rs, xh, mu, rstd):
        u_ref[rs, :] = (xh * scale + shift).astype(u_ref.dtype)
        mu_ref[rs, :] = jnp.broadcast_to(mu, (BF16_ROWS, LANES))
        rstd_ref[rs, :] = jnp.broadcast_to(rstd, (BF16_ROWS, LANES))

    _ln_rows(x_ref, consume)


def _ln_mod(x2, g, b, mod, seq, tr=512):
    t, d = x2.shape
    per = seq // tr
    stat = pl.BlockSpec((tr, LANES), lambda i: (i, 0))
    return pl.pallas_call(
        _ln_mod_kernel,
        grid=(t // tr,),
        in_specs=[pl.BlockSpec((tr, d), lambda i: (i, 0)),
                  pl.BlockSpec((1, d), lambda i: (0, 0)),
                  pl.BlockSpec((1, d), lambda i: (0, 0)),
                  pl.BlockSpec((None, 6, d), lambda i: (i // per, 0, 0))],
        out_specs=[pl.BlockSpec((tr, d), lambda i: (i, 0)), stat, stat],
        out_shape=[jax.ShapeDtypeStruct((t, d), BF16), jax.ShapeDtypeStruct((t, LANES), F32),
                   jax.ShapeDtypeStruct((t, LANES), F32)],
        compiler_params=_cparams(("parallel",), 48),
        name="ln_mod",
    )(x2, g, b, mod)


def _mm_kernel(*refs, relu2, convert):
    if convert:
        *refs, ci_ref, o_ref, co_ref = refs
        co_ref[...] = ci_ref[...].astype(co_ref.dtype)
        refs = (*refs, o_ref)
    *a_refs, b_ref, o_ref = refs
    acc, off = None, 0
    for a_ref in a_refs:
        kd = a_ref.shape[1]
        part = jnp.dot(a_ref[...], b_ref[off:off + kd, :].astype(BF16), preferred_element_type=F32)
        acc = part if acc is None else acc + part
        off += kd
    if relu2:
        acc = jnp.square(jnp.maximum(acc, 0.0))
    o_ref[...] = acc.astype(o_ref.dtype)


def _mm_acc_kernel(a_ref, b_ref, o_ref, acc_ref):
    k = pl.program_id(2)

    @pl.when(k == 0)
    def _():
        acc_ref[...] = jnp.zeros_like(acc_ref)

    acc_ref[...] += jnp.dot(a_ref[...], b_ref[...].astype(BF16), preferred_element_type=F32)

    @pl.when(k == pl.num_programs(2) - 1)
    def _():
        o_ref[...] = acc_ref[...].astype(o_ref.dtype)


def _matmul(a, b, *, tm, tn, tk=None, out_dtype=F32, relu2=False, convert=None, name="matmul"):
    a = a if isinstance(a, (list, tuple)) else [a]
    m = a[0].shape[0]
    kd, n = b.shape
    tm = min(tm, m)
    nj = n // tn
    osz = jnp.dtype(out_dtype).itemsize
    bsz = jnp.dtype(b.dtype).itemsize
    bsz = 2 * bsz + (2 if bsz == 4 else 0)
    if tk is None or tk >= kd:
        vm = (2 * tm * kd * 2 + kd * tn * bsz + 2 * tm * tn * osz + tm * tn * 4) / 2 ** 20 + 8
        in_specs = ([pl.BlockSpec((tm, x.shape[1]), lambda i, j: (i, 0)) for x in a]
                    + [pl.BlockSpec((kd, tn), lambda i, j: (0, j))])
        out_specs = pl.BlockSpec((tm, tn), lambda i, j: (i, j))
        out_shape = jax.ShapeDtypeStruct((m, n), out_dtype)
        args = (*a, b)
        if convert is not None:
            cr, cc = convert.shape
            slab = cr // ((m // tm) * nj)
            assert slab * (m // tm) * nj == cr and slab % BF16_ROWS == 0
            cspec = pl.BlockSpec((slab, cc), lambda i, j: (i * nj + j, 0))
            in_specs, args = in_specs + [cspec], (*args, convert)
            out_specs = [out_specs, cspec]
            out_shape = [out_shape, jax.ShapeDtypeStruct((cr, cc), BF16)]
            vm += slab * cc * 12 / 2 ** 20
        return pl.pallas_call(
            functools.partial(_mm_kernel, relu2=relu2, convert=convert is not None),
            grid=(m // tm, nj),
            in_specs=in_specs,
            out_specs=out_specs,
            out_shape=out_shape,
            compiler_params=_cparams(("parallel", "parallel"), vm),
            name=name,
        )(*args)
    assert not relu2 and len(a) == 1 and convert is None
    vm = (2 * tm * tk * 2 + tk * tn * bsz + 2 * tm * tn * osz + 2 * tm * tn * 4) / 2 ** 20 + 8
    return pl.pallas_call(
        _mm_acc_kernel,
        grid=(m // tm, nj, kd // tk),
        in_specs=[pl.BlockSpec((tm, tk), lambda i, j, k: (i, k)),
                  pl.BlockSpec((tk, tn), lambda i, j, k: (k, j))],
        out_specs=pl.BlockSpec((tm, tn), lambda i, j, k: (i, j)),
        out_shape=jax.ShapeDtypeStruct((m, n), out_dtype),
        scratch_shapes=[pltpu.VMEM((tm, tn), F32)],
        compiler_params=_cparams(("parallel", "parallel", "arbitrary"), vm),
        name=name,
    )(a[0], b)


def _scan_chunk(ins, hts, consts, tick):
    m0, strict, incl, eye, tri = consts

    ticking = [False]

    def each(f, *ls):
        out = [f(*a) for a in zip(*ls)]
        if ticking[0]:
            tick()
        return out

    r, k, v, al, be, lw = (list(z) for z in zip(*ins))

    def cumsum(x):
        hi = x.astype(BF16)
        rem = x - hi.astype(F32)
        mid = rem.astype(BF16)
        lo = (rem - mid.astype(F32)).astype(BF16)
        c3 = jnp.dot(tri, jnp.concatenate([hi, mid, lo], axis=1), preferred_element_type=F32)
        return c3[:, 0:PAIR] + c3[:, PAIR:2 * PAIR] + c3[:, 2 * PAIR:3 * PAIR]

    def sm(x):
        return jnp.concatenate([jnp.where(m0, x, 0.0), jnp.where(m0, 0.0, x)], axis=0)

    b16 = lambda t_: t_.astype(BF16)
    nn = lambda a_, b_: jnp.dot(a_, b_, preferred_element_type=F32)
    c = each(cumsum, lw)
    pc = each(lambda c_: jnp.exp(c_[CHUNK - 1:CHUNK, :]), c)
    einv = each(lambda c_: jnp.exp(-c_), c)
    a_sm = each(lambda a_, c_, l_: b16(sm(a_ * jnp.exp(c_ - l_))), al, c, lw)
    r_sm = each(lambda r_, c_: sm(r_ * jnp.exp(c_)), r, c)
    v_sm = each(lambda v_: b16(sm(v_)), v)
    b_t = each(lambda b_, e_: b_ * e_, be, einv)
    k_t = each(lambda k_, e_: k_ * e_, k, einv)

    def scores(a_, r_, b_, k_):
        bb, kb = b16(b_), b16(k_)
        return _dot_nt(jnp.concatenate([a_, b16(r_)], axis=0),
                       jnp.concatenate([bb, bb, kb, kb], axis=0))

    s = each(scores, a_sm, r_sm, b_t, k_t)
    lab = each(lambda s_: jnp.where(strict, s_[0:PAIR, 0:PAIR], 0.0), s)
    mak = each(lambda s_: b16(jnp.where(strict, s_[0:PAIR, PAIR:], 0.0)), s)
    incl2 = jnp.concatenate([incl, incl], axis=1)
    mrbk = each(lambda s_: b16(jnp.where(incl2, s_[PAIR:, :], 0.0)), s)

    ticking[0] = True
    ldt = each(lambda l_: l_.T, lab)
    xt = each(lambda l_: jnp.where(eye, 1.0, l_), ldt)
    lt = each(lambda l_: nn(b16(l_), b16(l_)), ldt)

    def series_step(l_, x_):
        lb = b16(l_)
        return nn(lb, jnp.concatenate([b16(x_), lb], axis=1))

    for _ in range(4):
        xl = each(series_step, lt, xt)
        xt = each(lambda x_, p_: x_ + p_[:, 0:PAIR], xt, xl)
        lt = each(lambda p_: p_[:, PAIR:], xl)
    x = each(lambda x_, l_: b16((x_ + nn(b16(l_), b16(x_))).T), xt, lt)

    makv = each(nn, mak, v_sm)
    wu = each(lambda x_, a_, m_: nn(x_, jnp.concatenate([a_, b16(m_)], axis=1)), x, a_sm, makv)
    bigr = each(lambda wu_, v_: jnp.concatenate(
        [b16(wu_), jnp.concatenate([jnp.zeros_like(v_), v_], axis=1)], axis=0), wu, v_sm)
    bk = each(lambda b_, k_, p_: b16(jnp.concatenate([sm(b_ * p_), sm(k_ * p_)], axis=0).T),
              b_t, k_t, pc)
    gz = each(nn, bk, bigr)
    qy = each(nn, mrbk, bigr)
    hb = each(b16, hts)

    def new_state(ht, h_, g_, p_):
        return ht * p_ + _dot_nt(h_, g_[:, 0:PAIR]) + g_[:, PAIR:].T

    def output(r_, q_, h_):
        y_sm = _dot_nt(r_ + q_[:, 0:PAIR], h_) + q_[:, PAIR:]
        return y_sm[0:CHUNK, :] + y_sm[CHUNK:, :]

    return each(output, r_sm, qy, hb), each(new_state, hts, hb, gz, pc)


def _rwkv_kernel(pr_ref, pk_ref, pv_ref, pc_ref, qr_ref, qk_ref, qv_ref, qc_ref,
                 mr_ref, mk_ref, mv_ref, mc_ref, w0_ref, a0_ref, kk_ref, ka_ref, wl_ref,
                 rk_ref, lg_ref, lb_ref, ci_ref, o_ref, co_ref,
                 h_ref, y_ref, r_s, k_s, v_s, al_s, be_s, lw_s, g_s, ud_s, *, nchunk):
    co_ref[...] = ci_ref[...].astype(co_ref.dtype)
    first = pl.program_id(1) == 0

    @pl.when(first)
    def _():
        h_ref[...] = jnp.zeros_like(h_ref)

    row = lax.broadcasted_iota(jnp.int32, (PAIR, PAIR), 0)
    col = lax.broadcasted_iota(jnp.int32, (PAIR, PAIR), 1)
    same = (row >> 6) == (col >> 6)
    tr_ = row & (CHUNK - 1)
    tc_ = col & (CHUNK - 1)
    strict = jnp.logical_and(same, tc_ < tr_)
    incl = jnp.logical_and(same, tc_ <= tr_)
    eye = row == col
    m0 = lax.broadcasted_iota(jnp.int32, (CHUNK, PAIR), 1) < HEAD
    tri = (lax.broadcasted_iota(jnp.int32, (CHUNK, CHUNK), 1)
           <= lax.broadcasted_iota(jnp.int32, (CHUNK, CHUNK), 0)).astype(BF16)
    consts = (m0, strict, incl, eye, tri)
    ones = _head_ones(2 * PAIR)
    slab = 4 * PAIR

    def prep_stages(cn):
        at_start = isinstance(cn, int)
        rows = pl.ds(0, CHUNK) if at_start else pl.ds(pl.multiple_of(cn * CHUNK, CHUNK), CHUNK)

        def shift(x_ref, q_ref, m_ref, ln):
            x = x_ref[rows, ln].astype(F32)
            if at_start:
                last = jnp.where(first, 0.0, q_ref[BF16_ROWS - 1:BF16_ROWS, ln].astype(F32))
            else:
                tail = pl.ds(pl.multiple_of(cn * CHUNK - BF16_ROWS, BF16_ROWS), BF16_ROWS)
                last = x_ref[tail, ln][BF16_ROWS - 1:BF16_ROWS, :].astype(F32)
            prev = pltpu.roll(x, 1, axis=0)
            r8 = lax.broadcasted_iota(jnp.int32, (8, x.shape[1]), 0)
            head = jnp.where(r8 == 0, jnp.broadcast_to(last, (8, x.shape[1])), prev[0:8, :])
            prev = jnp.concatenate([head, prev[8:, :]], axis=0)
            return x + (prev - x) * m_ref[:, ln]

        for j in range(D_A // slab):
            ln = slice(j * slab, (j + 1) * slab)
            r_s[rows, ln] = shift(pr_ref, qr_ref, mr_ref, ln)
            v_s[rows, ln] = shift(pv_ref, qv_ref, mv_ref, ln)
            yield
            k = shift(pk_ref, qk_ref, mk_ref, ln)
            kk = k * kk_ref[:, ln]
            kk2 = kk * kk
            ss = jnp.concatenate([_dot(kk2[:, i * 2 * PAIR:(i + 1) * 2 * PAIR], ones)
                                  for i in range(slab // (2 * PAIR))], axis=1)
            yield
            a = jax.nn.sigmoid(a0_ref[:, ln] + ud_s[rows, D_A + j * slab:D_A + (j + 1) * slab])
            kk = kk * lax.rsqrt(jnp.maximum(ss, 1e-24))
            k_s[rows, ln] = k * (1.0 + (a - 1.0) * ka_ref[:, ln])
            al_s[rows, ln] = -kk
            be_s[rows, ln] = kk * a
            lw_s[rows, ln] = -math.exp(-0.5) * jax.nn.sigmoid(w0_ref[:, ln] + ud_s[rows, ln])
            yield

    code = pc_ref[...].astype(F32)
    last = jnp.where(first, 0.0, qc_ref[BF16_ROWS - 1:BF16_ROWS, :].astype(F32))
    prev = pltpu.roll(code, 1, axis=0)
    r8 = lax.broadcasted_iota(jnp.int32, (8, LORA_COLS), 0)
    prev = jnp.concatenate([jnp.where(r8 == 0, jnp.broadcast_to(last, (8, LORA_COLS)), prev[0:8, :]),
                            prev[8:, :]], axis=0)
    code = code + (prev - code) * mc_ref[...]
    lane = lax.broadcasted_iota(jnp.int32, code.shape, 1)
    act = jnp.where(lane < DECAY_LORA, jnp.tanh(code),
                    jnp.where(lane < DECAY_LORA + ICLR_LORA, code, jax.nn.sigmoid(code))).astype(BF16)
    ud_s[:, 0:D_A] = jnp.dot(act, wl_ref[:, 0:D_A], preferred_element_type=F32)
    ud_s[:, D_A:] = jnp.dot(act, wl_ref[:, D_A:2 * D_A], preferred_element_type=F32)
    g_s[...] = jnp.dot(act, wl_ref[:, 2 * D_A:], preferred_element_type=F32)

    for _ in prep_stages(0):
        pass

    def chunk_step(ci, gen):
        start = ci * CHUNK if isinstance(ci, int) else pl.multiple_of(ci * CHUNK, CHUNK)
        rows = pl.ds(start, CHUNK)
        lanes = [slice(p * PAIR, (p + 1) * PAIR) for p in range(NPAIR)]
        ins = [tuple(s[rows, ln] for s in (r_s, k_s, v_s, al_s, be_s, lw_s)) for ln in lanes]
        hts = [h_ref[p] for p in range(NPAIR)]
        ys, hns = _scan_chunk(ins, hts, consts, lambda: next(gen, None))
        for _ in gen:
            pass
        for p in range(NPAIR):
            h_ref[p] = hns[p]
            y_ref[rows, lanes[p]] = ys[p]

    def finish_stages(rows):
        n = rows.stop - rows.start
        width = 2 * PAIR
        for half in range(2):
            lns = [slice(q * width, (q + 1) * width)
                   for q in range(half * NPAIR // 4, (half + 1) * NPAIR // 4)]
            stack = lambda f: jnp.concatenate([f(ln) for ln in lns], axis=0)
            y = stack(lambda ln: y_ref[rows, ln])
            sums = _dot(jnp.concatenate(
                [y, stack(lambda ln: r_s[rows, ln] * k_s[rows, ln] * rk_ref[:, ln])], axis=0), ones)
            yield
            yc = y - sums[0:len(lns) * n] * (1.0 / HEAD)
            rstd = lax.rsqrt(_dot(yc * yc, ones) * (1.0 / HEAD) + LNX_EPS)
            yield
            for i, ln in enumerate(lns):
                rs = slice(i * n, (i + 1) * n)
                yn = yc[rs] * rstd[rs] * lg_ref[:, ln] + lb_ref[:, ln]
                rk = sums[len(lns) * n + i * n:len(lns) * n + (i + 1) * n]
                o_ref[rows, ln] = ((yn + rk * v_s[rows, ln]) * g_s[rows, ln]).astype(o_ref.dtype)
                yield

    last = nchunk - 1

    def chunk_body(ci, carry):
        chunk_step(ci, prep_stages(jnp.minimum(ci + 1, last)))
        return carry

    lax.fori_loop(0, last, chunk_body, 0)
    chunk_step(last, finish_stages(slice(0, last * CHUNK)))
    for _ in finish_stages(slice(last * CHUNK, nchunk * CHUNK)):
        pass


def _rwkv(p, mu, w0, a0, k_k, k_a, w_lora, r_k, lnx_g, lnx_b, convert, bsz, seq, tb=256):
    t = p.shape[0]
    per = seq // tb
    slab = convert.shape[0] // (bsz * per)
    assert slab * bsz * per == convert.shape[0] and slab % BF16_ROWS == 0
    cspec = pl.BlockSpec((slab, convert.shape[1]), lambda b, s: (b * per + s, 0))
    nchunk = tb // CHUNK
    assert nchunk >= 2
    cblk = OFF_W // LORA_COLS
    tail = tb // BF16_ROWS

    def cur(j):
        return pl.BlockSpec((tb, D_A), lambda b, s: (b * per + s, j))

    def prev(j):
        return pl.BlockSpec((BF16_ROWS, D_A),
                            lambda b, s: (jnp.maximum((b * per + s) * tail - 1, 0), j))

    def vec(j):
        return pl.BlockSpec((1, D_A), lambda b, s: (0, j))

    in_specs = [cur(0), cur(1), cur(2),
                pl.BlockSpec((tb, LORA_COLS), lambda b, s: (b * per + s, cblk)),
                prev(0), prev(1), prev(2),
                pl.BlockSpec((BF16_ROWS, LORA_COLS),
                             lambda b, s: (jnp.maximum((b * per + s) * tail - 1, 0), cblk)),
                vec(0), vec(1), vec(2),
                pl.BlockSpec((1, LORA_COLS), lambda b, s: (0, cblk)),
                vec(0), vec(0), vec(0), vec(0),
                pl.BlockSpec((LORA_COLS, 3 * D_A), lambda b, s: (0, 0)),
                vec(0), vec(0), vec(0), cspec]
    blk = (tb, D_A)
    return pl.pallas_call(
        functools.partial(_rwkv_kernel, nchunk=nchunk),
        grid=(bsz, per),
        in_specs=in_specs,
        out_specs=[pl.BlockSpec(blk, lambda b, s: (b * per + s, 0)), cspec],
        out_shape=[jax.ShapeDtypeStruct((t, D_A), BF16), jax.ShapeDtypeStruct(convert.shape, BF16)],
        scratch_shapes=([pltpu.VMEM((NPAIR, PAIR, PAIR), F32)] + [pltpu.VMEM(blk, F32)] * 8
                        + [pltpu.VMEM((tb, 2 * D_A), F32)]),
        compiler_params=_cparams(("parallel", "arbitrary"), 56),
        name="rwkv",
    )(p, p, p, p, p, p, p, p, mu, mu, mu, mu, w0, a0, k_k, k_a, w_lora, r_k, lnx_g, lnx_b, convert)


def _bucket_table():
    qi = np.arange(BLOCK)[:, None]
    kj = np.arange(BLOCK)[None, :]
    n = np.where(kj > qi, qi + BLOCK - kj, qi - kj)
    assert WINDOW == BLOCK and n.min() >= 0 and n.max() < WINDOW
    nf = np.maximum(n, 1).astype(np.float32)
    large = RPB_MAX_EXACT + (np.log(nf / np.float32(RPB_MAX_EXACT))
                             / np.float32(math.log(RPB_MAX_DIST / RPB_MAX_EXACT))
                             * np.float32(RPB_BUCKETS - RPB_MAX_EXACT)).astype(np.int32)
    large = np.minimum(large, RPB_BUCKETS - 1)
    return np.where(n < RPB_MAX_EXACT, n, large).astype(np.int32)


def _bias_kernel(tab_ref, bkt_ref, o_ref):
    h0 = pl.program_id(0) * GQA
    bkt = bkt_ref[...]
    from_prev = (lax.broadcasted_iota(jnp.int32, bkt.shape, 1)
                 > lax.broadcasted_iota(jnp.int32, bkt.shape, 0))
    for i in range(GQA):
        acc = jnp.zeros(bkt.shape, F32)
        for b in range(RPB_BUCKETS):
            acc = jnp.where(bkt == b, tab_ref[b, h0 + i], acc)
        o_ref[0, i] = acc
        o_ref[1, i] = jnp.where(from_prev, NEG, acc)


def _attn_bias(rpb_table):
    bkt = jnp.asarray(_bucket_table())
    return pl.pallas_call(
        _bias_kernel,
        grid=(H_KV,),
        in_specs=[pl.BlockSpec(memory_space=pltpu.SMEM),
                  pl.BlockSpec((BLOCK, BLOCK), lambda g: (0, 0))],
        out_specs=pl.BlockSpec((2, GQA, BLOCK, BLOCK), lambda g: (0, g, 0, 0)),
        out_shape=jax.ShapeDtypeStruct((2, H_Q, BLOCK, BLOCK), F32),
        compiler_params=_cparams(("arbitrary",), 16),
        name="attn_bias",
    )(rpb_table, bkt)


def _swa_kernel(sink_ref, *refs):
    *q_refs, kc_ref, kp_ref, vc_ref, vp_ref, bias_ref, o_ref = refs
    each = lambda f, *ls: [f(*a) for a in zip(*ls)]
    lo = lax.broadcasted_iota(jnp.int32, (BLOCK, PAIR), 1) < HEAD
    from_prev = (lax.broadcasted_iota(jnp.int32, (BLOCK, BLOCK), 1)
                 > lax.broadcasted_iota(jnp.int32, (BLOCK, BLOCK), 0))
    from_prev2 = jnp.concatenate([from_prev, from_prev], axis=0)

    def exps(s_, m_):
        parts = []
        for sh, mh in zip(s_, m_):
            e_ = jnp.exp(sh - mh)
            parts += [jnp.where(from_prev, e_, 0.0), jnp.where(from_prev, 0.0, e_)]
        return jnp.concatenate(parts, axis=1).astype(BF16)

    scale = HEAD ** -0.5
    zeros = jnp.zeros((2 * BLOCK, HEAD), BF16)
    ones = jnp.ones((2 * BLOCK, HEAD), BF16)
    npr = GQA // 2
    kcat = jnp.concatenate([kp_ref[...], kc_ref[...]], axis=0).astype(F32) * scale
    vcat = jnp.concatenate([vp_ref[...], vc_ref[...]], axis=0).astype(F32)
    for g in range(H_KV):
        gsl = slice(g * HEAD, (g + 1) * HEAD)
        kg = kcat[:, gsl].astype(BF16)
        vg = vcat[:, gsl].astype(BF16)
        kdup = jnp.concatenate([kg, kg], axis=1)
        rhs = jnp.concatenate([jnp.concatenate([vg, zeros, ones, zeros], axis=1),
                               jnp.concatenate([zeros, vg, zeros, ones], axis=1)], axis=0)
        heads = [g * GQA + 2 * i for i in range(npr)]
        lanes = [slice(h * HEAD, (h + 2) * HEAD) for h in heads]
        qp = [q_refs[h // 4][:, (h % 4) * HEAD:(h % 4 + 2) * HEAD] for h in heads]
        s2 = each(lambda q_: _dot_nt(jnp.concatenate([jnp.where(lo, q_, 0.0).astype(BF16),
                                                      jnp.where(lo, 0.0, q_).astype(BF16)], axis=0),
                                     kdup), qp)
        sf = each(lambda s_: jnp.where(from_prev2, s_[:, 0:BLOCK], s_[:, BLOCK:]), s2)
        s = [(s_[0:BLOCK] + bias_ref[h], s_[BLOCK:] + bias_ref[h + 1]) for s_, h in zip(sf, heads)]
        m = [(jnp.maximum(jnp.max(a, axis=-1, keepdims=True), sink_ref[0, h]),
              jnp.maximum(jnp.max(b, axis=-1, keepdims=True), sink_ref[0, h + 1]))
             for (a, b), h in zip(s, heads)]
        e = each(exps, s, m)
        od = each(lambda e_: jnp.dot(e_, rhs, preferred_element_type=F32), e)
        for o_, m_, h, ln in zip(od, m, heads, lanes):
            den = o_[:, PAIR:] + jnp.where(lo, jnp.exp(sink_ref[0, h] - m_[0]),
                                           jnp.exp(sink_ref[0, h + 1] - m_[1]))
            o_ref[:, ln] = (o_[:, 0:PAIR] / den).astype(o_ref.dtype)


def _swa(p, bias, sinks, bsz, seq):
    t = p.shape[0]
    nb = seq // BLOCK
    kvw = H_KV * HEAD
    nq = D_B // kvw

    def cur(c):
        return pl.BlockSpec((BLOCK, kvw), lambda b, n: (b * nb + n, c))

    def prev(c):
        return pl.BlockSpec((BLOCK, kvw), lambda b, n: (b * nb + jnp.maximum(n - 1, 0), c))

    kb, vb = OFF_KB // kvw, OFF_VB // kvw
    return pl.pallas_call(
        _swa_kernel,
        grid=(bsz, nb),
        in_specs=[pl.BlockSpec(memory_space=pltpu.SMEM)]
        + [cur(OFF_Q // kvw + i) for i in range(nq)]
        + [cur(kb), prev(kb), cur(vb), prev(vb),
           pl.BlockSpec((None, H_Q, BLOCK, BLOCK),
                        lambda b, n: (jnp.where(n == 0, 1, 0), 0, 0, 0))],
        out_specs=pl.BlockSpec((BLOCK, D_B), lambda b, n: (b * nb + n, 0)),
        out_shape=jax.ShapeDtypeStruct((t, D_B), BF16),
        compiler_params=_cparams(("parallel", "arbitrary"), 32),
        name="swa",
    )(sinks, *([p] * (nq + 4)), bias)


def _post_mix_kernel(x_ref, mu_ref, rstd_ref, mix_ref, ge_ref, be_ref, g1_ref, b1_ref, mod_ref,
                     x1_ref, u_ref):
    wide = lambda s_: jnp.concatenate([s_] * (x_ref.shape[1] // LANES), axis=1)
    axn = ((x_ref[...] - wide(mu_ref[...])) * wide(rstd_ref[...]) * (ALPHA * ge_ref[...])
           + ALPHA * be_ref[...])
    z = axn + (1.0 + mod_ref[2:3, :]) * mix_ref[...].astype(F32)
    x1 = _layer_norm(z, g1_ref[...], b1_ref[...])
    x1_ref[...] = x1
    u_ref[...] = (x1 * (1.0 + mod_ref[4:5, :]) + mod_ref[3:4, :]).astype(u_ref.dtype)


def _post_mix(x2, mu, rstd, mix, ge, be, g1, b1, mod, seq, tr=256):
    t, d = x2.shape
    per = seq // tr
    row = pl.BlockSpec((tr, d), lambda i: (i, 0))
    stat = pl.BlockSpec((tr, LANES), lambda i: (i, 0))
    vec = pl.BlockSpec((1, d), lambda i: (0, 0))
    return pl.pallas_call(
        _post_mix_kernel,
        grid=(t // tr,),
        in_specs=[row, stat, stat, row, vec, vec, vec, vec,
                  pl.BlockSpec((None, 6, d), lambda i: (i // per, 0, 0))],
        out_specs=[row, row],
        out_shape=[jax.ShapeDtypeStruct((t, d), F32), jax.ShapeDtypeStruct((t, d), BF16)],
        compiler_params=_cparams(("parallel",), 52),
        name="post_mix",
    )(x2, mu, rstd, mix, ge, be, g1, b1, mod)


def _final_kernel(x1_ref, h_ref, g2_ref, b2_ref, mod_ref, o_ref):
    z = ALPHA * x1_ref[...] + (1.0 + mod_ref[5:6, :]) * h_ref[...].astype(F32)
    o_ref[...] = _layer_norm(z, g2_ref[...], b2_ref[...])


def _final(x1, h, g2, b2, mod, seq, tr=256):
    t, d = x1.shape
    per = seq // tr
    row = pl.BlockSpec((tr, d), lambda i: (i, 0))
    vec = pl.BlockSpec((1, d), lambda i: (0, 0))
    return pl.pallas_call(
        _final_kernel,
        grid=(t // tr,),
        in_specs=[row, row, vec, vec, pl.BlockSpec((None, 6, d), lambda i: (i // per, 0, 0))],
        out_specs=row,
        out_shape=jax.ShapeDtypeStruct((t, d), F32),
        compiler_params=_cparams(("parallel",), 48),
        name="final_ln",
    )(x1, h, g2, b2, mod)


def _lora_weights(w_decay_up, w_iclr_up, w_gate_up):
    zd = jnp.zeros((LORA_COLS, D_A), F32)
    wd = zd.at[0:DECAY_LORA].set(w_decay_up)
    wa = zd.at[DECAY_LORA:DECAY_LORA + ICLR_LORA].set(w_iclr_up)
    wg = zd.at[DECAY_LORA + ICLR_LORA:].set(w_gate_up)
    return jnp.concatenate([wd, wa, wg], axis=1).astype(BF16)


def kernel(x, c, ln_emb_g, ln_emb_b, rpb_table, w_mod, b_mod, w_in, mu_shift, w0, w_decay_up, a0,
           w_iclr_up, w_gate_up, k_k, k_a, r_k, lnx_g, lnx_b, attn_sinks, w_out, ln1_g, ln1_b,
           w_up, w_down, ln2_g, ln2_b):
    bsz, seq, d = x.shape
    assert w_mod.shape[0] == DEPTH == 1 and d == D_MODEL and bsz <= 8
    t = bsz * seq
    row = lambda a: a.reshape(1, -1)
    x2 = x.reshape(t, d)
    c8 = jnp.pad(c, ((0, 8 - bsz), (0, 0)))
    bias = _attn_bias(rpb_table)
    mod = _modulation(c8, w_mod[0], row(b_mod[0]))[:bsz].reshape(bsz, 6, d)
    u1, mu, rstd = _ln_mod(x2, row(ln_emb_g), row(ln_emb_b), mod, seq)
    p = _matmul(u1, w_in[0].astype(BF16), tm=1024, tn=1280, out_dtype=BF16, name="in_proj")
    w_lora = _lora_weights(w_decay_up[0], w_iclr_up[0], w_gate_up[0])
    y_a, wo = _rwkv(p, row(mu_shift[0]), row(w0[0]), row(a0[0]), row(k_k[0]), row(k_a[0]), w_lora,
                    row(r_k[0]), row(lnx_g[0]), row(lnx_b[0]), w_out[0], bsz, seq)
    y_b = _swa(p, bias, row(attn_sinks[0]), bsz, seq)
    mix, wu = _matmul([y_a, y_b], wo, tm=1024, tn=512, out_dtype=BF16, convert=w_up[0],
                      name="out_proj")
    x1, u2 = _post_mix(x2, mu, rstd, mix, row(ln_emb_g), row(ln_emb_b), row(ln1_g[0]),
                       row(ln1_b[0]), mod, seq)
    hmid, wd = _matmul(u2, wu, tm=1024, tn=1024, out_dtype=BF16, relu2=True,
                       convert=w_down[0], name="mlp_up")
    hout = _matmul(hmid, wd, tm=1024, tn=1024, tk=4096, out_dtype=BF16, name="mlp_down")
    out = _final(x1, hout, row(ln2_g[0]), row(ln2_b[0]), mod, seq)
    return out.reshape(bsz, seq, d)
```

```python
import functools
import math

import numpy as np
import jax
import jax.numpy as jnp
from jax import lax
from jax.experimental import pallas as pl
from jax.experimental.pallas import tpu as pltpu

F32 = jnp.float32
BF16 = jnp.bfloat16

D_MODEL = 4096
HEAD = 64
D_A = D_MODEL // 2
D_B = D_MODEL - D_A
H_A = D_A // HEAD
H_Q = D_B // HEAD
GQA = 8
H_KV = H_Q // GQA
WINDOW = 128
BLOCK = 128
RPB_BUCKETS = 32
RPB_MAX_EXACT = RPB_BUCKETS // 2
RPB_MAX_DIST = 128
DECAY_LORA = max(32, int(round(D_A ** 0.5 * 1.8 / 32)) * 32)
ICLR_LORA = max(32, int(round(D_A ** 0.5 * 1.8 / 32)) * 32)
GATE_LORA = max(32, int(round(D_A ** 0.6 * 0.8 / 32)) * 32)
LORA_COLS = DECAY_LORA + ICLR_LORA + GATE_LORA
D_FF = 4 * D_MODEL
DEPTH = 1
ALPHA = (2.0 * DEPTH) ** 0.25
LN_EPS = 1e-5
LNX_EPS = 64e-5
OFF_W = 3 * D_A
RWKV_COLS = OFF_W + LORA_COLS
OFF_Q = RWKV_COLS
OFF_KB = OFF_Q + D_B
OFF_VB = OFF_KB + H_KV * HEAD
N_IN = OFF_VB + H_KV * HEAD
NEG = -1e30

CHUNK = 64
PAIR = 2 * HEAD
NPAIR = H_A // 2
BF16_ROWS = 16
LANES = 128
VMEM_CAP = 56 * 1024 * 1024


def _cparams(sem, vmem_mb):
    return pltpu.CompilerParams(dimension_semantics=sem,
                                vmem_limit_bytes=min(int(vmem_mb * 1024 * 1024), VMEM_CAP))


def _dot(a, b):
    return jnp.dot(a.astype(BF16), b.astype(BF16), preferred_element_type=F32)


def _dot_nt(a, b):
    return lax.dot_general(a.astype(BF16), b.astype(BF16), (((1,), (1,)), ((), ())),
                           preferred_element_type=F32)


def _split2(x):
    hi = x.astype(BF16)
    lo = (x - hi.astype(F32)).astype(BF16)
    return hi, lo


def _layer_norm(x, g, b):
    mu = jnp.mean(x, axis=-1, keepdims=True)
    xc = x - mu
    var = jnp.mean(xc * xc, axis=-1, keepdims=True)
    return xc * lax.rsqrt(var + LN_EPS) * g + b


def _ln_rows(src_ref, consume):
    groups = [slice(g * BF16_ROWS, (g + 1) * BF16_ROWS) for g in range(src_ref.shape[0] // BF16_ROWS)]
    mus = [jnp.mean(src_ref[rs, :], axis=-1, keepdims=True) for rs in groups]
    rstd = [lax.rsqrt(jnp.mean(jnp.square(src_ref[rs, :] - mu), axis=-1, keepdims=True) + LN_EPS)
            for rs, mu in zip(groups, mus)]
    for rs, mu, r in zip(groups, mus, rstd):
        consume(rs, (src_ref[rs, :] - mu) * r, mu, r)


def _head_ones(n):
    r = lax.broadcasted_iota(jnp.int32, (n, n), 0)
    c = lax.broadcasted_iota(jnp.int32, (n, n), 1)
    return ((r >> 6) == (c >> 6)).astype(BF16)


def _mod_kernel(c_ref, w_ref, b_ref, o_ref):
    c = c_ref[...]
    cond = c * jax.nn.sigmoid(c)
    ch, cl = _split2(cond)
    wh, wl = _split2(w_ref[...])
    rows = c.shape[0]
    both = jnp.dot(jnp.concatenate([ch, cl], axis=0), wh, preferred_element_type=F32)
    o_ref[...] = (both[0:rows] + both[rows:] + jnp.dot(ch, wl, preferred_element_type=F32)
                  + b_ref[...])


def _modulation(c8, w_mod, b_mod, tn=1024):
    d, n = w_mod.shape
    return pl.pallas_call(
        _mod_kernel,
        grid=(n // tn,),
        in_specs=[pl.BlockSpec((8, d), lambda j: (0, 0)),
                  pl.BlockSpec((d, tn), lambda j: (0, j)),
                  pl.BlockSpec((1, tn), lambda j: (0, j))],
        out_specs=pl.BlockSpec((8, tn), lambda j: (0, j)),
        out_shape=jax.ShapeDtypeStruct((8, n), F32),
        compiler_params=_cparams(("parallel",), 56),
        name="modulation",
    )(c8, w_mod, b_mod)


def _ln_mod_kernel(x_ref, g_ref, b_ref, mod_ref, u_ref, mu_ref, rstd_ref):
    gain = 1.0 + mod_ref[1:2, :]
    scale = g_ref[...] * gain
    shift = b_ref[...] * gain + mod_ref[0:1, :]

    def consume(rs, xh, mu, rstd):
        u_ref[rs, :] = (xh * scale + shift).astype(u_ref.dtype)
        mu_ref[rs, :] = jnp.broadcast_to(mu, (BF16_ROWS, LANES))
        rstd_ref[rs, :] = jnp.broadcast_to(rstd, (BF16_ROWS, LANES))

    _ln_rows(x_ref, consume)


def _ln_mod(x2, g, b, mod, seq, tr=512):
    t, d = x2.shape
    per = seq // tr
    stat = pl.BlockSpec((tr, LANES), lambda i: (i, 0))
    return pl.pallas_call(
        _ln_mod_kernel,
        grid=(t // tr,),
        in_specs=[pl.BlockSpec((tr, d), lambda i: (i, 0)),
                  pl.BlockSpec((1, d), lambda i: (0, 0)),
                  pl.BlockSpec((1, d), lambda i: (0, 0)),
                  pl.BlockSpec((None, 6, d), lambda i: (i // per, 0, 0))],
        out_specs=[pl.BlockSpec((tr, d), lambda i: (i, 0)), stat, stat],
        out_shape=[jax.ShapeDtypeStruct((t, d), BF16), jax.ShapeDtypeStruct((t, LANES), F32),
                   jax.ShapeDtypeStruct((t, LANES), F32)],
        compiler_params=_cparams(("parallel",), 48),
        name="ln_mod",
    )(x2, g, b, mod)


def _mm_kernel(*refs, relu2, convert):
    if convert:
        *refs, ci_ref, o_ref, co_ref = refs
        co_ref[...] = ci_ref[...].astype(co_ref.dtype)
        refs = (*refs, o_ref)
    *a_refs, b_ref, o_ref = refs
    acc, off = None, 0
    for a_ref in a_refs:
        kd = a_ref.shape[1]
        part = jnp.dot(a_ref[...], b_ref[off:off + kd, :].astype(BF16), preferred_element_type=F32)
        acc = part if acc is None else acc + part
        off += kd
    if relu2:
        acc = jnp.square(jnp.maximum(acc, 0.0))
    o_ref[...] = acc.astype(o_ref.dtype)


def _mm_acc_kernel(a_ref, b_ref, o_ref, acc_ref):
    k = pl.program_id(2)

    @pl.when(k == 0)
    def _():
        acc_ref[...] = jnp.zeros_like(acc_ref)

    acc_ref[...] += jnp.dot(a_ref[...], b_ref[...].astype(BF16), preferred_element_type=F32)

    @pl.when(k == pl.num_programs(2) - 1)
    def _():
        o_ref[...] = acc_ref[...].astype(o_ref.dtype)


def _matmul(a, b, *, tm, tn, tk=None, out_dtype=F32, relu2=False, convert=None, name="matmul"):
    a = a if isinstance(a, (list, tuple)) else [a]
    m = a[0].shape[0]
    kd, n = b.shape
    tm = min(tm, m)
    nj = n // tn
    osz = jnp.dtype(out_dtype).itemsize
    bsz = jnp.dtype(b.dtype).itemsize
    bsz = 2 * bsz + (2 if bsz == 4 else 0)
    if tk is None or tk >= kd:
        vm = (2 * tm * kd * 2 + kd * tn * bsz + 2 * tm * tn * osz + tm * tn * 4) / 2 ** 20 + 8
        in_specs = ([pl.BlockSpec((tm, x.shape[1]), lambda i, j: (i, 0)) for x in a]
                    + [pl.BlockSpec((kd, tn), lambda i, j: (0, j))])
        out_specs = pl.BlockSpec((tm, tn), lambda i, j: (i, j))
        out_shape = jax.ShapeDtypeStruct((m, n), out_dtype)
        args = (*a, b)
        if convert is not None:
            cr, cc = convert.shape
            slab = cr // ((m // tm) * nj)
            assert slab * (m // tm) * nj == cr and slab % BF16_ROWS == 0
            cspec = pl.BlockSpec((slab, cc), lambda i, j: (i * nj + j, 0))
            in_specs, args = in_specs + [cspec], (*args, convert)
            out_specs = [out_specs, cspec]
            out_shape = [out_shape, jax.ShapeDtypeStruct((cr, cc), BF16)]
            vm += slab * cc * 12 / 2 ** 20
        return pl.pallas_call(
            functools.partial(_mm_kernel, relu2=relu2, convert=convert is not None),
            grid=(m // tm, nj),
            in_specs=in_specs,
            out_specs=out_specs,
            out_shape=out_shape,
            compiler_params=_cparams(("parallel", "parallel"), vm),
            name=name,
        )(*args)
    assert not relu2 and len(a) == 1 and convert is None
    vm = (2 * tm * tk * 2 + tk * tn * bsz + 2 * tm * tn * osz + 2 * tm * tn * 4) / 2 ** 20 + 8
    return pl.pallas_call(
        _mm_acc_kernel,
        grid=(m // tm, nj, kd // tk),
        in_specs=[pl.BlockSpec((tm, tk), lambda i, j, k: (i, k)),
                  pl.BlockSpec((tk, tn), lambda i, j, k: (k, j))],
        out_specs=pl.BlockSpec((tm, tn), lambda i, j, k: (i, j)),
        out_shape=jax.ShapeDtypeStruct((m, n), out_dtype),
        scratch_shapes=[pltpu.VMEM((tm, tn), F32)],
        compiler_params=_cparams(("parallel", "parallel", "arbitrary"), vm),
        name=name,
    )(a[0], b)


def _scan_chunk(ins, hts, consts, tick):
    m0, strict, incl, eye, tri = consts

    ticking = [False]

    def each(f, *ls):
        out = [f(*a) for a in zip(*ls)]
        if ticking[0]:
            tick()
        return out

    r, k, v, al, be, lw = (list(z) for z in zip(*ins))

    def cumsum(x):
        hi = x.astype(BF16)
        rem = x - hi.astype(F32)
        mid = rem.astype(BF16)
        lo = (rem - mid.astype(F32)).astype(BF16)
        c3 = jnp.dot(tri, jnp.concatenate([hi, mid, lo], axis=1), preferred_element_type=F32)
        return c3[:, 0:PAIR] + c3[:, PAIR:2 * PAIR] + c3[:, 2 * PAIR:3 * PAIR]

    def sm(x):
        return jnp.concatenate([jnp.where(m0, x, 0.0), jnp.where(m0, 0.0, x)], axis=0)

    b16 = lambda t_: t_.astype(BF16)
    nn = lambda a_, b_: jnp.dot(a_, b_, preferred_element_type=F32)
    c = each(cumsum, lw)
    pc = each(lambda c_: jnp.exp(c_[CHUNK - 1:CHUNK, :]), c)
    einv = each(lambda c_: jnp.exp(-c_), c)
    a_sm = each(lambda a_, c_, l_: b16(sm(a_ * jnp.exp(c_ - l_))), al, c, lw)
    r_sm = each(lambda r_, c_: sm(r_ * jnp.exp(c_)), r, c)
    v_sm = each(lambda v_: b16(sm(v_)), v)
    b_t = each(lambda b_, e_: b_ * e_, be, einv)
    k_t = each(lambda k_, e_: k_ * e_, k, einv)

    def scores(a_, r_, b_, k_):
        bb, kb = b16(b_), b16(k_)
        return _dot_nt(jnp.concatenate([a_, b16(r_)], axis=0),
                       jnp.concatenate([bb, bb, kb, kb], axis=0))

    s = each(scores, a_sm, r_sm, b_t, k_t)
    lab = each(lambda s_: jnp.where(strict, s_[0:PAIR, 0:PAIR], 0.0), s)
    mak = each(lambda s_: b16(jnp.where(strict, s_[0:PAIR, PAIR:], 0.0)), s)
    incl2 = jnp.concatenate([incl, incl], axis=1)
    mrbk = each(lambda s_: b16(jnp.where(incl2, s_[PAIR:, :], 0.0)), s)

    ticking[0] = True
    ldt = each(lambda l_: l_.T, lab)
    xt = each(lambda l_: jnp.where(eye, 1.0, l_), ldt)
    lt = each(lambda l_: nn(b16(l_), b16(l_)), ldt)

    def series_step(l_, x_):
        lb = b16(l_)
        return nn(lb, jnp.concatenate([b16(x_), lb], axis=1))

    for _ in range(4):
        xl = each(series_step, lt, xt)
        xt = each(lambda x_, p_: x_ + p_[:, 0:PAIR], xt, xl)
        lt = each(lambda p_: p_[:, PAIR:], xl)
    x = each(lambda x_, l_: b16((x_ + nn(b16(l_), b16(x_))).T), xt, lt)

    makv = each(nn, mak, v_sm)
    wu = each(lambda x_, a_, m_: nn(x_, jnp.concatenate([a_, b16(m_)], axis=1)), x, a_sm, makv)
    bigr = each(lambda wu_, v_: jnp.concatenate(
        [b16(wu_), jnp.concatenate([jnp.zeros_like(v_), v_], axis=1)], axis=0), wu, v_sm)
    bk = each(lambda b_, k_, p_: b16(jnp.concatenate([sm(b_ * p_), sm(k_ * p_)], axis=0).T),
              b_t, k_t, pc)
    gz = each(nn, bk, bigr)
    qy = each(nn, mrbk, bigr)
    hb = each(b16, hts)

    def new_state(ht, h_, g_, p_):
        return ht * p_ + _dot_nt(h_, g_[:, 0:PAIR]) + g_[:, PAIR:].T

    def output(r_, q_, h_):
        y_sm = _dot_nt(r_ + q_[:, 0:PAIR], h_) + q_[:, PAIR:]
        return y_sm[0:CHUNK, :] + y_sm[CHUNK:, :]

    return each(output, r_sm, qy, hb), each(new_state, hts, hb, gz, pc)


def _rwkv_kernel(pr_ref, pk_ref, pv_ref, pc_ref, qr_ref, qk_ref, qv_ref, qc_ref,
                 mr_ref, mk_ref, mv_ref, mc_ref, w0_ref, a0_ref, kk_ref, ka_ref, wl_ref,
                 rk_ref, lg_ref, lb_ref, ci_ref, o_ref, co_ref,
                 h_ref, y_ref, r_s, k_s, v_s, al_s, be_s, lw_s, g_s, ud_s, *, nchunk):
    co_ref[...] = ci_ref[...].astype(co_ref.dtype)
    first = pl.program_id(1) == 0

    @pl.when(first)
    def _():
        h_ref[...] = jnp.zeros_like(h_ref)

    row = lax.broadcasted_iota(jnp.int32, (PAIR, PAIR), 0)
    col = lax.broadcasted_iota(jnp.int32, (PAIR, PAIR), 1)
    same = (row >> 6) == (col >> 6)
    tr_ = row & (CHUNK - 1)
    tc_ = col & (CHUNK - 1)
    strict = jnp.logical_and(same, tc_ < tr_)
    incl = jnp.logical_and(same, tc_ <= tr_)
    eye = row == col
    m0 = lax.broadcasted_iota(jnp.int32, (CHUNK, PAIR), 1) < HEAD
    tri = (lax.broadcasted_iota(jnp.int32, (CHUNK, CHUNK), 1)
           <= lax.broadcasted_iota(jnp.int32, (CHUNK, CHUNK), 0)).astype(BF16)
    consts = (m0, strict, incl, eye, tri)
    ones = _head_ones(2 * PAIR)
    slab = 4 * PAIR

    def prep_stages(cn):
        at_start = isinstance(cn, int)
        rows = pl.ds(0, CHUNK) if at_start else pl.ds(pl.multiple_of(cn * CHUNK, CHUNK), CHUNK)

        def shift(x_ref, q_ref, m_ref, ln):
            x = x_ref[rows, ln].astype(F32)
            if at_start:
                last = jnp.where(first, 0.0, q_ref[BF16_ROWS - 1:BF16_ROWS, ln].astype(F32))
            else:
                tail = pl.ds(pl.multiple_of(cn * CHUNK - BF16_ROWS, BF16_ROWS), BF16_ROWS)
                last = x_ref[tail, ln][BF16_ROWS - 1:BF16_ROWS, :].astype(F32)
            prev = pltpu.roll(x, 1, axis=0)
            r8 = lax.broadcasted_iota(jnp.int32, (8, x.shape[1]), 0)
            head = jnp.where(r8 == 0, jnp.broadcast_to(last, (8, x.shape[1])), prev[0:8, :])
            prev = jnp.concatenate([head, prev[8:, :]], axis=0)
            return x + (prev - x) * m_ref[:, ln]

        for j in range(D_A // slab):
            ln = slice(j * slab, (j + 1) * slab)
            r_s[rows, ln] = shift(pr_ref, qr_ref, mr_ref, ln)
            v_s[rows, ln] = shift(pv_ref, qv_ref, mv_ref, ln)
            yield
            k = shift(pk_ref, qk_ref, mk_ref, ln)
            kk = k * kk_ref[:, ln]
            kk2 = kk * kk
            ss = jnp.concatenate([_dot(kk2[:, i * 2 * PAIR:(i + 1) * 2 * PAIR], ones)
                                  for i in range(slab // (2 * PAIR))], axis=1)
            yield
            a = jax.nn.sigmoid(a0_ref[:, ln] + ud_s[rows, D_A + j * slab:D_A + (j + 1) * slab])
            kk = kk * lax.rsqrt(jnp.maximum(ss, 1e-24))
            k_s[rows, ln] = k * (1.0 + (a - 1.0) * ka_ref[:, ln])
            al_s[rows, ln] = -kk
            be_s[rows, ln] = kk * a
            lw_s[rows, ln] = -math.exp(-0.5) * jax.nn.sigmoid(w0_ref[:, ln] + ud_s[rows, ln])
            yield

    code = pc_ref[...].astype(F32)
    last = jnp.where(first, 0.0, qc_ref[BF16_ROWS - 1:BF16_ROWS, :].astype(F32))
    prev = pltpu.roll(code, 1, axis=0)
    r8 = lax.broadcasted_iota(jnp.int32, (8, LORA_COLS), 0)
    prev = jnp.concatenate([jnp.where(r8 == 0, jnp.broadcast_to(last, (8, LORA_COLS)), prev[0:8, :]),
                            prev[8:, :]], axis=0)
    code = code + (prev - code) * mc_ref[...]
    lane = lax.broadcasted_iota(jnp.int32, code.shape, 1)
    act = jnp.where(lane < DECAY_LORA, jnp.tanh(code),
                    jnp.where(lane < DECAY_LORA + ICLR_LORA, code, jax.nn.sigmoid(code))).astype(BF16)
    ud_s[:, 0:D_A] = jnp.dot(act, wl_ref[:, 0:D_A], preferred_element_type=F32)
    ud_s[:, D_A:] = jnp.dot(act, wl_ref[:, D_A:2 * D_A], preferred_element_type=F32)
    g_s[...] = jnp.dot(act, wl_ref[:, 2 * D_A:], preferred_element_type=F32)

    for _ in prep_stages(0):
        pass

    def chunk_step(ci, gen):
        start = ci * CHUNK if isinstance(ci, int) else pl.multiple_of(ci * CHUNK, CHUNK)
        rows = pl.ds(start, CHUNK)
        lanes = [slice(p * PAIR, (p + 1) * PAIR) for p in range(NPAIR)]
        ins = [tuple(s[rows, ln] for s in (r_s, k_s, v_s, al_s, be_s, lw_s)) for ln in lanes]
        hts = [h_ref[p] for p in range(NPAIR)]
        ys, hns = _scan_chunk(ins, hts, consts, lambda: next(gen, None))
        for _ in gen:
            pass
        for p in range(NPAIR):
            h_ref[p] = hns[p]
            y_ref[rows, lanes[p]] = ys[p]

    def finish_stages(rows):
        n = rows.stop - rows.start
        width = 2 * PAIR
        for half in range(2):
            lns = [slice(q * width, (q + 1) * width)
                   for q in range(half * NPAIR // 4, (half + 1) * NPAIR // 4)]
            stack = lambda f: jnp.concatenate([f(ln) for ln in lns], axis=0)
            y = stack(lambda ln: y_ref[rows, ln])
            sums = _dot(jnp.concatenate(
                [y, stack(lambda ln: r_s[rows, ln] * k_s[rows, ln] * rk_ref[:, ln])], axis=0), ones)
            yield
            yc = y - sums[0:len(lns) * n] * (1.0 / HEAD)
            rstd = lax.rsqrt(_dot(yc * yc, ones) * (1.0 / HEAD) + LNX_EPS)
            yield
            for i, ln in enumerate(lns):
                rs = slice(i * n, (i + 1) * n)
                yn = yc[rs] * rstd[rs] * lg_ref[:, ln] + lb_ref[:, ln]
                rk = sums[len(lns) * n + i * n:len(lns) * n + (i + 1) * n]
                o_ref[rows, ln] = ((yn + rk * v_s[rows, ln]) * g_s[rows, ln]).astype(o_ref.dtype)
                yield

    last = nchunk - 1

    def chunk_body(ci, carry):
        chunk_step(ci, prep_stages(jnp.minimum(ci + 1, last)))
        return carry

    lax.fori_loop(0, last, chunk_body, 0)
    chunk_step(last, finish_stages(slice(0, last * CHUNK)))
    for _ in finish_stages(slice(last * CHUNK, nchunk * CHUNK)):
        pass


def _rwkv(p, mu, w0, a0, k_k, k_a, w_lora, r_k, lnx_g, lnx_b, convert, bsz, seq, tb=256):
    t = p.shape[0]
    per = seq // tb
    slab = convert.shape[0] // (bsz * per)
    assert slab * bsz * per == convert.shape[0] and slab % BF16_ROWS == 0
    cspec = pl.BlockSpec((slab, convert.shape[1]), lambda b, s: (b * per + s, 0))
    nchunk = tb // CHUNK
    assert nchunk >= 2
    cblk = OFF_W // LORA_COLS
    tail = tb // BF16_ROWS

    def cur(j):
        return pl.BlockSpec((tb, D_A), lambda b, s: (b * per + s, j))

    def prev(j):
        return pl.BlockSpec((BF16_ROWS, D_A),
                            lambda b, s: (jnp.maximum((b * per + s) * tail - 1, 0), j))

    def vec(j):
        return pl.BlockSpec((1, D_A), lambda b, s: (0, j))

    in_specs = [cur(0), cur(1), cur(2),
                pl.BlockSpec((tb, LORA_COLS), lambda b, s: (b * per + s, cblk)),
                prev(0), prev(1), prev(2),
                pl.BlockSpec((BF16_ROWS, LORA_COLS),
                             lambda b, s: (jnp.maximum((b * per + s) * tail - 1, 0), cblk)),
                vec(0), vec(1), vec(2),
                pl.BlockSpec((1, LORA_COLS), lambda b, s: (0, cblk)),
                vec(0), vec(0), vec(0), vec(0),
                pl.BlockSpec((LORA_COLS, 3 * D_A), lambda b, s: (0, 0)),
                vec(0), vec(0), vec(0), cspec]
    blk = (tb, D_A)
    return pl.pallas_call(
        functools.partial(_rwkv_kernel, nchunk=nchunk),
        grid=(bsz, per),
        in_specs=in_specs,
        out_specs=[pl.BlockSpec(blk, lambda b, s: (b * per + s, 0)), cspec],
        out_shape=[jax.ShapeDtypeStruct((t, D_A), BF16), jax.ShapeDtypeStruct(convert.shape, BF16)],
        scratch_shapes=([pltpu.VMEM((NPAIR, PAIR, PAIR), F32)] + [pltpu.VMEM(blk, F32)] * 8
                        + [pltpu.VMEM((tb, 2 * D_A), F32)]),
        compiler_params=_cparams(("parallel", "arbitrary"), 56),
        name="rwkv",
    )(p, p, p, p, p, p, p, p, mu, mu, mu, mu, w0, a0, k_k, k_a, w_lora, r_k, lnx_g, lnx_b, convert)


def _bucket_table():
    qi = np.arange(BLOCK)[:, None]
    kj = np.arange(BLOCK)[None, :]
    n = np.where(kj > qi, qi + BLOCK - kj, qi - kj)
    assert WINDOW == BLOCK and n.min() >= 0 and n.max() < WINDOW
    nf = np.maximum(n, 1).astype(np.float32)
    large = RPB_MAX_EXACT + (np.log(nf / np.float32(RPB_MAX_EXACT))
                             / np.float32(math.log(RPB_MAX_DIST / RPB_MAX_EXACT))
                             * np.float32(RPB_BUCKETS - RPB_MAX_EXACT)).astype(np.int32)
    large = np.minimum(large, RPB_BUCKETS - 1)
    return np.where(n < RPB_MAX_EXACT, n, large).astype(np.int32)


def _bias_kernel(tab_ref, bkt_ref, o_ref):
    h0 = pl.program_id(0) * GQA
    bkt = bkt_ref[...]
    from_prev = (lax.broadcasted_iota(jnp.int32, bkt.shape, 1)
                 > lax.broadcasted_iota(jnp.int32, bkt.shape, 0))
    for i in range(GQA):
        acc = jnp.zeros(bkt.shape, F32)
        for b in range(RPB_BUCKETS):
            acc = jnp.where(bkt == b, tab_ref[b, h0 + i], acc)
        o_ref[0, i] = acc
        o_ref[1, i] = jnp.where(from_prev, NEG, acc)


def _attn_bias(rpb_table):
    bkt = jnp.asarray(_bucket_table())
    return pl.pallas_call(
        _bias_kernel,
        grid=(H_KV,),
        in_specs=[pl.BlockSpec(memory_space=pltpu.SMEM),
                  pl.BlockSpec((BLOCK, BLOCK), lambda g: (0, 0))],
        out_specs=pl.BlockSpec((2, GQA, BLOCK, BLOCK), lambda g: (0, g, 0, 0)),
        out_shape=jax.ShapeDtypeStruct((2, H_Q, BLOCK, BLOCK), F32),
        compiler_params=_cparams(("arbitrary",), 16),
        name="attn_bias",
    )(rpb_table, bkt)


def _swa_kernel(sink_ref, *refs):
    *q_refs, kc_ref, kp_ref, vc_ref, vp_ref, bias_ref, o_ref = refs
    each = lambda f, *ls: [f(*a) for a in zip(*ls)]
    lo = lax.broadcasted_iota(jnp.int32, (BLOCK, PAIR), 1) < HEAD
    from_prev = (lax.broadcasted_iota(jnp.int32, (BLOCK, BLOCK), 1)
                 > lax.broadcasted_iota(jnp.int32, (BLOCK, BLOCK), 0))
    from_prev2 = jnp.concatenate([from_prev, from_prev], axis=0)

    def exps(s_, m_):
        parts = []
        for sh, mh in zip(s_, m_):
            e_ = jnp.exp(sh - mh)
            parts += [jnp.where(from_prev, e_, 0.0), jnp.where(from_prev, 0.0, e_)]
        return jnp.concatenate(parts, axis=1).astype(BF16)

    scale = HEAD ** -0.5
    zeros = jnp.zeros((2 * BLOCK, HEAD), BF16)
    ones = jnp.ones((2 * BLOCK, HEAD), BF16)
    npr = GQA // 2
    kcat = jnp.concatenate([kp_ref[...], kc_ref[...]], axis=0).astype(F32) * scale
    vcat = jnp.concatenate([vp_ref[...], vc_ref[...]], axis=0).astype(F32)
    for g in range(H_KV):
        gsl = slice(g * HEAD, (g + 1) * HEAD)
        kg = kcat[:, gsl].astype(BF16)
        vg = vcat[:, gsl].astype(BF16)
        kdup = jnp.concatenate([kg, kg], axis=1)
        rhs = jnp.concatenate([jnp.concatenate([vg, zeros, ones, zeros], axis=1),
                               jnp.concatenate([zeros, vg, zeros, ones], axis=1)], axis=0)
        heads = [g * GQA + 2 * i for i in range(npr)]
        lanes = [slice(h * HEAD, (h + 2) * HEAD) for h in heads]
        qp = [q_refs[h // 4][:, (h % 4) * HEAD:(h % 4 + 2) * HEAD] for h in heads]
        s2 = each(lambda q_: _dot_nt(jnp.concatenate([jnp.where(lo, q_, 0.0).astype(BF16),
                                                      jnp.where(lo, 0.0, q_).astype(BF16)], axis=0),
                                     kdup), qp)
        sf = each(lambda s_: jnp.where(from_prev2, s_[:, 0:BLOCK], s_[:, BLOCK:]), s2)
        s = [(s_[0:BLOCK] + bias_ref[h], s_[BLOCK:] + bias_ref[h + 1]) for s_, h in zip(sf, heads)]
        m = [(jnp.maximum(jnp.max(a, axis=-1, keepdims=True), sink_ref[0, h]),
              jnp.maximum(jnp.max(b, axis=-1, keepdims=True), sink_ref[0, h + 1]))
             for (a, b), h in zip(s, heads)]
        e = each(exps, s, m)
        od = each(lambda e_: jnp.dot(e_, rhs, preferred_element_type=F32), e)
        for o_, m_, h, ln in zip(od, m, heads, lanes):
            den = o_[:, PAIR:] + jnp.where(lo, jnp.exp(sink_ref[0, h] - m_[0]),
                                           jnp.exp(sink_ref[0, h + 1] - m_[1]))
            o_ref[:, ln] = (o_[:, 0:PAIR] / den).astype(o_ref.dtype)


def _swa(p, bias, sinks, bsz, seq):
    t = p.shape[0]
    nb = seq // BLOCK
    kvw = H_KV * HEAD
    nq = D_B // kvw

    def cur(c):
        return pl.BlockSpec((BLOCK, kvw), lambda b, n: (b * nb + n, c))

    def prev(c):
        return pl.BlockSpec((BLOCK, kvw), lambda b, n: (b * nb + jnp.maximum(n - 1, 0), c))

    kb, vb = OFF_KB // kvw, OFF_VB // kvw
    return pl.pallas_call(
        _swa_kernel,
        grid=(bsz, nb),
        in_specs=[pl.BlockSpec(memory_space=pltpu.SMEM)]
        + [cur(OFF_Q // kvw + i) for i in range(nq)]
        + [cur(kb), prev(kb), cur(vb), prev(vb),
           pl.BlockSpec((None, H_Q, BLOCK, BLOCK),
                        lambda b, n: (jnp.where(n == 0, 1, 0), 0, 0, 0))],
        out_specs=pl.BlockSpec((BLOCK, D_B), lambda b, n: (b * nb + n, 0)),
        out_shape=jax.ShapeDtypeStruct((t, D_B), BF16),
        compiler_params=_cparams(("parallel", "arbitrary"), 32),
        name="swa",
    )(sinks, *([p] * (nq + 4)), bias)


def _post_mix_kernel(x_ref, mu_ref, rstd_ref, mix_ref, ge_ref, be_ref, g1_ref, b1_ref, mod_ref,
                     x1_ref, u_ref):
    wide = lambda s_: jnp.concatenate([s_] * (x_ref.shape[1] // LANES), axis=1)
    axn = ((x_ref[...] - wide(mu_ref[...])) * wide(rstd_ref[...]) * (ALPHA * ge_ref[...])
           + ALPHA * be_ref[...])
    z = axn + (1.0 + mod_ref[2:3, :]) * mix_ref[...].astype(F32)
    x1 = _layer_norm(z, g1_ref[...], b1_ref[...])
    x1_ref[...] = x1
    u_ref[...] = (x1 * (1.0 + mod_ref[4:5, :]) + mod_ref[3:4, :]).astype(u_ref.dtype)


def _post_mix(x2, mu, rstd, mix, ge, be, g1, b1, mod, seq, tr=256):
    t, d = x2.shape
    per = seq // tr
    row = pl.BlockSpec((tr, d), lambda i: (i, 0))
    stat = pl.BlockSpec((tr, LANES), lambda i: (i, 0))
    vec = pl.BlockSpec((1, d), lambda i: (0, 0))
    return pl.pallas_call(
        _post_mix_kernel,
        grid=(t // tr,),
        in_specs=[row, stat, stat, row, vec, vec, vec, vec,
                  pl.BlockSpec((None, 6, d), lambda i: (i // per, 0, 0))],
        out_specs=[row, row],
        out_shape=[jax.ShapeDtypeStruct((t, d), F32), jax.ShapeDtypeStruct((t, d), BF16)],
        compiler_params=_cparams(("parallel",), 52),
        name="post_mix",
    )(x2, mu, rstd, mix, ge, be, g1, b1, mod)


def _final_kernel(x1_ref, h_ref, g2_ref, b2_ref, mod_ref, o_ref):
    z = ALPHA * x1_ref[...] + (1.0 + mod_ref[5:6, :]) * h_ref[...].astype(F32)
    o_ref[...] = _layer_norm(z, g2_ref[...], b2_ref[...])


def _down_final_kernel(a_ref, b_ref, x1_ref, g2_ref, b2_ref, mod_ref, o_ref):
    k = pl.program_id(1)
    part = jnp.dot(a_ref[...], b_ref[...], preferred_element_type=F32)

    @pl.when(k == 0)
    def _():
        o_ref[...] = part

    @pl.when(k > 0)
    def _():
        o_ref[...] += part

    @pl.when(k == pl.num_programs(1) - 1)
    def _():
        scale = 1.0 + mod_ref[5:6, :]
        for g in range(o_ref.shape[0] // BF16_ROWS):
            rs = slice(g * BF16_ROWS, (g + 1) * BF16_ROWS)
            o_ref[rs, :] = ALPHA * x1_ref[rs, :] + scale * o_ref[rs, :]

        def consume(rs, xh, mu, rstd):
            o_ref[rs, :] = xh * g2_ref[...] + b2_ref[...]

        _ln_rows(o_ref, consume)


def _down_final(h, w, x1, g2, b2, mod, seq, tm=512, tk=512):
    t, kd = h.shape
    d = w.shape[1]
    per = seq // tm
    row = pl.BlockSpec((tm, d), lambda i, k: (i, 0))
    vec = pl.BlockSpec((1, d), lambda i, k: (0, 0))
    return pl.pallas_call(
        _down_final_kernel,
        grid=(t // tm, kd // tk),
        in_specs=[pl.BlockSpec((tm, tk), lambda i, k: (i, k)),
                  pl.BlockSpec((tk, d), lambda i, k: (k, 0)),
                  row, vec, vec, pl.BlockSpec((None, 6, d), lambda i, k: (i // per, 0, 0))],
        out_specs=row,
        out_shape=jax.ShapeDtypeStruct((t, d), F32),
        compiler_params=_cparams(("parallel", "arbitrary"), 56),
        name="mlp_down_final",
    )(h, w, x1, g2, b2, mod)


def _final(x1, h, g2, b2, mod, seq, tr=256):
    t, d = x1.shape
    per = seq // tr
    row = pl.BlockSpec((tr, d), lambda i: (i, 0))
    vec = pl.BlockSpec((1, d), lambda i: (0, 0))
    return pl.pallas_call(
        _final_kernel,
        grid=(t // tr,),
        in_specs=[row, row, vec, vec, pl.BlockSpec((None, 6, d), lambda i: (i // per, 0, 0))],
        out_specs=row,
        out_shape=jax.ShapeDtypeStruct((t, d), F32),
        compiler_params=_cparams(("parallel",), 48),
        name="final_ln",
    )(x1, h, g2, b2, mod)


def _lora_weights(w_decay_up, w_iclr_up, w_gate_up):
    zd = jnp.zeros((LORA_COLS, D_A), F32)
    wd = zd.at[0:DECAY_LORA].set(w_decay_up)
    wa = zd.at[DECAY_LORA:DECAY_LORA + ICLR_LORA].set(w_iclr_up)
    wg = zd.at[DECAY_LORA + ICLR_LORA:].set(w_gate_up)
    return jnp.concatenate([wd, wa, wg], axis=1).astype(BF16)


def kernel(x, c, ln_emb_g, ln_emb_b, rpb_table, w_mod, b_mod, w_in, mu_shift, w0, w_decay_up, a0,
           w_iclr_up, w_gate_up, k_k, k_a, r_k, lnx_g, lnx_b, attn_sinks, w_out, ln1_g, ln1_b,
           w_up, w_down, ln2_g, ln2_b):
    bsz, seq, d = x.shape
    assert w_mod.shape[0] == DEPTH == 1 and d == D_MODEL and bsz <= 8
    t = bsz * seq
    row = lambda a: a.reshape(1, -1)
    x2 = x.reshape(t, d)
    c8 = jnp.pad(c, ((0, 8 - bsz), (0, 0)))
    bias = _attn_bias(rpb_table)
    mod = _modulation(c8, w_mod[0], row(b_mod[0]))[:bsz].reshape(bsz, 6, d)
    u1, mu, rstd = _ln_mod(x2, row(ln_emb_g), row(ln_emb_b), mod, seq)
    p = _matmul(u1, w_in[0].astype(BF16), tm=1024, tn=1280, out_dtype=BF16, name="in_proj")
    w_lora = _lora_weights(w_decay_up[0], w_iclr_up[0], w_gate_up[0])
    y_a, wo = _rwkv(p, row(mu_shift[0]), row(w0[0]), row(a0[0]), row(k_k[0]), row(k_a[0]), w_lora,
                    row(r_k[0]), row(lnx_g[0]), row(lnx_b[0]), w_out[0], bsz, seq)
    y_b = _swa(p, bias, row(attn_sinks[0]), bsz, seq)
    mix, wu = _matmul([y_a, y_b], wo, tm=1024, tn=512, out_dtype=BF16, convert=w_up[0],
                      name="out_proj")
    x1, u2 = _post_mix(x2, mu, rstd, mix, row(ln_emb_g), row(ln_emb_b), row(ln1_g[0]),
                       row(ln1_b[0]), mod, seq)
    hmid, wd = _matmul(u2, wu, tm=1024, tn=1024, out_dtype=BF16, relu2=True,
                       convert=w_down[0], name="mlp_up")
    out = _down_final(hmid, wd, x1, row(ln2_g[0]), row(ln2_b[0]), mod, seq)
    return out.reshape(bsz, seq, d)
```
